```python
import jax, jax.numpy as jnp
from jax import lax
import numpy as np

D_MODEL = 1024
BATCH = 4
SEQ = 8192
DEPTH = 4

GRID_W = 64
CTX_LEN = 256
N_MIXERS = 4
D_FF = -(-8 * D_MODEL // (3 * 256)) * 256
DEEPNORM_ALPHA = (2 * DEPTH) ** 0.25
DEEPNORM_BETA = (8 * DEPTH) ** -0.25
LN_EPS = 1e-5
RMS_EPS = 1e-6
ROPE_BASE = 10000.0
CHUNK = 64
NEG_INF = -1e30

RET_HEADS = 4
RET_DK = D_MODEL // RET_HEADS
RET_DV = 2 * RET_DK
NA_HEADS = 16
NA_DH = D_MODEL // NA_HEADS
NA_WIN_ROWS = 8
NA_WIN_COLS = 16
NA_QBLOCK_W = 16
NA_BAND_W = NA_QBLOCK_W + NA_WIN_COLS
MLA_HEADS = 16
MLA_NOPE = 64
MLA_ROPE = 32
MLA_V = 64
MLA_Q_LORA = 512
MLA_KV_LORA = 256
MLA_QBLOCK = 128
HG_EXPAND = 128
HG_HEADS = D_MODEL // HG_EXPAND
HG_DI = D_MODEL // HG_HEADS
HG_FDIM = HG_HEADS * HG_EXPAND

kernel_name = 'hybrid_interleaved_flow_backbone'


def layer_norm(x, g, b):
    xf = x.astype(jnp.float32)
    xc = xf - jnp.mean(xf, -1, keepdims=True)
    var = jnp.mean(xc * xc, -1, keepdims=True)
    return (xc * lax.rsqrt(var + LN_EPS) * g.astype(jnp.float32) + b.astype(jnp.float32)).astype(x.dtype)


def rms_norm(x, g=None):
    xf = x.astype(jnp.float32)
    y = xf * lax.rsqrt(jnp.mean(xf * xf, -1, keepdims=True) + RMS_EPS)
    if g is not None:
        y = y * g.astype(jnp.float32)
    return y.astype(x.dtype)


def modulate(h, shift, scale):
    return h * (1 + scale) + shift


def axial_rope(n_tokens, rot_dim):
    t = jnp.arange(n_tokens)
    rows = (t // GRID_W).astype(jnp.float32)
    cols = (t % GRID_W).astype(jnp.float32)
    n_freq = rot_dim // 4
    inv = ROPE_BASE ** (-jnp.arange(n_freq, dtype=jnp.float32) / n_freq)
    ang = jnp.concatenate([rows[:, None] * inv, cols[:, None] * inv], -1)
    return jnp.cos(ang)[:, None, :], jnp.sin(ang)[:, None, :]


def apply_rope(x, cos, sin):
    x1, x2 = jnp.split(x.astype(jnp.float32), 2, axis=-1)
    return jnp.concatenate([x1 * cos - x2 * sin, x1 * sin + x2 * cos], -1).astype(x.dtype)


def chunk_gla(q, k, v, log_f, s0):
    B, H, L, dk = q.shape
    dv = v.shape[-1]
    n = L // CHUNK

    def to_chunks(a):
        return a.astype(jnp.float32).reshape(B, H, n, CHUNK, a.shape[-1]).transpose(2, 0, 1, 3, 4)

    mask = jnp.tril(jnp.ones((CHUNK, CHUNK), bool))

    def step(S, inp):
        qi, ki, vi, gi = inp
        b = jnp.cumsum(gi, axis=-2)
        b_last = b[..., -1:, :]
        q_d = qi * jnp.exp(b)
        att = jnp.where(mask, jnp.einsum('bhtk,bhsk->bhts', q_d, ki * jnp.exp(-b)), 0.0)
        o = jnp.einsum('bhts,bhsv->bhtv', att, vi) + jnp.einsum('bhtk,bhkv->bhtv', q_d, S)
        S_new = jnp.exp(b_last[..., 0, :])[..., None] * S + jnp.einsum('bhsk,bhsv->bhkv', ki * jnp.exp(b_last - b), vi)
        return S_new, o

    S_fin, oc = lax.scan(step, s0, (to_chunks(q), to_chunks(k), to_chunks(v), to_chunks(log_f)))
    return oc.transpose(1, 2, 0, 3, 4).reshape(B, H, L, dv), S_fin


def bidirectional_scan(ctx_terms, lat_terms):
    qc, kcf, kcb, vc, gcf, gcb = ctx_terms
    ql, klf, klb, vl, glf, glb = lat_terms
    B, H, _, dk = qc.shape
    s0 = jnp.zeros((B, H, dk, vc.shape[-1]), jnp.float32)
    flip = lambda t: jnp.flip(t, axis=2)
    o_cf, s_cf = chunk_gla(qc, kcf, vc, gcf, s0)
    o_cb, s_cb = chunk_gla(flip(qc), flip(kcb), flip(vc), flip(gcb), s0)
    o_lf, _ = chunk_gla(ql, klf, vl, glf, s_cf)
    o_lb, _ = chunk_gla(flip(ql), flip(klb), flip(vl), flip(glb), s_cb)
    return o_cf + flip(o_cb), o_lf + flip(o_lb)


def retention_mixer(a_ctx, a_lat, w_in, decay_param, w_out, want_ctx):
    B, L, _ = a_lat.shape
    hk, hv = RET_HEADS * RET_DK, RET_HEADS * RET_DV
    log_gamma = -jnp.exp(decay_param.astype(jnp.float32))

    def project(a, rope):
        n = a.shape[1]
        q, k, v, g = jnp.split(a @ w_in, [hk, 2 * hk, 2 * hk + hv], axis=-1)
        q = q.reshape(B, n, RET_HEADS, RET_DK)
        k = k.reshape(B, n, RET_HEADS, RET_DK)
        if rope is not None:
            q, k = apply_rope(q, *rope), apply_rope(k, *rope)
        k = k * RET_DK ** -0.5
        v = v.reshape(B, n, RET_HEADS, RET_DV)
        decay = lambda lg: jnp.broadcast_to(lg[None, :, None, None], (B, RET_HEADS, n, RET_DK))
        qh, kh, vh = (t.transpose(0, 2, 1, 3) for t in (q, k, v))
        return (qh, kh, kh, vh, decay(log_gamma[0]), decay(log_gamma[1])), g

    ctx_terms, g_ctx = project(a_ctx, None)
    lat_terms, g_lat = project(a_lat, axial_rope(L, RET_DK))
    o_ctx, o_lat = bidirectional_scan(ctx_terms, lat_terms)

    def readout(o, g):
        n = o.shape[2]
        y = rms_norm(o).transpose(0, 2, 1, 3).reshape(B, n, hv).astype(g.dtype)
        return (jax.nn.silu(g) * y) @ w_out

    return (readout(o_ctx, g_ctx) if want_ctx else None), readout(o_lat, g_lat)


def neighbourhood_mixer(a_ctx, a_lat, w_qkv, rpb, w_out, want_ctx):
    B, L, _ = a_lat.shape
    Lc = a_ctx.shape[1]
    rows = L // GRID_W
    wr = min(NA_WIN_ROWS, rows)
    scale = NA_DH ** -0.5
    q, k, v = jnp.split(a_lat @ w_qkv, 3, axis=-1)
    grid = lambda t: t.reshape(B, rows, GRID_W, NA_HEADS, NA_DH)
    q, k, v = grid(q * scale), grid(k), grid(v)
    qc, kc, vc = (t.reshape(B, Lc, NA_HEADS, NA_DH) for t in jnp.split(a_ctx @ w_qkv, 3, axis=-1))

    n_cb = GRID_W // NA_QBLOCK_W
    band0 = np.clip(np.arange(n_cb) * NA_QBLOCK_W - NA_WIN_COLS // 2, 0, GRID_W - NA_BAND_W)
    kcol = band0[:, None] + np.arange(NA_BAND_W)
    qcol = np.arange(GRID_W).reshape(n_cb, NA_QBLOCK_W)
    wstart = np.clip(qcol - NA_WIN_COLS // 2, 0, GRID_W - NA_WIN_COLS)
    kc3 = kcol[:, None, :]
    col_ok = (kc3 >= wstart[..., None]) & (kc3 < wstart[..., None] + NA_WIN_COLS)
    c_idx = np.clip(kc3 - qcol[..., None] + NA_WIN_COLS - 1, 0, 2 * NA_WIN_COLS - 2)

    def row(r):
        rs = jnp.clip(r - wr // 2, 0, rows - wr)
        qr = lax.dynamic_index_in_dim(q, r, axis=1, keepdims=False).reshape(B, n_cb, NA_QBLOCK_W, NA_HEADS, NA_DH)
        kb = lax.dynamic_slice_in_dim(k, rs, wr, axis=1)[:, :, kcol]
        vb = lax.dynamic_slice_in_dim(v, rs, wr, axis=1)[:, :, kcol]
        r_idx = rs + jnp.arange(wr) - r + NA_WIN_ROWS - 1
        bias = rpb[:, r_idx[None, None, :, None], c_idx[:, :, None, :]]
        s_lat = jnp.einsum('bnqhd,brnkhd->bhnqrk', qr, kb).astype(jnp.float32) + bias[None].astype(jnp.float32)
        s_lat = jnp.where(col_ok[:, :, None, :], s_lat, NEG_INF).reshape(B, NA_HEADS, n_cb, NA_QBLOCK_W, wr * NA_BAND_W)
        s_ctx = jnp.einsum('bnqhd,bchd->bhnqc', qr, kc).astype(jnp.float32)
        p = jax.nn.softmax(jnp.concatenate([s_lat, s_ctx], -1), -1).astype(v.dtype)
        p_lat = p[..., :wr * NA_BAND_W].reshape(B, NA_HEADS, n_cb, NA_QBLOCK_W, wr, NA_BAND_W)
        o = jnp.einsum('bhnqrk,brnkhd->bnqhd', p_lat, vb) + jnp.einsum('bhnqc,bchd->bnqhd', p[..., wr * NA_BAND_W:], vc)
        return o.reshape(B, GRID_W, NA_HEADS * NA_DH)

    o_lat = lax.map(row, jnp.arange(rows)).transpose(1, 0, 2, 3).reshape(B, L, NA_HEADS * NA_DH)
    y_ctx = None
    if want_ctx:
        s = jnp.einsum('bqhd,bkhd->bhqk', qc * scale, kc).astype(jnp.float32)
        p = jax.nn.softmax(s, -1).astype(vc.dtype)
        y_ctx = jnp.einsum('bhqk,bkhd->bqhd', p, vc).reshape(B, Lc, NA_HEADS * NA_DH) @ w_out
    return y_ctx, o_lat @ w_out


def mla_mixer(a_ctx, a_lat, w_down, q_norm, kv_norm, w_uq, w_ukv, w_out, want_ctx):
    B, L, _ = a_lat.shape
    scale = (MLA_NOPE + MLA_ROPE) ** -0.5

    def project(a, rope):
        n = a.shape[1]
        cq, ckv, kr = jnp.split(a @ w_down, [MLA_Q_LORA, MLA_Q_LORA + MLA_KV_LORA], axis=-1)
        q = (rms_norm(cq, q_norm) @ w_uq).reshape(B, n, MLA_HEADS, MLA_NOPE + MLA_ROPE)
        kv = (rms_norm(ckv, kv_norm) @ w_ukv).reshape(B, n, MLA_HEADS, MLA_NOPE + MLA_V)
        q_nope, q_rope = jnp.split(q, [MLA_NOPE], axis=-1)
        k_nope, v = jnp.split(kv, [MLA_NOPE], axis=-1)
        kr = kr[:, :, None, :]
        if rope is not None:
            q_rope, kr = apply_rope(q_rope, *rope), apply_rope(kr, *rope)
        q = jnp.concatenate([q_nope, q_rope], -1) * scale
        k = jnp.concatenate([k_nope, jnp.broadcast_to(kr, (B, n, MLA_HEADS, MLA_ROPE))], -1)
        return q, k, v

    qc, kc, vc = project(a_ctx, None)
    ql, kl, vl = project(a_lat, axial_rope(L, MLA_ROPE))
    k_all = jnp.concatenate([kc, kl], axis=1)
    v_all = jnp.concatenate([vc, vl], axis=1)

    def attend(qb, keys, vals):
        s = jnp.einsum('bqhd,bkhd->bhqk', qb, keys).astype(jnp.float32)
        p = jax.nn.softmax(s, -1).astype(vals.dtype)
        return jnp.einsum('bhqk,bkhd->bqhd', p, vals)

    nb = L // MLA_QBLOCK
    qblocks = ql.reshape(B, nb, MLA_QBLOCK, MLA_HEADS, MLA_NOPE + MLA_ROPE).transpose(1, 0, 2, 3, 4)
    o = lax.map(lambda qb: attend(qb, k_all, v_all), qblocks)
    o_lat = o.transpose(1, 0, 2, 3, 4).reshape(B, L, MLA_HEADS * MLA_V)
    y_ctx = attend(qc, kc, vc).reshape(B, -1, MLA_HEADS * MLA_V) @ w_out if want_ctx else None
    return y_ctx, o_lat @ w_out


def hgrn2_mixer(a_ctx, a_lat, w_in, lower_bounds, norm_g, w_out, layer_idx, want_ctx):
    B = a_lat.shape[0]
    lb_soft = jax.nn.softmax(lower_bounds.astype(jnp.float32), axis=0)
    lb = (jnp.cumsum(lb_soft, axis=0) - lb_soft[0])[layer_idx]

    def project(a):
        n = a.shape[1]
        q, f_f, f_b, i, g = jnp.split(a @ w_in, [HG_FDIM, 2 * HG_FDIM, 3 * HG_FDIM, 3 * HG_FDIM + HG_HEADS * HG_DI], axis=-1)
        heads = lambda t, d: t.reshape(B, n, HG_HEADS, d).transpose(0, 2, 1, 3)
        q = heads(jax.nn.silu(q), HG_EXPAND) * HG_EXPAND ** -0.5

        def gate(f):
            forget = lb + (1 - lb) * jax.nn.sigmoid(f.astype(jnp.float32))
            return heads(1 - forget, HG_EXPAND), heads(jnp.log(forget), HG_EXPAND)

        kf, gf = gate(f_f)
        kb, gb = gate(f_b)
        return (q, kf, kb, heads(i, HG_DI), gf, gb), g

    ctx_terms, g_ctx = project(a_ctx)
    lat_terms, g_lat = project(a_lat)
    o_ctx, o_lat = bidirectional_scan(ctx_terms, lat_terms)

    def readout(o, g):
        n = o.shape[2]
        y = rms_norm(o, norm_g).transpose(0, 2, 1, 3).reshape(B, n, HG_HEADS * HG_DI).astype(g.dtype)
        return (y * jax.nn.silu(g)) @ w_out

    return (readout(o_ctx, g_ctx) if want_ctx else None), readout(o_lat, g_lat)


def swiglu(a, w13, w2):
    gate, up = jnp.split(a @ w13, 2, axis=-1)
    return (jax.nn.silu(gate) * up) @ w2


def setup_inputs(seed: int = 0) -> dict:
    key = jax.random.key(seed)
    ks = iter(jax.random.split(key, 32))
    nrm = lambda shape, std: std * jax.random.normal(next(ks), shape, jnp.float32)
    D, F, beta = D_MODEL, D_FF, DEEPNORM_BETA
    ret_decay_base = jnp.log(-jnp.log1p(-(2.0 ** (-5.0 - jnp.arange(RET_HEADS, dtype=jnp.float32)))))
    return {
        'x': nrm((BATCH, SEQ, D), 1.0),
        'c': nrm((BATCH, D), 1.0),
        'ctx': nrm((BATCH, CTX_LEN, D), 1.0),
        'c_ctx': nrm((D,), 1.0),
        'ada_w': nrm((DEPTH, D, 6 * D), D ** -0.5),
        'ada_b': nrm((DEPTH, 6 * D), 0.01),
        'ln_g': 1.0 + nrm((DEPTH, 2, D), 0.01),
        'ln_b': nrm((DEPTH, 2, D), 0.01),
        'ffn_w13': nrm((DEPTH, D, 2 * F), D ** -0.5),
        'ffn_w2': nrm((DEPTH, F, D), beta * F ** -0.5),
        'ret_w_in': nrm((D, 2 * RET_HEADS * RET_DK + 2 * RET_HEADS * RET_DV), D ** -0.5),
        'ret_decay': ret_decay_base[None, :] + nrm((2, RET_HEADS), 0.01),
        'ret_w_out': nrm((RET_HEADS * RET_DV, D), beta * (RET_HEADS * RET_DV) ** -0.5),
        'na_w_qkv': nrm((D, 3 * NA_HEADS * NA_DH), D ** -0.5),
        'na_rpb': nrm((NA_HEADS, 2 * NA_WIN_ROWS - 1, 2 * NA_WIN_COLS - 1), 0.02),
        'na_w_out': nrm((NA_HEADS * NA_DH, D), beta * (NA_HEADS * NA_DH) ** -0.5),
        'mla_w_down': nrm((D, MLA_Q_LORA + MLA_KV_LORA + MLA_ROPE), D ** -0.5),
        'mla_q_norm': 1.0 + nrm((MLA_Q_LORA,), 0.01),
        'mla_kv_norm': 1.0 + nrm((MLA_KV_LORA,), 0.01),
        'mla_w_uq': nrm((MLA_Q_LORA, MLA_HEADS * (MLA_NOPE + MLA_ROPE)), MLA_Q_LORA ** -0.5),
        'mla_w_ukv': nrm((MLA_KV_LORA, MLA_HEADS * (MLA_NOPE + MLA_V)), MLA_KV_LORA ** -0.5),
        'mla_w_out': nrm((MLA_HEADS * MLA_V, D), beta * (MLA_HEADS * MLA_V) ** -0.5),
        'hg_w_in': nrm((D, 3 * HG_FDIM + 2 * HG_HEADS * HG_DI), D ** -0.5),
        'hg_lower_bounds': nrm((DEPTH, HG_FDIM), 0.1),
        'hg_norm_g': 1.0 + nrm((HG_DI,), 0.01),
        'hg_w_out': nrm((HG_HEADS * HG_DI, D), beta * (HG_HEADS * HG_DI) ** -0.5),
    }


def reference(x, c, ctx, c_ctx, ada_w, ada_b, ln_g, ln_b, ffn_w13, ffn_w2,
              ret_w_in, ret_decay, ret_w_out,
              na_w_qkv, na_rpb, na_w_out,
              mla_w_down, mla_q_norm, mla_kv_norm, mla_w_uq, mla_w_ukv, mla_w_out,
              hg_w_in, hg_lower_bounds, hg_norm_g, hg_w_out):
    h_lat, h_ctx = x, ctx
    cond_lat = jax.nn.silu(c)[:, None, :]
    cond_ctx = jax.nn.silu(c_ctx)[None, None, :]
    for i in range(DEPTH):
        want_ctx = i < DEPTH - 1
        m_lat = jnp.split(cond_lat @ ada_w[i] + ada_b[i], 6, axis=-1)
        m_ctx = jnp.split(cond_ctx @ ada_w[i] + ada_b[i], 6, axis=-1)
        a_lat = modulate(h_lat, m_lat[0], m_lat[1])
        a_ctx = modulate(h_ctx, m_ctx[0], m_ctx[1])
        kind = i % N_MIXERS
        if kind == 0:
            y_ctx, y_lat = retention_mixer(a_ctx, a_lat, ret_w_in, ret_decay, ret_w_out, want_ctx)
        elif kind == 1:
            y_ctx, y_lat = neighbourhood_mixer(a_ctx, a_lat, na_w_qkv, na_rpb, na_w_out, want_ctx)
        elif kind == 2:
            y_ctx, y_lat = mla_mixer(a_ctx, a_lat, mla_w_down, mla_q_norm, mla_kv_norm, mla_w_uq, mla_w_ukv, mla_w_out, want_ctx)
        else:
            y_ctx, y_lat = hgrn2_mixer(a_ctx, a_lat, hg_w_in, hg_lower_bounds, hg_norm_g, hg_w_out, i, want_ctx)
        h_lat = layer_norm(DEEPNORM_ALPHA * h_lat + m_lat[2] * y_lat, ln_g[i, 0], ln_b[i, 0])
        f_lat = swiglu(modulate(h_lat, m_lat[3], m_lat[4]), ffn_w13[i], ffn_w2[i])
        h_lat = layer_norm(DEEPNORM_ALPHA * h_lat + m_lat[5] * f_lat, ln_g[i, 1], ln_b[i, 1])
        if want_ctx:
            h_ctx = layer_norm(DEEPNORM_ALPHA * h_ctx + m_ctx[2] * y_ctx, ln_g[i, 0], ln_b[i, 0])
            f_ctx = swiglu(modulate(h_ctx, m_ctx[3], m_ctx[4]), ffn_w13[i], ffn_w2[i])
            h_ctx = layer_norm(DEEPNORM_ALPHA * h_ctx + m_ctx[5] * f_ctx, ln_g[i, 1], ln_b[i, 1])
    return h_lat
```

```python
import functools

import numpy as np
import jax
import jax.numpy as jnp
from jax import lax
from jax.experimental import pallas as pl
from jax.experimental.pallas import tpu as pltpu

F32 = jnp.float32
BF16 = jnp.bfloat16

GRID_W = 64
LN_EPS = 1e-5
RMS_EPS = 1e-6
ROPE_BASE = 10000.0
NEG_INF = -1e30

RET_HEADS = 4
NA_HEADS = 16
NA_WIN_ROWS = 8
NA_WIN_COLS = 16
MLA_HEADS = 16
MLA_NOPE = 64
MLA_ROPE = 32
MLA_V = 64
MLA_Q_LORA = 512
MLA_KV_LORA = 256
HG_EXPAND = 128
HG_CHUNK = 64
RET_CHUNK = 256

VMEM_LIMIT = 56 * 1024 * 1024
LANES = 128


def _cparams(sem):
    return pltpu.CompilerParams(dimension_semantics=sem, vmem_limit_bytes=VMEM_LIMIT)


def _const_spec(shape):
    nd = len(shape)
    return pl.BlockSpec(shape, lambda *_: (0,) * nd, pipeline_mode=pl.Buffered(1))


def _dot(a, b):
    return jnp.dot(a, b, preferred_element_type=F32)


def _dot_nt(a, b):
    return lax.dot_general(a, b, (((1,), (1,)), ((), ())), preferred_element_type=F32)


def _dot_tn(a, b):
    return lax.dot_general(a, b, (((0,), (0,)), ((), ())), preferred_element_type=F32)


def _silu(x):
    return x * jax.nn.sigmoid(x)


def _layer_norm(r, g, b):
    mu = jnp.mean(r, axis=-1, keepdims=True)
    rc = r - mu
    var = jnp.mean(rc * rc, axis=-1, keepdims=True)
    return rc * lax.rsqrt(var + LN_EPS) * g + b


class _Rows:
    def __init__(self, B, L, Lc, tm):
        assert L % tm == 0 and (B * Lc) % tm == 0
        self.B, self.L, self.Lc, self.tm = B, L, Lc, tm
        self.n_lat = B * L // tm
        self.n_all = (B * L + B * Lc) // tm
        self.per_b = L // tm

    def group(self, i):
        return jnp.where(i < self.n_lat, 1 + i // self.per_b, 0)

    def pos_block(self, i):
        return jnp.where(i < self.n_lat, i % self.per_b, self.per_b)


def _adaln_kernel(c_ref, w_ref, b_ref, o_ref):
    cond = _silu(c_ref[...])
    o_ref[0] = jnp.dot(cond, w_ref[0], preferred_element_type=F32,
                       precision=lax.Precision.HIGHEST) + b_ref[0]


def _adaln(cond_in, ada_w, ada_b):
    depth, D, N = ada_w.shape
    G = cond_in.shape[0]
    tn = 1536
    return pl.pallas_call(
        _adaln_kernel,
        grid=(depth, N // tn),
        in_specs=[pl.BlockSpec((G, D), lambda l, j: (0, 0)),
                  pl.BlockSpec((1, D, tn), lambda l, j: (l, 0, j)),
                  pl.BlockSpec((1, 1, tn), lambda l, j: (l, 0, j))],
        out_specs=pl.BlockSpec((1, G, tn), lambda l, j: (l, 0, j)),
        out_shape=jax.ShapeDtypeStruct((depth, G, N), F32),
        compiler_params=_cparams(("arbitrary", "arbitrary")),
        name="adaln",
    )(cond_in, ada_w, ada_b.reshape(depth, 1, N))


def _modmm_kernel(x_ref, mod_ref, w_ref, *rest, rope_tiles, k_scale):
    if rope_tiles:
        cos_ref, sin_ref, o_ref, a_scr = rest
    else:
        o_ref, a_scr = rest
    j = pl.program_id(1)

    @pl.when(j == 0)
    def _():
        sh = mod_ref[0, 0:1, :]
        sc = mod_ref[0, 1:2, :]
        a_scr[...] = (x_ref[...] * (1.0 + sc) + sh).astype(BF16)

    acc = _dot(a_scr[...], w_ref[...])
    if not rope_tiles:
        o_ref[...] = acc.astype(o_ref.dtype)
        return

    @pl.when(j >= rope_tiles)
    def _():
        o_ref[...] = acc.astype(o_ref.dtype)

    @pl.when(j < rope_tiles)
    def _():
        cos = cos_ref[...]
        sin = sin_ref[...]
        scale = jnp.where(j == 1, k_scale, 1.0).astype(F32)
        tn = acc.shape[1]
        for h in range(tn // (2 * LANES)):
            x1 = acc[:, h * 256:h * 256 + LANES]
            x2 = acc[:, h * 256 + LANES:(h + 1) * 256]
            o_ref[:, h * 256:h * 256 + LANES] = ((x1 * cos - x2 * sin) * scale).astype(o_ref.dtype)
            o_ref[:, h * 256 + LANES:(h + 1) * 256] = ((x1 * sin + x2 * cos) * scale).astype(o_ref.dtype)


def _modmm(rows, x, mod, w, out_dtype, tn, rope=None, k_scale=1.0):
    R, D = x.shape
    N = w.shape[1]
    tm = rows.tm
    in_specs = [pl.BlockSpec((tm, D), lambda i, j: (i, 0)),
                pl.BlockSpec((1, 6, D), lambda i, j: (rows.group(i), 0, 0)),
                pl.BlockSpec((D, tn), lambda i, j: (0, j))]
    args = [x, mod, w]
    if rope is not None:
        in_specs += [pl.BlockSpec((tm, LANES), lambda i, j: (rows.pos_block(i), 0))] * 2
        args += list(rope)
    return pl.pallas_call(
        functools.partial(_modmm_kernel, rope_tiles=2 if rope is not None else 0, k_scale=k_scale),
        grid=(rows.n_all, N // tn),
        in_specs=in_specs,
        out_specs=pl.BlockSpec((tm, tn), lambda i, j: (i, j)),
        out_shape=jax.ShapeDtypeStruct((R, N), out_dtype),
        scratch_shapes=[pltpu.VMEM((tm, D), BF16)],
        compiler_params=_cparams(("arbitrary", "arbitrary")),
        name="modmm",
    )(*args)


def _outproj_kernel(x_ref, mod_ref, y_ref, *rest, gated, alpha):
    if gated:
        g_ref, w_ref, lng_ref, lnb_ref, o_ref = rest
        g = g_ref[...].astype(F32)
        a = (_silu(g) * y_ref[...].astype(F32)).astype(BF16)
    else:
        w_ref, lng_ref, lnb_ref, o_ref = rest
        a = y_ref[...]
    acc = _dot(a, w_ref[...])
    gate = mod_ref[0, 2:3, :]
    r = alpha * x_ref[...] + gate * acc
    o_ref[...] = _layer_norm(r, lng_ref[...], lnb_ref[...])


def _outproj_ln(rows, x, mod, y, w, ln_g, ln_b, alpha, gate_src=None):
    R, D = x.shape
    K = w.shape[0]
    tm = rows.tm
    in_specs = [pl.BlockSpec((tm, D), lambda i: (i, 0)),
                pl.BlockSpec((1, 6, D), lambda i: (rows.group(i), 0, 0)),
                pl.BlockSpec((tm, K), lambda i: (i, 0))]
    args = [x, mod, y]
    if gate_src is not None:
        g_arr, g_blk = gate_src
        in_specs.append(pl.BlockSpec((tm, K), lambda i: (i, g_blk)))
        args.append(g_arr)
    in_specs += [_const_spec((K, D)), _const_spec((1, D)), _const_spec((1, D))]
    args += [w, ln_g.reshape(1, D), ln_b.reshape(1, D)]
    return pl.pallas_call(
        functools.partial(_outproj_kernel, gated=gate_src is not None, alpha=alpha),
        grid=(rows.n_all,),
        in_specs=in_specs,
        out_specs=pl.BlockSpec((tm, D), lambda i: (i, 0)),
        out_shape=jax.ShapeDtypeStruct((R, D), F32),
        compiler_params=_cparams(("arbitrary",)),
        name="outproj_ln",
    )(*args)


def _ffn_kernel(x_ref, mod_ref, w13_ref, w2_ref, lng_ref, lnb_ref, o_ref, *, alpha, fc):
    x = x_ref[...]
    sh = mod_ref[0, 3:4, :]
    sc = mod_ref[0, 4:5, :]
    gate = mod_ref[0, 5:6, :]
    a = (x * (1.0 + sc) + sh).astype(BF16)
    F = w2_ref.shape[0]
    acc = jnp.zeros(x.shape, F32)
    for c in range(F // fc):
        g = _dot(a, w13_ref[:, c * fc:(c + 1) * fc])
        u = _dot(a, w13_ref[:, F + c * fc:F + (c + 1) * fc])
        act = (_silu(g) * u).astype(BF16)
        acc = acc + _dot(act, w2_ref[c * fc:(c + 1) * fc, :])
    r = alpha * x + gate * acc
    o_ref[...] = _layer_norm(r, lng_ref[...], lnb_ref[...])


def _ffn(rows, x, mod, w13, w2, ln_g, ln_b, alpha, n_tiles):
    R, D = x.shape
    F = w2.shape[0]
    tm = rows.tm
    return pl.pallas_call(
        functools.partial(_ffn_kernel, alpha=alpha, fc=256),
        grid=(n_tiles,),
        in_specs=[pl.BlockSpec((tm, D), lambda i: (i, 0)),
                  pl.BlockSpec((1, 6, D), lambda i: (rows.group(i), 0, 0)),
                  _const_spec((D, 2 * F)), _const_spec((F, D)),
                  _const_spec((1, D)), _const_spec((1, D))],
        out_specs=pl.BlockSpec((tm, D), lambda i: (i, 0)),
        out_shape=jax.ShapeDtypeStruct((n_tiles * tm, D), F32),
        compiler_params=_cparams(("arbitrary",)),
        name="ffn",
    )(x, mod, w13, w2, ln_g.reshape(1, D), ln_b.reshape(1, D))


def _chunk_order(B, L, Lc, C, backward):
    ncc, ncl = Lc // C, L // C

    def blk(b, j):
        if backward:
            in_ctx = j < ncc
            return jnp.where(in_ctx, B * ncl + b * ncc + (ncc - 1 - j), b * ncl + (ncl - 1 - (j - ncc)))
        return jnp.where(j < ncc, B * ncl + b * ncc + j, b * ncl + (j - ncc))

    return blk, ncc + ncl


def _ret_scan_kernel(lg_ref, q_ref, k_ref, v_ref, *rest, backward, C):
    if backward:
        of_ref, o_ref, st_scr = rest
    else:
        o_ref, st_scr = rest
    h = pl.program_id(1)
    j = pl.program_id(2)
    lg = lg_ref[1 if backward else 0, h]

    @pl.when(j == 0)
    def _():
        st_scr[...] = jnp.zeros_like(st_scr)

    q = q_ref[...]
    k = k_ref[...]
    v = v_ref[...]
    ti = lax.broadcasted_iota(jnp.int32, (C, C), 0)
    si = lax.broadcasted_iota(jnp.int32, (C, C), 1)
    dist = (si - ti) if backward else (ti - si)
    decay = jnp.where(dist >= 0, jnp.exp(lg * jnp.maximum(dist, 0).astype(F32)), 0.0)
    att = (_dot_nt(q, k) * decay).astype(BF16)
    tcol = lax.broadcasted_iota(jnp.int32, (C, 1), 0)
    eq = (C - tcol) if backward else (tcol + 1)
    ek = tcol if backward else (C - 1 - tcol)
    st = st_scr[...]
    o = _dot(att, v) + _dot_nt(q, st.astype(BF16)) * jnp.exp(lg * eq.astype(F32))
    kd = (k.astype(F32) * jnp.exp(lg * ek.astype(F32))).astype(BF16)
    st_scr[...] = st * jnp.exp(lg * jnp.full((1, 1), float(C), F32)) + _dot_tn(v, kd)
    if backward:
        tot = of_ref[...] + o
        y = tot * lax.rsqrt(jnp.mean(tot * tot, axis=-1, keepdims=True) + RMS_EPS)
        o_ref[...] = y.astype(o_ref.dtype)
    else:
        o_ref[...] = o


def _ret_scan(proj, lg, B, L, Lc):
    R = proj.shape[0]
    C = RET_CHUNK
    H, dk, dv = RET_HEADS, 256, 512
    outs = None
    for backward in (False, True):
        blk, nch = _chunk_order(B, L, Lc, C, backward)
        in_specs = [pl.BlockSpec(memory_space=pltpu.SMEM),
                    pl.BlockSpec((C, dk), lambda b, h, j: (blk(b, j), h)),
                    pl.BlockSpec((C, dk), lambda b, h, j: (blk(b, j), H + h)),
                    pl.BlockSpec((C, dv), lambda b, h, j: (blk(b, j), H + h))]
        args = [lg, proj, proj, proj]
        if backward:
            in_specs.append(pl.BlockSpec((C, dv), lambda b, h, j: (blk(b, j), h)))
            args.append(outs)
        outs = pl.pallas_call(
            functools.partial(_ret_scan_kernel, backward=backward, C=C),
            grid=(B, H, nch),
            in_specs=in_specs,
            out_specs=pl.BlockSpec((C, dv), lambda b, h, j: (blk(b, j), h)),
            out_shape=jax.ShapeDtypeStruct((R, H * dv), BF16 if backward else F32),
            scratch_shapes=[pltpu.VMEM((dv, dk), F32)],
            compiler_params=_cparams(("arbitrary", "arbitrary", "arbitrary")),
            name="ret_scan_bwd" if backward else "ret_scan_fwd",
        )(*args)
    return outs


def _cumsum_rows(x):
    n = x.shape[0]
    r = lax.broadcasted_iota(jnp.int32, (n, 1), 0)
    s = 1
    while s < n:
        x = x + jnp.where(r >= s, pltpu.roll(x, s, axis=0), 0.0)
        s *= 2
    return x


def _hg_scan_kernel(q_ref, f_ref, v_ref, lb_ref, *rest, backward, C, heads, scale):
    if backward:
        of_ref, ng_ref, o_ref, st_scr = rest
    else:
        o_ref, st_scr = rest
    j = pl.program_id(1)

    @pl.when(j == 0)
    def _():
        st_scr[...] = jnp.zeros_like(st_scr)

    lb = lb_ref[...]
    forget = lb + (1.0 - lb) * jax.nn.sigmoid(f_ref[...])
    kk = 1.0 - forget
    gl = jnp.log(forget)
    pre = _cumsum_rows(gl)
    tot = pre[C - 1:C, :]
    bc = (tot - pre + gl) if backward else pre
    qd = (_silu(q_ref[...]) * scale * jnp.exp(bc)).astype(BF16)
    kd = (kk * jnp.exp(-bc)).astype(BF16)
    ke = (kk * jnp.exp(tot - bc)).astype(BF16)
    v = v_ref[...].astype(BF16)
    etot = jnp.exp(tot)
    ti = lax.broadcasted_iota(jnp.int32, (C, C), 0)
    si = lax.broadcasted_iota(jnp.int32, (C, C), 1)
    keep = (si >= ti) if backward else (ti >= si)
    d = HG_EXPAND
    for h in range(heads):
        sl = slice(h * d, (h + 1) * d)
        att = jnp.where(keep, _dot_nt(qd[:, sl], kd[:, sl]), 0.0).astype(BF16)
        st = st_scr[h]
        o = _dot(att, v[:, sl]) + _dot_nt(qd[:, sl], st.astype(BF16))
        st_scr[h] = st * etot[:, sl] + _dot_tn(v[:, sl], ke[:, sl])
        if backward:
            t = of_ref[:, sl] + o
            y = t * lax.rsqrt(jnp.mean(t * t, axis=-1, keepdims=True) + RMS_EPS) * ng_ref[...]
            o_ref[:, sl] = y.astype(o_ref.dtype)
        else:
            o_ref[:, sl] = o


def _hg_scan(proj, lb, norm_g, B, L, Lc):
    R = proj.shape[0]
    C = HG_CHUNK
    Dm = proj.shape[1] // 5
    heads = Dm // HG_EXPAND
    outs = None
    for backward in (False, True):
        blk, nch = _chunk_order(B, L, Lc, C, backward)
        fcol = 2 if backward else 1
        in_specs = [pl.BlockSpec((C, Dm), lambda b, j: (blk(b, j), 0)),
                    pl.BlockSpec((C, Dm), lambda b, j: (blk(b, j), fcol)),
                    pl.BlockSpec((C, Dm), lambda b, j: (blk(b, j), 3)),
                    pl.BlockSpec((1, Dm), lambda b, j: (0, 0))]
        args = [proj, proj, proj, lb.reshape(1, Dm)]
        if backward:
            in_specs += [pl.BlockSpec((C, Dm), lambda b, j: (blk(b, j), 0)),
                         pl.BlockSpec((1, HG_EXPAND), lambda b, j: (0, 0))]
            args += [outs, norm_g.reshape(1, HG_EXPAND)]
        outs = pl.pallas_call(
            functools.partial(_hg_scan_kernel, backward=backward, C=C, heads=heads,
                              scale=HG_EXPAND ** -0.5),
            grid=(B, nch),
            in_specs=in_specs,
            out_specs=pl.BlockSpec((C, Dm), lambda b, j: (blk(b, j), 0)),
            out_shape=jax.ShapeDtypeStruct((R, Dm), BF16 if backward else F32),
            scratch_shapes=[pltpu.VMEM((heads, HG_EXPAND, HG_EXPAND), F32)],
            compiler_params=_cparams(("arbitrary", "arbitrary")),
            name="hg_scan_bwd" if backward else "hg_scan_fwd",
        )(*args)
    return outs


def _softmax_pv(scores, values):
    m = functools.reduce(jnp.maximum, [jnp.max(s, axis=-1, keepdims=True) for s in scores])
    ps = [jnp.exp(s - m) for s in scores]
    l = functools.reduce(lambda a, b: a + b, [jnp.sum(p, axis=-1, keepdims=True) for p in ps])
    o = functools.reduce(lambda a, b: a + b, [_dot(p.astype(BF16), v) for p, v in zip(ps, values)])
    return o / l


def _lane_lo(shape):
    return lax.broadcasted_iota(jnp.int32, shape, 1) < (LANES // 2)


def _na_kernel(q_ref, kw_ref, vw_ref, kc_ref, vc_ref, bias_ref, o_ref, *, heads):
    q = q_ref[...] * 0.125
    lo = _lane_lo((q.shape[0], LANES))
    for hp in range(heads // 2):
        sl = slice(hp * LANES, (hp + 1) * LANES)
        q2, kw, vw, kc, vc = q[:, sl], kw_ref[:, sl], vw_ref[:, sl], kc_ref[:, sl], vc_ref[:, sl]
        outs = []
        for e in range(2):
            qm = jnp.where(lo if e == 0 else ~lo, q2, jnp.zeros_like(q2))
            s_lat = _dot_nt(qm, kw) + bias_ref[2 * hp + e, 0]
            s_ctx = _dot_nt(qm, kc)
            outs.append(_softmax_pv([s_lat, s_ctx], [vw, vc]))
        o_ref[:, sl] = jnp.where(lo, outs[0], outs[1]).astype(o_ref.dtype)


def _na_bias_table(rpb):
    W, wr, wc = GRID_W, NA_WIN_ROWS, NA_WIN_COLS
    qcol = np.arange(W)[:, None]
    kcol = np.arange(W)[None, :]
    ws = np.clip(qcol - wc // 2, 0, W - wc)
    ok = (kcol >= ws) & (kcol < ws + wc)
    cidx = np.clip(kcol - qcol + wc - 1, 0, 2 * wc - 2)
    delta = np.arange(wr)[:, None]
    jrow = np.arange(wr)[None, :]
    ridx = jrow - delta + wr - 1
    t = rpb[:, ridx[:, None, :, None], cidx[None, :, None, :]]
    t = jnp.where(ok[None, None, :, None, :], t, NEG_INF)
    return t.reshape(rpb.shape[0], wr, W, wr * W).astype(F32)


def _na_attention(qkv, bias, B, L, Lc):
    D = qkv.shape[1] // 3
    W, wr = GRID_W, NA_WIN_ROWS
    nrows = L // W
    assert nrows >= wr
    rs = lambda r: jnp.clip(r - wr // 2, 0, nrows - wr)
    ctx_blk0 = B * L // Lc
    return pl.pallas_call(
        functools.partial(_na_kernel, heads=NA_HEADS),
        grid=(B, nrows),
        in_specs=[pl.BlockSpec((W, D), lambda b, r: (b * nrows + r, 0)),
                  pl.BlockSpec((pl.Element(wr * W), pl.Element(D)),
                               lambda b, r: ((b * nrows + rs(r)) * W, D)),
                  pl.BlockSpec((pl.Element(wr * W), pl.Element(D)),
                               lambda b, r: ((b * nrows + rs(r)) * W, 2 * D)),
                  pl.BlockSpec((Lc, D), lambda b, r: (ctx_blk0 + b, 1)),
                  pl.BlockSpec((Lc, D), lambda b, r: (ctx_blk0 + b, 2)),
                  pl.BlockSpec((NA_HEADS, 1, W, wr * W), lambda b, r: (0, r - rs(r), 0, 0))],
        out_specs=pl.BlockSpec((W, D), lambda b, r: (b * nrows + r, 0)),
        out_shape=jax.ShapeDtypeStruct((B * L, D), BF16),
        compiler_params=_cparams(("arbitrary", "arbitrary")),
        name="na_attn",
    )(qkv, qkv, qkv, qkv, qkv, bias)


def _ctx_attn_kernel(q_ref, k_ref, v_ref, o_ref, *, heads, split64, q_scale):
    lo = _lane_lo((q_ref.shape[0], LANES))
    for hp in range(heads // 2):
        vsl = slice(hp * LANES, (hp + 1) * LANES)
        v2 = v_ref[:, vsl]
        outs = []
        for e in range(2):
            if split64:
                q2 = q_ref[:, vsl] * q_scale
                qh = jnp.where(lo if e == 0 else ~lo, q2, jnp.zeros_like(q2))
                kh = k_ref[:, vsl]
            else:
                hsl = slice((2 * hp + e) * LANES, (2 * hp + e + 1) * LANES)
                qh, kh = q_ref[:, hsl], k_ref[:, hsl]
            outs.append(_softmax_pv([_dot_nt(qh, kh)], [v2]))
        o_ref[:, vsl] = jnp.where(lo, outs[0], outs[1]).astype(o_ref.dtype)


def _ctx_attention(q_src, k_src, v_src, B, L, Lc, heads, split64, q_scale=1.0):
    blk0 = B * L // Lc
    spec = lambda src: pl.BlockSpec((Lc, src[1]), lambda b: (blk0 + b, src[2]))
    Dv = v_src[1]
    return pl.pallas_call(
        functools.partial(_ctx_attn_kernel, heads=heads, split64=split64, q_scale=q_scale),
        grid=(B,),
        in_specs=[spec(q_src), spec(k_src), spec(v_src)],
        out_specs=pl.BlockSpec((Lc, Dv), lambda b: (b, 0)),
        out_shape=jax.ShapeDtypeStruct((B * Lc, Dv), BF16),
        compiler_params=_cparams(("arbitrary",)),
        name="ctx_attn",
    )(q_src[0], k_src[0], v_src[0])


def _rope_slot(x, cos, sina, sinb):
    return x * cos + pltpu.roll(x, LANES - 16, axis=1) * sina + pltpu.roll(x, 16, axis=1) * sinb


def _mla_proj_kernel(x_ref, mod_ref, wd_ref, qn_ref, kvn_ref, wq_ref, wk_ref, wv_ref,
                     cos_ref, sina_ref, sinb_ref, q_ref, k_ref, v_ref, *, heads, scale):
    sh = mod_ref[0, 0:1, :]
    sc = mod_ref[0, 1:2, :]
    a = (x_ref[...] * (1.0 + sc) + sh).astype(BF16)
    d = _dot(a, wd_ref[...])
    cq = d[:, :MLA_Q_LORA]
    ckv = d[:, MLA_Q_LORA:MLA_Q_LORA + MLA_KV_LORA]
    kr = d[:, MLA_Q_LORA + MLA_KV_LORA:]
    rms = lambda t, g: (t * lax.rsqrt(jnp.mean(t * t, axis=-1, keepdims=True) + RMS_EPS) * g).astype(BF16)
    cqn = rms(cq, qn_ref[...])
    ckvn = rms(ckv, kvn_ref[...])
    cos, sina, sinb = cos_ref[...], sina_ref[...], sinb_ref[...]
    krr = _rope_slot(kr, cos, sina, sinb)
    qf = _dot(cqn, wq_ref[...])
    kf = _dot(ckvn, wk_ref[...])
    for h in range(heads):
        sl = slice(h * LANES, (h + 1) * LANES)
        q_ref[:, sl] = (_rope_slot(qf[:, sl], cos, sina, sinb) * scale).astype(q_ref.dtype)
        k_ref[:, sl] = (kf[:, sl] + krr).astype(k_ref.dtype)
    v_ref[...] = _dot(ckvn, wv_ref[...]).astype(v_ref.dtype)


def _mla_proj(rows, x, mod, wd, qn, kvn, wq, wk, wv, tables):
    R, D = x.shape
    tm = rows.tm
    H = MLA_HEADS
    tab_spec = pl.BlockSpec((tm, LANES), lambda i: (rows.pos_block(i), 0))
    return pl.pallas_call(
        functools.partial(_mla_proj_kernel, heads=H, scale=(MLA_NOPE + MLA_ROPE) ** -0.5),
        grid=(rows.n_all,),
        in_specs=[pl.BlockSpec((tm, D), lambda i: (i, 0)),
                  pl.BlockSpec((1, 6, D), lambda i: (rows.group(i), 0, 0)),
                  _const_spec(wd.shape), _const_spec((1, MLA_Q_LORA)), _const_spec((1, MLA_KV_LORA)),
                  _const_spec(wq.shape), _const_spec(wk.shape), _const_spec(wv.shape),
                  tab_spec, tab_spec, tab_spec],
        out_specs=[pl.BlockSpec((tm, H * LANES), lambda i: (i, 0)),
                   pl.BlockSpec((tm, H * LANES), lambda i: (i, 0)),
                   pl.BlockSpec((tm, H * MLA_V), lambda i: (i, 0))],
        out_shape=[jax.ShapeDtypeStruct((R, H * LANES), BF16),
                   jax.ShapeDtypeStruct((R, H * LANES), BF16),
                   jax.ShapeDtypeStruct((R, H * MLA_V), BF16)],
        compiler_params=_cparams(("arbitrary",)),
        name="mla_proj",
    )(x, mod, wd, qn.reshape(1, -1), kvn.reshape(1, -1), wq, wk, wv, *tables)


def _mla_flash_kernel(q_ref, kc_ref, vc_ref, k_ref, v_ref, o_ref, *, tk):
    tq = q_ref.shape[0]
    L = k_ref.shape[0]
    lo = _lane_lo((tq, LANES))
    qs = [q_ref[:, e * LANES:(e + 1) * LANES] for e in range(2)]

    def block(kblk, vblk, state):
        ms, ls, acc = state
        new_m, new_l, pvs, alphas = [], [], [], []
        for e in range(2):
            s = _dot_nt(qs[e], kblk[:, e * LANES:(e + 1) * LANES])
            m_new = jnp.maximum(ms[e], jnp.max(s, axis=-1, keepdims=True))
            alpha = jnp.exp(ms[e] - m_new)
            p = jnp.exp(s - m_new)
            new_l.append(alpha * ls[e] + jnp.sum(p, axis=-1, keepdims=True))
            new_m.append(m_new)
            alphas.append(alpha)
            pvs.append(_dot(p.astype(BF16), vblk))
        acc = jnp.where(lo, alphas[0], alphas[1]) * acc + jnp.where(lo, pvs[0], pvs[1])
        return tuple(new_m), tuple(new_l), acc

    neg = jnp.full((tq, 1), NEG_INF, F32)
    zero = jnp.zeros((tq, 1), F32)
    state = block(kc_ref[...], vc_ref[...], ((neg, neg), (zero, zero), jnp.zeros((tq, LANES), F32)))

    def body(c, st):
        off = pl.multiple_of(c * tk, tk)
        return block(k_ref[pl.ds(off, tk), :], v_ref[pl.ds(off, tk), :], st)

    ms, ls, acc = lax.fori_loop(0, L // tk, body, state)
    o_ref[...] = (acc / jnp.where(lo, ls[0], ls[1])).astype(o_ref.dtype)


def _mla_attention(q, k, v, B, L, Lc, tq, tk):
    H = MLA_HEADS
    nq = L // tq
    ctx_blk0 = B * L // Lc
    return pl.pallas_call(
        functools.partial(_mla_flash_kernel, tk=tk),
        grid=(B, H // 2, nq),
        in_specs=[pl.BlockSpec((tq, 2 * LANES), lambda b, hp, i: (b * nq + i, hp)),
                  pl.BlockSpec((Lc, 2 * LANES), lambda b, hp, i: (ctx_blk0 + b, hp)),
                  pl.BlockSpec((Lc, LANES), lambda b, hp, i: (ctx_blk0 + b, hp)),
                  pl.BlockSpec((L, 2 * LANES), lambda b, hp, i: (b, hp)),
                  pl.BlockSpec((L, LANES), lambda b, hp, i: (b, hp))],
        out_specs=pl.BlockSpec((tq, LANES), lambda b, hp, i: (b * nq + i, hp)),
        out_shape=jax.ShapeDtypeStruct((B * L, H * MLA_V), BF16),
        compiler_params=_cparams(("arbitrary", "arbitrary", "arbitrary")),
        name="mla_flash",
    )(q, k, v, k, v)


def _axial_angles(L, rot_dim):
    t = jnp.arange(L)
    rows = (t // GRID_W).astype(F32)
    cols = (t % GRID_W).astype(F32)
    n_freq = rot_dim // 4
    inv = ROPE_BASE ** (-jnp.arange(n_freq, dtype=F32) / n_freq)
    return jnp.concatenate([rows[:, None] * inv, cols[:, None] * inv], -1)


def _ret_rope_tables(L, tm):
    ang = _axial_angles(L, 256)
    cos = jnp.concatenate([jnp.cos(ang), jnp.ones((tm, LANES), F32)], 0)
    sin = jnp.concatenate([jnp.sin(ang), jnp.zeros((tm, LANES), F32)], 0)
    return cos, sin


def _mla_rope_tables(L, tm):
    ang = _axial_angles(L, MLA_ROPE)
    c, s = jnp.cos(ang), jnp.sin(ang)
    one = jnp.ones((L, MLA_NOPE), F32)
    z16 = jnp.zeros((L, 16), F32)
    z32 = jnp.zeros((L, 32), F32)
    z64 = jnp.zeros((L, MLA_NOPE), F32)
    cos = jnp.concatenate([one, c, c, jnp.ones((L, 32), F32)], -1)
    sina = jnp.concatenate([z64, -s, z16, z32], -1)
    sinb = jnp.concatenate([z64, z16, s, z32], -1)
    ident = lambda t, fill: jnp.concatenate([t, jnp.full((tm, LANES), fill, F32)], 0)
    return ident(cos, 1.0), ident(sina, 0.0), ident(sinb, 0.0)


def kernel(x, c, ctx, c_ctx, ada_w, ada_b, ln_g, ln_b, ffn_w13, ffn_w2, ret_w_in, ret_decay, ret_w_out, na_w_qkv, na_rpb, na_w_out, mla_w_down, mla_q_norm, mla_kv_norm, mla_w_uq, mla_w_ukv, mla_w_out, hg_w_in, hg_lower_bounds, hg_norm_g, hg_w_out):
    B, L, D = x.shape
    Lc = ctx.shape[1]
    depth = ada_w.shape[0]
    alpha = (2 * depth) ** 0.25
    tm = 512 if (B * Lc) % 512 == 0 else 256
    rows = _Rows(B, L, Lc, tm)
    n_lat_rows = B * L

    h = jnp.concatenate([x.reshape(B * L, D), ctx.reshape(B * Lc, D)], 0)

    G = 8 * (-(-(B + 1) // 8))
    cond_in = jnp.zeros((G, D), F32).at[0].set(c_ctx).at[1:B + 1].set(c)
    mods = _adaln(cond_in, ada_w, ada_b).reshape(depth, G, 6, D)

    bf = lambda w: w.astype(BF16)
    H = MLA_HEADS
    wd = jnp.zeros((D, MLA_Q_LORA + MLA_KV_LORA + LANES), F32)
    wd = wd.at[:, :MLA_Q_LORA + MLA_KV_LORA].set(mla_w_down[:, :MLA_Q_LORA + MLA_KV_LORA])
    wd = wd.at[:, MLA_Q_LORA + MLA_KV_LORA + MLA_NOPE:MLA_Q_LORA + MLA_KV_LORA + MLA_NOPE + MLA_ROPE].set(
        mla_w_down[:, MLA_Q_LORA + MLA_KV_LORA:])
    wq = jnp.pad(mla_w_uq.reshape(MLA_Q_LORA, H, MLA_NOPE + MLA_ROPE),
                 ((0, 0), (0, 0), (0, LANES - MLA_NOPE - MLA_ROPE))).reshape(MLA_Q_LORA, H * LANES)
    wukv = mla_w_ukv.reshape(MLA_KV_LORA, H, MLA_NOPE + MLA_V)
    wk = jnp.pad(wukv[:, :, :MLA_NOPE], ((0, 0), (0, 0), (0, LANES - MLA_NOPE))).reshape(MLA_KV_LORA, H * LANES)
    wv = wukv[:, :, MLA_NOPE:].reshape(MLA_KV_LORA, H * MLA_V)

    for i in range(depth):
        mod = mods[i]
        kind = i % 4
        want_ctx = i < depth - 1
        if kind == 0:
            lg = -jnp.exp(ret_decay.astype(F32))
            proj = _modmm(rows, h, mod, bf(ret_w_in), BF16, 1024,
                          rope=_ret_rope_tables(L, tm), k_scale=256 ** -0.5)
            y = _ret_scan(proj, lg, B, L, Lc)
            h1 = _outproj_ln(rows, h, mod, y, bf(ret_w_out), ln_g[i, 0], ln_b[i, 0], alpha,
                             gate_src=(proj, 2))
        elif kind == 1:
            qkv = _modmm(rows, h, mod, bf(na_w_qkv), BF16, 1024)
            o_lat = _na_attention(qkv, _na_bias_table(na_rpb), B, L, Lc)
            o_ctx = _ctx_attention((qkv, D, 0), (qkv, D, 1), (qkv, D, 2), B, L, Lc, NA_HEADS,
                                   split64=True, q_scale=0.125)
            y = jnp.concatenate([o_lat, o_ctx], 0)
            h1 = _outproj_ln(rows, h, mod, y, bf(na_w_out), ln_g[i, 0], ln_b[i, 0], alpha)
        elif kind == 2:
            q, k, v = _mla_proj(rows, h, mod, bf(wd), mla_q_norm, mla_kv_norm, bf(wq), bf(wk), bf(wv),
                                _mla_rope_tables(L, tm))
            tq = 256
            tk = 512 if L % 512 == 0 else L
            o_lat = _mla_attention(q, k, v, B, L, Lc, tq, tk)
            o_ctx = _ctx_attention((q, H * LANES, 0), (k, H * LANES, 0), (v, H * MLA_V, 0), B, L, Lc,
                                   H, split64=False)
            y = jnp.concatenate([o_lat, o_ctx], 0)
            h1 = _outproj_ln(rows, h, mod, y, bf(mla_w_out), ln_g[i, 0], ln_b[i, 0], alpha)
        else:
            lb_soft = jax.nn.softmax(hg_lower_bounds.astype(F32), axis=0)
            lb = (jnp.cumsum(lb_soft, axis=0) - lb_soft[0])[i]
            proj = _modmm(rows, h, mod, bf(hg_w_in), F32, 1024)
            y = _hg_scan(proj, lb, hg_norm_g, B, L, Lc)
            h1 = _outproj_ln(rows, h, mod, y, bf(hg_w_out), ln_g[i, 0], ln_b[i, 0], alpha,
                             gate_src=(proj, 4))
        n_tiles = rows.n_all if want_ctx else rows.n_lat
        h = _ffn(rows, h1, mod, bf(ffn_w13[i]), bf(ffn_w2[i]), ln_g[i, 1], ln_b[i, 1], alpha, n_tiles)
    return h[:n_lat_rows].reshape(B, L, D)
```

```python
import functools

import numpy as np
import jax
import jax.numpy as jnp
from jax import lax
from jax.experimental import pallas as pl
from jax.experimental.pallas import tpu as pltpu

F32 = jnp.float32
BF16 = jnp.bfloat16

GRID_W = 64
LN_EPS = 1e-5
RMS_EPS = 1e-6
ROPE_BASE = 10000.0
NEG_INF = -1e30

RET_HEADS = 4
NA_HEADS = 16
NA_WIN_ROWS = 8
NA_WIN_COLS = 16
MLA_HEADS = 16
MLA_NOPE = 64
MLA_ROPE = 32
MLA_V = 64
MLA_Q_LORA = 512
MLA_KV_LORA = 256
HG_EXPAND = 128
HG_CHUNK = 64
RET_CHUNK = 256

VMEM_LIMIT = 56 * 1024 * 1024
LANES = 128


def _cparams(sem):
    return pltpu.CompilerParams(dimension_semantics=sem, vmem_limit_bytes=VMEM_LIMIT)


def _const_spec(shape):
    nd = len(shape)
    return pl.BlockSpec(shape, lambda *_: (0,) * nd, pipeline_mode=pl.Buffered(1))


def _dot(a, b):
    return jnp.dot(a, b, preferred_element_type=F32)


def _dot_nt(a, b):
    return lax.dot_general(a, b, (((1,), (1,)), ((), ())), preferred_element_type=F32)


def _dot_tn(a, b):
    return lax.dot_general(a, b, (((0,), (0,)), ((), ())), preferred_element_type=F32)


def _silu(x):
    return x * jax.nn.sigmoid(x)


def _layer_norm(r, g, b):
    mu = jnp.mean(r, axis=-1, keepdims=True)
    rc = r - mu
    var = jnp.mean(rc * rc, axis=-1, keepdims=True)
    return rc * lax.rsqrt(var + LN_EPS) * g + b


class _Rows:
    def __init__(self, B, L, Lc, tm):
        assert L % tm == 0 and (B * Lc) % tm == 0
        self.B, self.L, self.Lc, self.tm = B, L, Lc, tm
        self.n_lat = B * L // tm
        self.n_all = (B * L + B * Lc) // tm
        self.per_b = L // tm

    def group(self, i):
        return jnp.where(i < self.n_lat, 1 + i // self.per_b, 0)

    def pos_block(self, i):
        return jnp.where(i < self.n_lat, i % self.per_b, self.per_b)


def _adaln_kernel(c_ref, w_ref, b_ref, o_ref):
    cond = _silu(c_ref[...])
    o_ref[0] = jnp.dot(cond, w_ref[0], preferred_element_type=F32,
                       precision=lax.Precision.HIGHEST) + b_ref[0]


def _adaln(cond_in, ada_w, ada_b):
    depth, D, N = ada_w.shape
    G = cond_in.shape[0]
    tn = 1536
    return pl.pallas_call(
        _adaln_kernel,
        grid=(depth, N // tn),
        in_specs=[pl.BlockSpec((G, D), lambda l, j: (0, 0)),
                  pl.BlockSpec((1, D, tn), lambda l, j: (l, 0, j)),
                  pl.BlockSpec((1, 1, tn), lambda l, j: (l, 0, j))],
        out_specs=pl.BlockSpec((1, G, tn), lambda l, j: (l, 0, j)),
        out_shape=jax.ShapeDtypeStruct((depth, G, N), F32),
        compiler_params=_cparams(("arbitrary", "arbitrary")),
        name="adaln",
    )(cond_in, ada_w, ada_b.reshape(depth, 1, N))


def _modmm_kernel(x_ref, mod_ref, w_ref, *rest, rope_tiles, k_scale):
    if rope_tiles:
        cos_ref, sin_ref, o_ref, a_scr = rest
    else:
        o_ref, a_scr = rest
    j = pl.program_id(1)

    @pl.when(j == 0)
    def _():
        sh = mod_ref[0, 0:1, :]
        sc = mod_ref[0, 1:2, :]
        a_scr[...] = (x_ref[...] * (1.0 + sc) + sh).astype(BF16)

    acc = _dot(a_scr[...], w_ref[...])
    if not rope_tiles:
        o_ref[...] = acc.astype(o_ref.dtype)
        return

    @pl.when(j >= rope_tiles)
    def _():
        o_ref[...] = acc.astype(o_ref.dtype)

    @pl.when(j < rope_tiles)
    def _():
        cos = cos_ref[...]
        sin = sin_ref[...]
        scale = jnp.where(j == 1, k_scale, 1.0).astype(F32)
        tn = acc.shape[1]
        for h in range(tn // (2 * LANES)):
            x1 = acc[:, h * 256:h * 256 + LANES]
            x2 = acc[:, h * 256 + LANES:(h + 1) * 256]
            o_ref[:, h * 256:h * 256 + LANES] = ((x1 * cos - x2 * sin) * scale).astype(o_ref.dtype)
            o_ref[:, h * 256 + LANES:(h + 1) * 256] = ((x1 * sin + x2 * cos) * scale).astype(o_ref.dtype)


def _modmm(rows, x, mod, w, out_dtype, tn, rope=None, k_scale=1.0):
    R, D = x.shape
    N = w.shape[1]
    tm = rows.tm
    in_specs = [pl.BlockSpec((tm, D), lambda i, j: (i, 0)),
                pl.BlockSpec((1, 6, D), lambda i, j: (rows.group(i), 0, 0)),
                pl.BlockSpec((D, tn), lambda i, j: (0, j))]
    args = [x, mod, w]
    if rope is not None:
        in_specs += [pl.BlockSpec((tm, LANES), lambda i, j: (rows.pos_block(i), 0))] * 2
        args += list(rope)
    return pl.pallas_call(
        functools.partial(_modmm_kernel, rope_tiles=2 if rope is not None else 0, k_scale=k_scale),
        grid=(rows.n_all, N // tn),
        in_specs=in_specs,
        out_specs=pl.BlockSpec((tm, tn), lambda i, j: (i, j)),
        out_shape=jax.ShapeDtypeStruct((R, N), out_dtype),
        scratch_shapes=[pltpu.VMEM((tm, D), BF16)],
        compiler_params=_cparams(("arbitrary", "arbitrary")),
        name="modmm",
    )(*args)


def _outproj_kernel(x_ref, mod_ref, y_ref, *rest, gated, alpha):
    if gated:
        g_ref, w_ref, lng_ref, lnb_ref, o_ref = rest
        g = g_ref[...].astype(F32)
        a = (_silu(g) * y_ref[...].astype(F32)).astype(BF16)
    else:
        w_ref, lng_ref, lnb_ref, o_ref = rest
        a = y_ref[...]
    acc = _dot(a, w_ref[...])
    gate = mod_ref[0, 2:3, :]
    r = alpha * x_ref[...] + gate * acc
    o_ref[...] = _layer_norm(r, lng_ref[...], lnb_ref[...])


def _outproj_ln(rows, x, mod, y, w, ln_g, ln_b, alpha, gate_src=None):
    R, D = x.shape
    K = w.shape[0]
    tm = rows.tm
    in_specs = [pl.BlockSpec((tm, D), lambda i: (i, 0)),
                pl.BlockSpec((1, 6, D), lambda i: (rows.group(i), 0, 0)),
                pl.BlockSpec((tm, K), lambda i: (i, 0))]
    args = [x, mod, y]
    if gate_src is not None:
        g_arr, g_blk = gate_src
        in_specs.append(pl.BlockSpec((tm, K), lambda i: (i, g_blk)))
        args.append(g_arr)
    in_specs += [_const_spec((K, D)), _const_spec((1, D)), _const_spec((1, D))]
    args += [w, ln_g.reshape(1, D), ln_b.reshape(1, D)]
    return pl.pallas_call(
        functools.partial(_outproj_kernel, gated=gate_src is not None, alpha=alpha),
        grid=(rows.n_all,),
        in_specs=in_specs,
        out_specs=pl.BlockSpec((tm, D), lambda i: (i, 0)),
        out_shape=jax.ShapeDtypeStruct((R, D), F32),
        compiler_params=_cparams(("arbitrary",)),
        name="outproj_ln",
    )(*args)


def _ffn_kernel(x_ref, mod_ref, w13_ref, w2_ref, lng_ref, lnb_ref, o_ref, *, alpha, fc):
    x = x_ref[...]
    sh = mod_ref[0, 3:4, :]
    sc = mod_ref[0, 4:5, :]
    gate = mod_ref[0, 5:6, :]
    a = (x * (1.0 + sc) + sh).astype(BF16)
    F = w2_ref.shape[0]
    acc = jnp.zeros(x.shape, F32)
    for c in range(F // fc):
        g = _dot(a, w13_ref[:, c * fc:(c + 1) * fc])
        u = _dot(a, w13_ref[:, F + c * fc:F + (c + 1) * fc])
        act = (_silu(g) * u).astype(BF16)
        acc = acc + _dot(act, w2_ref[c * fc:(c + 1) * fc, :])
    r = alpha * x + gate * acc
    o_ref[...] = _layer_norm(r, lng_ref[...], lnb_ref[...])


def _ffn(rows, x, mod, w13, w2, ln_g, ln_b, alpha, n_tiles):
    R, D = x.shape
    F = w2.shape[0]
    tm = rows.tm
    return pl.pallas_call(
        functools.partial(_ffn_kernel, alpha=alpha, fc=256),
        grid=(n_tiles,),
        in_specs=[pl.BlockSpec((tm, D), lambda i: (i, 0)),
                  pl.BlockSpec((1, 6, D), lambda i: (rows.group(i), 0, 0)),
                  _const_spec((D, 2 * F)), _const_spec((F, D)),
                  _const_spec((1, D)), _const_spec((1, D))],
        out_specs=pl.BlockSpec((tm, D), lambda i: (i, 0)),
        out_shape=jax.ShapeDtypeStruct((n_tiles * tm, D), F32),
        compiler_params=_cparams(("arbitrary",)),
        name="ffn",
    )(x, mod, w13, w2, ln_g.reshape(1, D), ln_b.reshape(1, D))


def _chunk_order(B, L, Lc, C, backward):
    ncc, ncl = Lc // C, L // C

    def blk(b, j):
        if backward:
            in_ctx = j < ncc
            return jnp.where(in_ctx, B * ncl + b * ncc + (ncc - 1 - j), b * ncl + (ncl - 1 - (j - ncc)))
        return jnp.where(j < ncc, B * ncl + b * ncc + j, b * ncl + (j - ncc))

    return blk, ncc + ncl


def _ret_scan_kernel(lg_ref, q_ref, k_ref, v_ref, *rest, backward, C):
    if backward:
        of_ref, o_ref, st_scr = rest
    else:
        o_ref, st_scr = rest
    h = pl.program_id(1)
    j = pl.program_id(2)
    lg = lg_ref[1 if backward else 0, h]

    @pl.when(j == 0)
    def _():
        st_scr[...] = jnp.zeros_like(st_scr)

    q = q_ref[...]
    k = k_ref[...]
    v = v_ref[...]
    ti = lax.broadcasted_iota(jnp.int32, (C, C), 0)
    si = lax.broadcasted_iota(jnp.int32, (C, C), 1)
    dist = (si - ti) if backward else (ti - si)
    decay = jnp.where(dist >= 0, jnp.exp(lg * jnp.maximum(dist, 0).astype(F32)), 0.0)
    att = (_dot_nt(q, k) * decay).astype(BF16)
    tcol = lax.broadcasted_iota(jnp.int32, (C, 1), 0)
    eq = (C - tcol) if backward else (tcol + 1)
    ek = tcol if backward else (C - 1 - tcol)
    st = st_scr[...]
    o = _dot(att, v) + _dot_nt(q, st.astype(BF16)) * jnp.exp(lg * eq.astype(F32))
    kd = (k.astype(F32) * jnp.exp(lg * ek.astype(F32))).astype(BF16)
    st_scr[...] = st * jnp.exp(lg * jnp.full((1, 1), float(C), F32)) + _dot_tn(v, kd)
    if backward:
        tot = of_ref[...] + o
        y = tot * lax.rsqrt(jnp.mean(tot * tot, axis=-1, keepdims=True) + RMS_EPS)
        o_ref[...] = y.astype(o_ref.dtype)
    else:
        o_ref[...] = o


def _ret_scan(proj, lg, B, L, Lc):
    R = proj.shape[0]
    C = RET_CHUNK
    H, dk, dv = RET_HEADS, 256, 512
    outs = None
    for backward in (False, True):
        blk, nch = _chunk_order(B, L, Lc, C, backward)
        in_specs = [pl.BlockSpec(memory_space=pltpu.SMEM),
                    pl.BlockSpec((C, dk), lambda b, h, j: (blk(b, j), h)),
                    pl.BlockSpec((C, dk), lambda b, h, j: (blk(b, j), H + h)),
                    pl.BlockSpec((C, dv), lambda b, h, j: (blk(b, j), H + h))]
        args = [lg, proj, proj, proj]
        if backward:
            in_specs.append(pl.BlockSpec((C, dv), lambda b, h, j: (blk(b, j), h)))
            args.append(outs)
        outs = pl.pallas_call(
            functools.partial(_ret_scan_kernel, backward=backward, C=C),
            grid=(B, H, nch),
            in_specs=in_specs,
            out_specs=pl.BlockSpec((C, dv), lambda b, h, j: (blk(b, j), h)),
            out_shape=jax.ShapeDtypeStruct((R, H * dv), BF16 if backward else F32),
            scratch_shapes=[pltpu.VMEM((dv, dk), F32)],
            compiler_params=_cparams(("arbitrary", "arbitrary", "arbitrary")),
            name="ret_scan_bwd" if backward else "ret_scan_fwd",
        )(*args)
    return outs


def _cumsum_rows(x):
    n = x.shape[0]
    r = lax.broadcasted_iota(jnp.int32, (n, 1), 0)
    s = 1
    while s < n:
        x = x + jnp.where(r >= s, pltpu.roll(x, s, axis=0), 0.0)
        s *= 2
    return x


def _hg_scan_kernel(q_ref, f_ref, v_ref, lb_ref, *rest, backward, C, heads, scale):
    if backward:
        of_ref, ng_ref, o_ref, st_scr = rest
    else:
        o_ref, st_scr = rest
    j = pl.program_id(1)

    @pl.when(j == 0)
    def _():
        st_scr[...] = jnp.zeros_like(st_scr)

    lb = lb_ref[...]
    forget = lb + (1.0 - lb) * jax.nn.sigmoid(f_ref[...])
    kk = 1.0 - forget
    gl = jnp.log(forget)
    pre = _cumsum_rows(gl)
    tot = pre[C - 1:C, :]
    bc = (tot - pre + gl) if backward else pre
    qd = (_silu(q_ref[...]) * scale * jnp.exp(bc)).astype(BF16)
    kd = (kk * jnp.exp(-bc)).astype(BF16)
    ke = (kk * jnp.exp(tot - bc)).astype(BF16)
    v = v_ref[...].astype(BF16)
    etot = jnp.exp(tot)
    ti = lax.broadcasted_iota(jnp.int32, (C, C), 0)
    si = lax.broadcasted_iota(jnp.int32, (C, C), 1)
    keep = (si >= ti) if backward else (ti >= si)
    d = HG_EXPAND
    for h in range(heads):
        sl = slice(h * d, (h + 1) * d)
        att = jnp.where(keep, _dot_nt(qd[:, sl], kd[:, sl]), 0.0).astype(BF16)
        st = st_scr[h]
        o = _dot(att, v[:, sl]) + _dot_nt(qd[:, sl], st.astype(BF16))
        st_scr[h] = st * etot[:, sl] + _dot_tn(v[:, sl], ke[:, sl])
        if backward:
            t = of_ref[:, sl] + o
            y = t * lax.rsqrt(jnp.mean(t * t, axis=-1, keepdims=True) + RMS_EPS) * ng_ref[...]
            o_ref[:, sl] = y.astype(o_ref.dtype)
        else:
            o_ref[:, sl] = o


def _hg_scan(proj, lb, norm_g, B, L, Lc):
    R = proj.shape[0]
    C = HG_CHUNK
    Dm = proj.shape[1] // 5
    heads = Dm // HG_EXPAND
    outs = None
    for backward in (False, True):
        blk, nch = _chunk_order(B, L, Lc, C, backward)
        fcol = 2 if backward else 1
        in_specs = [pl.BlockSpec((C, Dm), lambda b, j: (blk(b, j), 0)),
                    pl.BlockSpec((C, Dm), lambda b, j: (blk(b, j), fcol)),
                    pl.BlockSpec((C, Dm), lambda b, j: (blk(b, j), 3)),
                    pl.BlockSpec((1, Dm), lambda b, j: (0, 0))]
        args = [proj, proj, proj, lb.reshape(1, Dm)]
        if backward:
            in_specs += [pl.BlockSpec((C, Dm), lambda b, j: (blk(b, j), 0)),
                         pl.BlockSpec((1, HG_EXPAND), lambda b, j: (0, 0))]
            args += [outs, norm_g.reshape(1, HG_EXPAND)]
        outs = pl.pallas_call(
            functools.partial(_hg_scan_kernel, backward=backward, C=C, heads=heads,
                              scale=HG_EXPAND ** -0.5),
            grid=(B, nch),
            in_specs=in_specs,
            out_specs=pl.BlockSpec((C, Dm), lambda b, j: (blk(b, j), 0)),
            out_shape=jax.ShapeDtypeStruct((R, Dm), BF16 if backward else F32),
            scratch_shapes=[pltpu.VMEM((heads, HG_EXPAND, HG_EXPAND), F32)],
            compiler_params=_cparams(("arbitrary", "arbitrary")),
            name="hg_scan_bwd" if backward else "hg_scan_fwd",
        )(*args)
    return outs


def _softmax_pv(scores, values):
    m = functools.reduce(jnp.maximum, [jnp.max(s, axis=-1, keepdims=True) for s in scores])
    ps = [jnp.exp(s - m) for s in scores]
    l = functools.reduce(lambda a, b: a + b, [jnp.sum(p, axis=-1, keepdims=True) for p in ps])
    o = functools.reduce(lambda a, b: a + b, [_dot(p.astype(BF16), v) for p, v in zip(ps, values)])
    return o / l


def _lane_lo(shape):
    return lax.broadcasted_iota(jnp.int32, shape, 1) < (LANES // 2)


def _na_kernel(q_ref, kw_ref, vw_ref, kc_ref, vc_ref, bias_ref, o_ref, *, heads):
    q = q_ref[...] * 0.125
    lo = _lane_lo((q.shape[0], LANES))
    for hp in range(heads // 2):
        sl = slice(hp * LANES, (hp + 1) * LANES)
        q2, kw, vw, kc, vc = q[:, sl], kw_ref[:, sl], vw_ref[:, sl], kc_ref[:, sl], vc_ref[:, sl]
        outs = []
        for e in range(2):
            qm = jnp.where(lo if e == 0 else ~lo, q2, jnp.zeros_like(q2))
            s_lat = _dot_nt(qm, kw) + bias_ref[2 * hp + e, 0]
            s_ctx = _dot_nt(qm, kc)
            outs.append(_softmax_pv([s_lat, s_ctx], [vw, vc]))
        o_ref[:, sl] = jnp.where(lo, outs[0], outs[1]).astype(o_ref.dtype)


def _na_bias_table(rpb):
    H = rpb.shape[0]
    W, wr, wc = GRID_W, NA_WIN_ROWS, NA_WIN_COLS
    qcol = np.arange(W)[:, None]
    kcol = np.arange(W)[None, :]
    ws = np.clip(qcol - wc // 2, 0, W - wc)
    ok = (kcol >= ws) & (kcol < ws + wc)
    r_pad = jnp.pad(rpb.astype(F32), ((0, 0), (0, 0), (W - wc, W + wc - (2 * wc - 1))))
    skew = jnp.tile(r_pad, (1, 1, W))[:, :, :W * (2 * W - 1)].reshape(H, 2 * wr - 1, W, 2 * W - 1)
    t15 = jnp.where(ok[None, None], skew[:, :, :, W - 1:], NEG_INF)
    tabs = [t15[:, wr - 1 - d:2 * wr - 1 - d].transpose(0, 2, 1, 3).reshape(H, W, wr * W)
            for d in range(wr)]
    return jnp.stack(tabs, 1)


def _na_attention(qkv, bias, B, L, Lc):
    D = qkv.shape[1] // 3
    W, wr = GRID_W, NA_WIN_ROWS
    nrows = L // W
    assert nrows >= wr
    rs = lambda r: jnp.clip(r - wr // 2, 0, nrows - wr)
    ctx_blk0 = B * L // Lc
    return pl.pallas_call(
        functools.partial(_na_kernel, heads=NA_HEADS),
        grid=(B, nrows),
        in_specs=[pl.BlockSpec((W, D), lambda b, r: (b * nrows + r, 0)),
                  pl.BlockSpec((pl.Element(wr * W), pl.Element(D)),
                               lambda b, r: ((b * nrows + rs(r)) * W, D)),
                  pl.BlockSpec((pl.Element(wr * W), pl.Element(D)),
                               lambda b, r: ((b * nrows + rs(r)) * W, 2 * D)),
                  pl.BlockSpec((Lc, D), lambda b, r: (ctx_blk0 + b, 1)),
                  pl.BlockSpec((Lc, D), lambda b, r: (ctx_blk0 + b, 2)),
                  pl.BlockSpec((NA_HEADS, 1, W, wr * W), lambda b, r: (0, r - rs(r), 0, 0))],
        out_specs=pl.BlockSpec((W, D), lambda b, r: (b * nrows + r, 0)),
        out_shape=jax.ShapeDtypeStruct((B * L, D), BF16),
        compiler_params=_cparams(("arbitrary", "arbitrary")),
        name="na_attn",
    )(qkv, qkv, qkv, qkv, qkv, bias)


def _ctx_attn_kernel(q_ref, k_ref, v_ref, o_ref, *, heads, split64, q_scale):
    lo = _lane_lo((q_ref.shape[0], LANES))
    for hp in range(heads // 2):
        vsl = slice(hp * LANES, (hp + 1) * LANES)
        v2 = v_ref[:, vsl]
        outs = []
        for e in range(2):
            if split64:
                q2 = q_ref[:, vsl] * q_scale
                qh = jnp.where(lo if e == 0 else ~lo, q2, jnp.zeros_like(q2))
                kh = k_ref[:, vsl]
            else:
                hsl = slice((2 * hp + e) * LANES, (2 * hp + e + 1) * LANES)
                qh, kh = q_ref[:, hsl], k_ref[:, hsl]
            outs.append(_softmax_pv([_dot_nt(qh, kh)], [v2]))
        o_ref[:, vsl] = jnp.where(lo, outs[0], outs[1]).astype(o_ref.dtype)


def _ctx_attention(q_src, k_src, v_src, B, L, Lc, heads, split64, q_scale=1.0):
    blk0 = B * L // Lc
    spec = lambda src: pl.BlockSpec((Lc, src[1]), lambda b: (blk0 + b, src[2]))
    Dv = v_src[1]
    return pl.pallas_call(
        functools.partial(_ctx_attn_kernel, heads=heads, split64=split64, q_scale=q_scale),
        grid=(B,),
        in_specs=[spec(q_src), spec(k_src), spec(v_src)],
        out_specs=pl.BlockSpec((Lc, Dv), lambda b: (b, 0)),
        out_shape=jax.ShapeDtypeStruct((B * Lc, Dv), BF16),
        compiler_params=_cparams(("arbitrary",)),
        name="ctx_attn",
    )(q_src[0], k_src[0], v_src[0])


def _rope_slot(x, cos, sina, sinb):
    return x * cos + pltpu.roll(x, LANES - 16, axis=1) * sina + pltpu.roll(x, 16, axis=1) * sinb


def _mla_proj_kernel(x_ref, mod_ref, wd_ref, qn_ref, kvn_ref, wq_ref, wk_ref, wvt_ref, vone_ref,
                     cos_ref, sina_ref, sinb_ref, q_ref, k_ref, vt_ref, *, heads, scale):
    sh = mod_ref[0, 0:1, :]
    sc = mod_ref[0, 1:2, :]
    a = (x_ref[...] * (1.0 + sc) + sh).astype(BF16)
    d = _dot(a, wd_ref[...])
    cq = d[:, :MLA_Q_LORA]
    ckv = d[:, MLA_Q_LORA:MLA_Q_LORA + MLA_KV_LORA]
    kr = d[:, MLA_Q_LORA + MLA_KV_LORA:]
    rms = lambda t, g: (t * lax.rsqrt(jnp.mean(t * t, axis=-1, keepdims=True) + RMS_EPS) * g).astype(BF16)
    cqn = rms(cq, qn_ref[...])
    ckvn = rms(ckv, kvn_ref[...])
    cos, sina, sinb = cos_ref[...], sina_ref[...], sinb_ref[...]
    krr = _rope_slot(kr, cos, sina, sinb)
    qf = _dot(cqn, wq_ref[...])
    kf = _dot(ckvn, wk_ref[...])
    for h in range(heads):
        sl = slice(h * LANES, (h + 1) * LANES)
        q_ref[:, sl] = (_rope_slot(qf[:, sl], cos, sina, sinb) * scale).astype(q_ref.dtype)
        k_ref[:, sl] = (kf[:, sl] + krr).astype(k_ref.dtype)
    vt_ref[...] = (_dot_nt(wvt_ref[...], ckvn) + vone_ref[...]).astype(vt_ref.dtype)


def _mla_proj(rows, x, mod, wd, qn, kvn, wq, wk, wvt, vone, tables):
    R, D = x.shape
    tm = rows.tm
    H = MLA_HEADS
    tab_spec = pl.BlockSpec((tm, LANES), lambda i: (rows.pos_block(i), 0))
    return pl.pallas_call(
        functools.partial(_mla_proj_kernel, heads=H,
                          scale=(MLA_NOPE + MLA_ROPE) ** -0.5 * float(np.log2(np.e))),
        grid=(rows.n_all,),
        in_specs=[pl.BlockSpec((tm, D), lambda i: (i, 0)),
                  pl.BlockSpec((1, 6, D), lambda i: (rows.group(i), 0, 0)),
                  _const_spec(wd.shape), _const_spec((1, MLA_Q_LORA)), _const_spec((1, MLA_KV_LORA)),
                  _const_spec(wq.shape), _const_spec(wk.shape), _const_spec(wvt.shape),
                  _const_spec(vone.shape), tab_spec, tab_spec, tab_spec],
        out_specs=[pl.BlockSpec((tm, H * LANES), lambda i: (i, 0)),
                   pl.BlockSpec((tm, H * LANES), lambda i: (i, 0)),
                   pl.BlockSpec((H * LANES, tm), lambda i: (0, i))],
        out_shape=[jax.ShapeDtypeStruct((R, H * LANES), BF16),
                   jax.ShapeDtypeStruct((R, H * LANES), BF16),
                   jax.ShapeDtypeStruct((H * LANES, R), BF16)],
        compiler_params=_cparams(("arbitrary",)),
        name="mla_proj",
    )(x, mod, wd, qn.reshape(1, -1), kvn.reshape(1, -1), wq, wk, wvt, vone, *tables)


def _mla_flash_kernel(q_ref, kc_ref, vtc_ref, *rest, tk, with_latent):
    if with_latent:
        k_ref, vt_ref, o_ref, sa_ref, sb_ref = rest
    else:
        (o_ref,) = rest
    tq = q_ref.shape[0]
    hsl = [slice(e * LANES, (e + 1) * LANES) for e in range(2)]
    qs = [q_ref[:, sl] for sl in hsl]

    def scores(kblk):
        return [_dot_nt(kblk[:, hsl[e]], qs[e]) for e in range(2)]

    def update(ss, vtblk, state):
        out = []
        for e in range(2):
            m, acc = state[e]
            s = ss[e]
            m_new = jnp.maximum(m, jnp.max(s, axis=0, keepdims=True))
            p = jnp.exp2(s - m_new).astype(BF16)
            acc = jnp.exp2(m - m_new) * acc + _dot(vtblk[hsl[e], :], p)
            out.append((m_new, acc))
        return tuple(out)

    init = (jnp.full((1, tq), NEG_INF, F32), jnp.zeros((LANES, tq), F32))
    state = update(scores(kc_ref[...]), vtc_ref[...], (init, init))
    if with_latent:
        n = k_ref.shape[0] // tk
        assert n % 2 == 0

        def put(buf, c):
            ss = scores(k_ref[pl.ds(pl.multiple_of(c * tk, tk), tk), :])
            for e in range(2):
                buf[e] = ss[e]

        def take(buf, c, st):
            vtblk = vt_ref[:, pl.ds(pl.multiple_of(c * tk, tk), tk)]
            return update([buf[0], buf[1]], vtblk, st)

        put(sa_ref, 0)

        def body(i, st):
            put(sb_ref, 2 * i + 1)
            st = take(sa_ref, 2 * i, st)
            put(sa_ref, 2 * i + 2)
            return take(sb_ref, 2 * i + 1, st)

        state = lax.fori_loop(0, n // 2 - 1, body, state)
        put(sb_ref, n - 1)
        state = take(sa_ref, n - 2, state)
        state = take(sb_ref, n - 1, state)
    ot = jnp.concatenate([acc[:MLA_V] / acc[MLA_V:MLA_V + 1] for _, acc in state], axis=0)
    o_ref[...] = ot.T.astype(o_ref.dtype)


def _mla_attention(q, k, vt, B, L, Lc, tq, tk, latent_queries):
    H = MLA_HEADS
    ctx_blk0 = B * L // Lc
    if latent_queries:
        nq, q0, n_out = L // tq, 0, B * L
    else:
        assert tq == Lc
        nq, q0, n_out = 1, ctx_blk0, B * Lc
    in_specs = [pl.BlockSpec((tq, 2 * LANES), lambda b, hp, i: (q0 + b * nq + i, hp)),
                pl.BlockSpec((Lc, 2 * LANES), lambda b, hp, i: (ctx_blk0 + b, hp)),
                pl.BlockSpec((2 * LANES, Lc), lambda b, hp, i: (hp, ctx_blk0 + b))]
    args = [q, k, vt]
    scratch = []
    if latent_queries:
        in_specs += [pl.BlockSpec((L, 2 * LANES), lambda b, hp, i: (b, hp)),
                     pl.BlockSpec((2 * LANES, L), lambda b, hp, i: (hp, b))]
        args += [k, vt]
        scratch = [pltpu.VMEM((2, tk, tq), F32)] * 2
    return pl.pallas_call(
        functools.partial(_mla_flash_kernel, tk=tk, with_latent=latent_queries),
        grid=(B, H // 2, nq),
        in_specs=in_specs,
        out_specs=pl.BlockSpec((tq, LANES), lambda b, hp, i: (b * nq + i, hp)),
        out_shape=jax.ShapeDtypeStruct((n_out, H * MLA_V), BF16),
        scratch_shapes=scratch,
        compiler_params=_cparams(("arbitrary", "arbitrary", "arbitrary")),
        name="mla_flash" if latent_queries else "mla_flash_ctx",
    )(*args)


def _axial_angles(L, rot_dim):
    t = jnp.arange(L)
    rows = (t // GRID_W).astype(F32)
    cols = (t % GRID_W).astype(F32)
    n_freq = rot_dim // 4
    inv = ROPE_BASE ** (-jnp.arange(n_freq, dtype=F32) / n_freq)
    return jnp.concatenate([rows[:, None] * inv, cols[:, None] * inv], -1)


def _ret_rope_tables(L, tm):
    ang = _axial_angles(L, 256)
    cos = jnp.concatenate([jnp.cos(ang), jnp.ones((tm, LANES), F32)], 0)
    sin = jnp.concatenate([jnp.sin(ang), jnp.zeros((tm, LANES), F32)], 0)
    return cos, sin


def _mla_rope_tables(L, tm):
    ang = _axial_angles(L, MLA_ROPE)
    c, s = jnp.cos(ang), jnp.sin(ang)
    one = jnp.ones((L, MLA_NOPE), F32)
    z16 = jnp.zeros((L, 16), F32)
    z32 = jnp.zeros((L, 32), F32)
    z64 = jnp.zeros((L, MLA_NOPE), F32)
    cos = jnp.concatenate([one, c, c, jnp.ones((L, 32), F32)], -1)
    sina = jnp.concatenate([z64, -s, z16, z32], -1)
    sinb = jnp.concatenate([z64, z16, s, z32], -1)
    ident = lambda t, fill: jnp.concatenate([t, jnp.full((tm, LANES), fill, F32)], 0)
    return ident(cos, 1.0), ident(sina, 0.0), ident(sinb, 0.0)


def kernel(x, c, ctx, c_ctx, ada_w, ada_b, ln_g, ln_b, ffn_w13, ffn_w2, ret_w_in, ret_decay, ret_w_out, na_w_qkv, na_rpb, na_w_out, mla_w_down, mla_q_norm, mla_kv_norm, mla_w_uq, mla_w_ukv, mla_w_out, hg_w_in, hg_lower_bounds, hg_norm_g, hg_w_out):
    B, L, D = x.shape
    Lc = ctx.shape[1]
    depth = ada_w.shape[0]
    alpha = (2 * depth) ** 0.25
    tm = 512 if (B * Lc) % 512 == 0 else 256
    rows = _Rows(B, L, Lc, tm)
    n_lat_rows = B * L

    h = jnp.concatenate([x.reshape(B * L, D), ctx.reshape(B * Lc, D)], 0)

    G = 8 * (-(-(B + 1) // 8))
    cond_in = jnp.zeros((G, D), F32).at[0].set(c_ctx).at[1:B + 1].set(c)
    mods = _adaln(cond_in, ada_w, ada_b).reshape(depth, G, 6, D)

    bf = lambda w: w.astype(BF16)
    H = MLA_HEADS
    wd = jnp.zeros((D, MLA_Q_LORA + MLA_KV_LORA + LANES), F32)
    wd = wd.at[:, :MLA_Q_LORA + MLA_KV_LORA].set(mla_w_down[:, :MLA_Q_LORA + MLA_KV_LORA])
    wd = wd.at[:, MLA_Q_LORA + MLA_KV_LORA + MLA_NOPE:MLA_Q_LORA + MLA_KV_LORA + MLA_NOPE + MLA_ROPE].set(
        mla_w_down[:, MLA_Q_LORA + MLA_KV_LORA:])
    wq = jnp.pad(mla_w_uq.reshape(MLA_Q_LORA, H, MLA_NOPE + MLA_ROPE),
                 ((0, 0), (0, 0), (0, LANES - MLA_NOPE - MLA_ROPE))).reshape(MLA_Q_LORA, H * LANES)
    wukv = mla_w_ukv.reshape(MLA_KV_LORA, H, MLA_NOPE + MLA_V)
    wk = jnp.pad(wukv[:, :, :MLA_NOPE], ((0, 0), (0, 0), (0, LANES - MLA_NOPE))).reshape(MLA_KV_LORA, H * LANES)
    wvt = jnp.pad(wukv[:, :, MLA_NOPE:], ((0, 0), (0, 0), (0, LANES - MLA_V))).reshape(MLA_KV_LORA, H * LANES).T
    vone = jnp.tile((jnp.arange(LANES) == MLA_V).astype(F32), H).reshape(H * LANES, 1)

    for i in range(depth):
        mod = mods[i]
        kind = i % 4
        want_ctx = i < depth - 1
        if kind == 0:
            lg = -jnp.exp(ret_decay.astype(F32))
            proj = _modmm(rows, h, mod, bf(ret_w_in), BF16, 1024,
                          rope=_ret_rope_tables(L, tm), k_scale=256 ** -0.5)
            y = _ret_scan(proj, lg, B, L, Lc)
            h1 = _outproj_ln(rows, h, mod, y, bf(ret_w_out), ln_g[i, 0], ln_b[i, 0], alpha,
                             gate_src=(proj, 2))
        elif kind == 1:
            qkv = _modmm(rows, h, mod, bf(na_w_qkv), BF16, 1024)
            o_lat = _na_attention(qkv, _na_bias_table(na_rpb), B, L, Lc)
            o_ctx = _ctx_attention((qkv, D, 0), (qkv, D, 1), (qkv, D, 2), B, L, Lc, NA_HEADS,
                                   split64=True, q_scale=0.125)
            y = jnp.concatenate([o_lat, o_ctx], 0)
            h1 = _outproj_ln(rows, h, mod, y, bf(na_w_out), ln_g[i, 0], ln_b[i, 0], alpha)
        elif kind == 2:
            q, k, vt = _mla_proj(rows, h, mod, bf(wd), mla_q_norm, mla_kv_norm, bf(wq), bf(wk), bf(wvt), vone,
                                 _mla_rope_tables(L, tm))
            tk = 512 if L % 512 == 0 else L
            o_lat = _mla_attention(q, k, vt, B, L, Lc, 256, tk, latent_queries=True)
            o_ctx = _mla_attention(q, k, vt, B, L, Lc, Lc, tk, latent_queries=False)
            y = jnp.concatenate([o_lat, o_ctx], 0)
            h1 = _outproj_ln(rows, h, mod, y, bf(mla_w_out), ln_g[i, 0], ln_b[i, 0], alpha)
        else:
            lb_soft = jax.nn.softmax(hg_lower_bounds.astype(F32), axis=0)
            lb = (jnp.cumsum(lb_soft, axis=0) - lb_soft[0])[i]
            proj = _modmm(rows, h, mod, bf(hg_w_in), F32, 1024)
            y = _hg_scan(proj, lb, hg_norm_g, B, L, Lc)
            h1 = _outproj_ln(rows, h, mod, y, bf(hg_w_out), ln_g[i, 0], ln_b[i, 0], alpha,
                             gate_src=(proj, 4))
        n_tiles = rows.n_all if want_ctx else rows.n_lat
        h = _ffn(rows, h1, mod, bf(ffn_w13[i]), bf(ffn_w2[i]), ln_g[i, 1], ln_b[i, 1], alpha, n_tiles)
    return h[:n_lat_rows].reshape(B, L, D)
```

```python
import functools

import numpy as np
import jax
import jax.numpy as jnp
from jax import lax
from jax.experimental import pallas as pl
from jax.experimental.pallas import tpu as pltpu

F32 = jnp.float32
BF16 = jnp.bfloat16

GRID_W = 64
LN_EPS = 1e-5
RMS_EPS = 1e-6
ROPE_BASE = 10000.0
NEG_INF = -1e30

RET_HEADS = 4
NA_HEADS = 16
NA_WIN_ROWS = 8
NA_WIN_COLS = 16
MLA_HEADS = 16
MLA_NOPE = 64
MLA_ROPE = 32
MLA_V = 64
MLA_Q_LORA = 512
MLA_KV_LORA = 256
HG_EXPAND = 128
HG_CHUNK = 64
RET_CHUNK = 256

VMEM_LIMIT = 56 * 1024 * 1024
LANES = 128


def _cparams(sem):
    return pltpu.CompilerParams(dimension_semantics=sem, vmem_limit_bytes=VMEM_LIMIT)


def _const_spec(shape):
    nd = len(shape)
    return pl.BlockSpec(shape, lambda *_: (0,) * nd, pipeline_mode=pl.Buffered(1))


def _dot(a, b):
    return jnp.dot(a, b, preferred_element_type=F32)


def _dot_nt(a, b):
    return lax.dot_general(a, b, (((1,), (1,)), ((), ())), preferred_element_type=F32)


def _dot_tn(a, b):
    return lax.dot_general(a, b, (((0,), (0,)), ((), ())), preferred_element_type=F32)


def _silu(x):
    return x * jax.nn.sigmoid(x)


def _layer_norm(r, g, b):
    mu = jnp.mean(r, axis=-1, keepdims=True)
    rc = r - mu
    var = jnp.mean(rc * rc, axis=-1, keepdims=True)
    return rc * lax.rsqrt(var + LN_EPS) * g + b


class _Rows:
    def __init__(self, B, L, Lc, tm):
        assert L % tm == 0 and (B * Lc) % tm == 0
        self.B, self.L, self.Lc, self.tm = B, L, Lc, tm
        self.n_lat = B * L // tm
        self.n_all = (B * L + B * Lc) // tm
        self.per_b = L // tm

    def group(self, i):
        return jnp.where(i < self.n_lat, 1 + i // self.per_b, 0)

    def pos_block(self, i):
        return jnp.where(i < self.n_lat, i % self.per_b, self.per_b)


def _adaln_kernel(c_ref, w_ref, b_ref, o_ref):
    cond = _silu(c_ref[...])
    o_ref[0] = jnp.dot(cond, w_ref[0], preferred_element_type=F32,
                       precision=lax.Precision.HIGHEST) + b_ref[0]


def _adaln(cond_in, ada_w, ada_b):
    depth, D, N = ada_w.shape
    G = cond_in.shape[0]
    tn = 1536
    return pl.pallas_call(
        _adaln_kernel,
        grid=(depth, N // tn),
        in_specs=[pl.BlockSpec((G, D), lambda l, j: (0, 0)),
                  pl.BlockSpec((1, D, tn), lambda l, j: (l, 0, j)),
                  pl.BlockSpec((1, 1, tn), lambda l, j: (l, 0, j))],
        out_specs=pl.BlockSpec((1, G, tn), lambda l, j: (l, 0, j)),
        out_shape=jax.ShapeDtypeStruct((depth, G, N), F32),
        compiler_params=_cparams(("arbitrary", "arbitrary")),
        name="adaln",
    )(cond_in, ada_w, ada_b.reshape(depth, 1, N))


def _modmm_kernel(x_ref, mod_ref, w_ref, *rest, rope_tiles, k_scale, vt_tile):
    vt_ref = None
    if rope_tiles:
        cos_ref, sin_ref, o_ref, a_scr = rest
    elif vt_tile is not None:
        o_ref, vt_ref, a_scr = rest
    else:
        o_ref, a_scr = rest
    j = pl.program_id(1)

    @pl.when(j == 0)
    def _():
        sh = mod_ref[0, 0:1, :]
        sc = mod_ref[0, 1:2, :]
        a_scr[...] = (x_ref[...] * (1.0 + sc) + sh).astype(BF16)

    acc = _dot(a_scr[...], w_ref[...])
    if vt_ref is not None:
        @pl.when(j == vt_tile)
        def _():
            vt_ref[...] = acc.T.astype(vt_ref.dtype)

    if not rope_tiles:
        o_ref[...] = acc.astype(o_ref.dtype)
        return

    @pl.when(j >= rope_tiles)
    def _():
        o_ref[...] = acc.astype(o_ref.dtype)

    @pl.when(j < rope_tiles)
    def _():
        cos = cos_ref[...]
        sin = sin_ref[...]
        scale = jnp.where(j == 1, k_scale, 1.0).astype(F32)
        tn = acc.shape[1]
        for h in range(tn // (2 * LANES)):
            x1 = acc[:, h * 256:h * 256 + LANES]
            x2 = acc[:, h * 256 + LANES:(h + 1) * 256]
            o_ref[:, h * 256:h * 256 + LANES] = ((x1 * cos - x2 * sin) * scale).astype(o_ref.dtype)
            o_ref[:, h * 256 + LANES:(h + 1) * 256] = ((x1 * sin + x2 * cos) * scale).astype(o_ref.dtype)


def _modmm(rows, x, mod, w, out_dtype, tn, rope=None, k_scale=1.0, vt_tile=None):
    R, D = x.shape
    N = w.shape[1]
    tm = rows.tm
    out_specs = pl.BlockSpec((tm, tn), lambda i, j: (i, j))
    out_shape = jax.ShapeDtypeStruct((R, N), out_dtype)
    if vt_tile is not None:
        out_specs = [out_specs, pl.BlockSpec((tn, tm), lambda i, j: (0, i))]
        out_shape = [out_shape, jax.ShapeDtypeStruct((tn, R), out_dtype)]
    in_specs = [pl.BlockSpec((tm, D), lambda i, j: (i, 0)),
                pl.BlockSpec((1, 6, D), lambda i, j: (rows.group(i), 0, 0)),
                pl.BlockSpec((D, tn), lambda i, j: (0, j))]
    args = [x, mod, w]
    if rope is not None:
        in_specs += [pl.BlockSpec((tm, LANES), lambda i, j: (rows.pos_block(i), 0))] * 2
        args += list(rope)
    return pl.pallas_call(
        functools.partial(_modmm_kernel, rope_tiles=2 if rope is not None else 0, k_scale=k_scale,
                          vt_tile=vt_tile),
        grid=(rows.n_all, N // tn),
        in_specs=in_specs,
        out_specs=out_specs,
        out_shape=out_shape,
        scratch_shapes=[pltpu.VMEM((tm, D), BF16)],
        compiler_params=_cparams(("arbitrary", "arbitrary")),
        name="modmm",
    )(*args)


def _outproj_kernel(x_ref, mod_ref, y_ref, *rest, gated, alpha):
    if gated:
        g_ref, w_ref, lng_ref, lnb_ref, o_ref = rest
        g = g_ref[...].astype(F32)
        a = (_silu(g) * y_ref[...].astype(F32)).astype(BF16)
    else:
        w_ref, lng_ref, lnb_ref, o_ref = rest
        a = y_ref[...]
    acc = _dot(a, w_ref[...])
    gate = mod_ref[0, 2:3, :]
    r = alpha * x_ref[...] + gate * acc
    o_ref[...] = _layer_norm(r, lng_ref[...], lnb_ref[...])


def _outproj_ln(rows, x, mod, y, w, ln_g, ln_b, alpha, gate_src=None):
    R, D = x.shape
    K = w.shape[0]
    tm = rows.tm
    in_specs = [pl.BlockSpec((tm, D), lambda i: (i, 0)),
                pl.BlockSpec((1, 6, D), lambda i: (rows.group(i), 0, 0)),
                pl.BlockSpec((tm, K), lambda i: (i, 0))]
    args = [x, mod, y]
    if gate_src is not None:
        g_arr, g_blk = gate_src
        in_specs.append(pl.BlockSpec((tm, K), lambda i: (i, g_blk)))
        args.append(g_arr)
    in_specs += [_const_spec((K, D)), _const_spec((1, D)), _const_spec((1, D))]
    args += [w, ln_g.reshape(1, D), ln_b.reshape(1, D)]
    return pl.pallas_call(
        functools.partial(_outproj_kernel, gated=gate_src is not None, alpha=alpha),
        grid=(rows.n_all,),
        in_specs=in_specs,
        out_specs=pl.BlockSpec((tm, D), lambda i: (i, 0)),
        out_shape=jax.ShapeDtypeStruct((R, D), F32),
        compiler_params=_cparams(("arbitrary",)),
        name="outproj_ln",
    )(*args)


def _ffn_kernel(x_ref, mod_ref, w13_ref, w2_ref, lng_ref, lnb_ref, o_ref, *, alpha, fc):
    x = x_ref[...]
    sh = mod_ref[0, 3:4, :]
    sc = mod_ref[0, 4:5, :]
    gate = mod_ref[0, 5:6, :]
    a = (x * (1.0 + sc) + sh).astype(BF16)
    F = w2_ref.shape[0]
    acc = jnp.zeros(x.shape, F32)
    for c in range(F // fc):
        g = _dot(a, w13_ref[:, c * fc:(c + 1) * fc])
        u = _dot(a, w13_ref[:, F + c * fc:F + (c + 1) * fc])
        act = (_silu(g) * u).astype(BF16)
        acc = acc + _dot(act, w2_ref[c * fc:(c + 1) * fc, :])
    r = alpha * x + gate * acc
    o_ref[...] = _layer_norm(r, lng_ref[...], lnb_ref[...])


def _ffn(rows, x, mod, w13, w2, ln_g, ln_b, alpha, n_tiles):
    R, D = x.shape
    F = w2.shape[0]
    tm = rows.tm
    return pl.pallas_call(
        functools.partial(_ffn_kernel, alpha=alpha, fc=256),
        grid=(n_tiles,),
        in_specs=[pl.BlockSpec((tm, D), lambda i: (i, 0)),
                  pl.BlockSpec((1, 6, D), lambda i: (rows.group(i), 0, 0)),
                  _const_spec((D, 2 * F)), _const_spec((F, D)),
                  _const_spec((1, D)), _const_spec((1, D))],
        out_specs=pl.BlockSpec((tm, D), lambda i: (i, 0)),
        out_shape=jax.ShapeDtypeStruct((n_tiles * tm, D), F32),
        compiler_params=_cparams(("arbitrary",)),
        name="ffn",
    )(x, mod, w13, w2, ln_g.reshape(1, D), ln_b.reshape(1, D))


def _chunk_order(B, L, Lc, C, backward):
    ncc, ncl = Lc // C, L // C

    def blk(b, j):
        if backward:
            in_ctx = j < ncc
            return jnp.where(in_ctx, B * ncl + b * ncc + (ncc - 1 - j), b * ncl + (ncl - 1 - (j - ncc)))
        return jnp.where(j < ncc, B * ncl + b * ncc + j, b * ncl + (j - ncc))

    return blk, ncc + ncl


def _ret_scan_kernel(lg_ref, q_ref, k_ref, v_ref, *rest, backward, C):
    if backward:
        of_ref, o_ref, st_scr = rest
    else:
        o_ref, st_scr = rest
    h = pl.program_id(1)
    j = pl.program_id(2)
    lg = lg_ref[1 if backward else 0, h]

    @pl.when(j == 0)
    def _():
        st_scr[...] = jnp.zeros_like(st_scr)

    q = q_ref[...]
    k = k_ref[...]
    v = v_ref[...]
    ti = lax.broadcasted_iota(jnp.int32, (C, C), 0)
    si = lax.broadcasted_iota(jnp.int32, (C, C), 1)
    dist = (si - ti) if backward else (ti - si)
    decay = jnp.where(dist >= 0, jnp.exp(lg * jnp.maximum(dist, 0).astype(F32)), 0.0)
    att = (_dot_nt(q, k) * decay).astype(BF16)
    tcol = lax.broadcasted_iota(jnp.int32, (C, 1), 0)
    eq = (C - tcol) if backward else (tcol + 1)
    ek = tcol if backward else (C - 1 - tcol)
    st = st_scr[...]
    o = _dot(att, v) + _dot_nt(q, st.astype(BF16)) * jnp.exp(lg * eq.astype(F32))
    kd = (k.astype(F32) * jnp.exp(lg * ek.astype(F32))).astype(BF16)
    st_scr[...] = st * jnp.exp(lg * jnp.full((1, 1), float(C), F32)) + _dot_tn(v, kd)
    if backward:
        tot = of_ref[...] + o
        y = tot * lax.rsqrt(jnp.mean(tot * tot, axis=-1, keepdims=True) + RMS_EPS)
        o_ref[...] = y.astype(o_ref.dtype)
    else:
        o_ref[...] = o


def _ret_scan(proj, lg, B, L, Lc):
    R = proj.shape[0]
    C = RET_CHUNK
    H, dk, dv = RET_HEADS, 256, 512
    outs = None
    for backward in (False, True):
        blk, nch = _chunk_order(B, L, Lc, C, backward)
        in_specs = [pl.BlockSpec(memory_space=pltpu.SMEM),
                    pl.BlockSpec((C, dk), lambda b, h, j: (blk(b, j), h)),
                    pl.BlockSpec((C, dk), lambda b, h, j: (blk(b, j), H + h)),
                    pl.BlockSpec((C, dv), lambda b, h, j: (blk(b, j), H + h))]
        args = [lg, proj, proj, proj]
        if backward:
            in_specs.append(pl.BlockSpec((C, dv), lambda b, h, j: (blk(b, j), h)))
            args.append(outs)
        outs = pl.pallas_call(
            functools.partial(_ret_scan_kernel, backward=backward, C=C),
            grid=(B, H, nch),
            in_specs=in_specs,
            out_specs=pl.BlockSpec((C, dv), lambda b, h, j: (blk(b, j), h)),
            out_shape=jax.ShapeDtypeStruct((R, H * dv), BF16 if backward else F32),
            scratch_shapes=[pltpu.VMEM((dv, dk), F32)],
            compiler_params=_cparams(("arbitrary", "arbitrary", "arbitrary")),
            name="ret_scan_bwd" if backward else "ret_scan_fwd",
        )(*args)
    return outs


def _cumsum_rows(x):
    n = x.shape[0]
    r = lax.broadcasted_iota(jnp.int32, (n, 1), 0)
    s = 1
    while s < n:
        x = x + jnp.where(r >= s, pltpu.roll(x, s, axis=0), 0.0)
        s *= 2
    return x


def _hg_scan_kernel(q_ref, f_ref, v_ref, lb_ref, *rest, backward, C, heads, scale):
    if backward:
        of_ref, ng_ref, o_ref, st_scr = rest
    else:
        o_ref, st_scr = rest
    j = pl.program_id(1)

    @pl.when(j == 0)
    def _():
        st_scr[...] = jnp.zeros_like(st_scr)

    lb = lb_ref[...]
    forget = lb + (1.0 - lb) * jax.nn.sigmoid(f_ref[...])
    kk = 1.0 - forget
    gl = jnp.log(forget)
    pre = _cumsum_rows(gl)
    tot = pre[C - 1:C, :]
    bc = (tot - pre + gl) if backward else pre
    qd = (_silu(q_ref[...]) * scale * jnp.exp(bc)).astype(BF16)
    kd = (kk * jnp.exp(-bc)).astype(BF16)
    ke = (kk * jnp.exp(tot - bc)).astype(BF16)
    v = v_ref[...].astype(BF16)
    etot = jnp.exp(tot)
    ti = lax.broadcasted_iota(jnp.int32, (C, C), 0)
    si = lax.broadcasted_iota(jnp.int32, (C, C), 1)
    keep = (si >= ti) if backward else (ti >= si)
    d = HG_EXPAND
    for h in range(heads):
        sl = slice(h * d, (h + 1) * d)
        att = jnp.where(keep, _dot_nt(qd[:, sl], kd[:, sl]), 0.0).astype(BF16)
        st = st_scr[h]
        o = _dot(att, v[:, sl]) + _dot_nt(qd[:, sl], st.astype(BF16))
        st_scr[h] = st * etot[:, sl] + _dot_tn(v[:, sl], ke[:, sl])
        if backward:
            t = of_ref[:, sl] + o
            y = t * lax.rsqrt(jnp.mean(t * t, axis=-1, keepdims=True) + RMS_EPS) * ng_ref[...]
            o_ref[:, sl] = y.astype(o_ref.dtype)
        else:
            o_ref[:, sl] = o


def _hg_scan(proj, lb, norm_g, B, L, Lc):
    R = proj.shape[0]
    C = HG_CHUNK
    Dm = proj.shape[1] // 5
    heads = Dm // HG_EXPAND
    outs = None
    for backward in (False, True):
        blk, nch = _chunk_order(B, L, Lc, C, backward)
        fcol = 2 if backward else 1
        in_specs = [pl.BlockSpec((C, Dm), lambda b, j: (blk(b, j), 0)),
                    pl.BlockSpec((C, Dm), lambda b, j: (blk(b, j), fcol)),
                    pl.BlockSpec((C, Dm), lambda b, j: (blk(b, j), 3)),
                    pl.BlockSpec((1, Dm), lambda b, j: (0, 0))]
        args = [proj, proj, proj, lb.reshape(1, Dm)]
        if backward:
            in_specs += [pl.BlockSpec((C, Dm), lambda b, j: (blk(b, j), 0)),
                         pl.BlockSpec((1, HG_EXPAND), lambda b, j: (0, 0))]
            args += [outs, norm_g.reshape(1, HG_EXPAND)]
        outs = pl.pallas_call(
            functools.partial(_hg_scan_kernel, backward=backward, C=C, heads=heads,
                              scale=HG_EXPAND ** -0.5),
            grid=(B, nch),
            in_specs=in_specs,
            out_specs=pl.BlockSpec((C, Dm), lambda b, j: (blk(b, j), 0)),
            out_shape=jax.ShapeDtypeStruct((R, Dm), BF16 if backward else F32),
            scratch_shapes=[pltpu.VMEM((heads, HG_EXPAND, HG_EXPAND), F32)],
            compiler_params=_cparams(("arbitrary", "arbitrary")),
            name="hg_scan_bwd" if backward else "hg_scan_fwd",
        )(*args)
    return outs


def _softmax_pv(scores, values):
    m = functools.reduce(jnp.maximum, [jnp.max(s, axis=-1, keepdims=True) for s in scores])
    ps = [jnp.exp(s - m) for s in scores]
    l = functools.reduce(lambda a, b: a + b, [jnp.sum(p, axis=-1, keepdims=True) for p in ps])
    o = functools.reduce(lambda a, b: a + b, [_dot(p.astype(BF16), v) for p, v in zip(ps, values)])
    return o / l


def _lane_lo(shape):
    return lax.broadcasted_iota(jnp.int32, shape, 1) < (LANES // 2)


NA_FRAME_ROWS = NA_WIN_ROWS + 2


def _na_kernel(q_ref, kw_ref, vtw_ref, kc_ref, vtc_ref, bias_ref, o_ref, *, heads, nrows):
    W, wr, fr = GRID_W, NA_WIN_ROWS, NA_FRAME_ROWS
    r0 = 2 * pl.program_id(1)
    u = jnp.minimum(jnp.clip(r0 - wr // 2, 0, nrows - wr), nrows - fr)
    tile = []
    for j in range(fr):
        per = []
        for rho in range(2):
            r = r0 + rho
            rs = jnp.clip(r - wr // 2, 0, nrows - wr)
            ok = (u + j >= rs) & (u + j < rs + wr)
            per.append(jnp.where(ok, u + j - r + wr - 1, 2 * wr - 1))
        tile.append(per)
    lo = _lane_lo((2 * W, LANES))
    q = q_ref[...] * 0.125
    ones_l = jnp.ones((8, fr * W), BF16)
    ones_c = jnp.ones((8, kc_ref.shape[0]), BF16)
    ri = lax.broadcasted_iota(jnp.int32, (LANES, 4 * W), 0)
    li = lax.broadcasted_iota(jnp.int32, (LANES, 4 * W), 1)
    own_head = (ri >= W) == ((li % LANES) >= W)
    for hp in range(heads // 2):
        sl = slice(hp * LANES, (hp + 1) * LANES)
        q2 = q[:, sl]
        qlo = jnp.where(lo, q2, jnp.zeros_like(q2))
        qhi = jnp.where(lo, jnp.zeros_like(q2), q2)
        qblk = jnp.concatenate([qlo[:W], qhi[:W], qlo[W:], qhi[W:]], axis=0)
        s_raw = _dot_nt(kw_ref[:, sl], qblk)
        s_l = jnp.concatenate(
            [s_raw[j * W:(j + 1) * W]
             + jnp.concatenate([bias_ref[hp, tile[j][0]], bias_ref[hp, tile[j][1]]], axis=1)
             for j in range(fr)], axis=0)
        s_c = _dot_nt(kc_ref[:, sl], qblk)
        m = jnp.maximum(jnp.max(s_l, axis=0, keepdims=True), jnp.max(s_c, axis=0, keepdims=True))
        p_l = jnp.exp(s_l - m).astype(BF16)
        p_c = jnp.exp(s_c - m).astype(BF16)
        acc = (_dot(jnp.concatenate([vtw_ref[sl, :], ones_l], axis=0), p_l)
               + _dot(jnp.concatenate([vtc_ref[sl, :], ones_c], axis=0), p_c))
        ot = jnp.where(own_head, acc[:LANES] / acc[LANES:LANES + 1], 0.0)
        tr = ot.T
        for rho in range(2):
            blk = tr[rho * LANES:rho * LANES + W] + tr[rho * LANES + W:(rho + 1) * LANES]
            o_ref[rho * W:(rho + 1) * W, sl] = blk.astype(o_ref.dtype)


def _na_bias_table(rpb):
    H = rpb.shape[0]
    W, wr, wc = GRID_W, NA_WIN_ROWS, NA_WIN_COLS
    qcol = np.arange(W)[:, None]
    kcol = np.arange(W)[None, :]
    ws = np.clip(qcol - wc // 2, 0, W - wc)
    ok = (kcol >= ws) & (kcol < ws + wc)
    r_pad = jnp.pad(rpb.astype(F32), ((0, 0), (0, 0), (W - wc, W + wc - (2 * wc - 1))))
    skew = jnp.tile(r_pad, (1, 1, W))[:, :, :W * (2 * W - 1)].reshape(H, 2 * wr - 1, W, 2 * W - 1)
    t15 = jnp.where(ok[None, None], skew[:, :, :, W - 1:], NEG_INF)
    t = t15.reshape(H // 2, 2, 2 * wr - 1, W, W).transpose(0, 2, 4, 1, 3).reshape(H // 2, 2 * wr - 1, W, 2 * W)
    return jnp.concatenate([t, jnp.full((H // 2, 1, W, 2 * W), NEG_INF, F32)], axis=1)


def _na_attention(qkv, vt, bias, B, L, Lc):
    D = qkv.shape[1] // 3
    W, wr, fr = GRID_W, NA_WIN_ROWS, NA_FRAME_ROWS
    nrows = L // W
    assert nrows % 2 == 0 and nrows >= fr
    frame0 = lambda g: jnp.minimum(jnp.clip(2 * g - wr // 2, 0, nrows - wr), nrows - fr)
    ctx_blk0 = B * L // Lc
    return pl.pallas_call(
        functools.partial(_na_kernel, heads=NA_HEADS, nrows=nrows),
        grid=(B, nrows // 2),
        in_specs=[pl.BlockSpec((2 * W, D), lambda b, g: (b * (nrows // 2) + g, 0)),
                  pl.BlockSpec((pl.Element(fr * W), pl.Element(D)),
                               lambda b, g: ((b * nrows + frame0(g)) * W, D)),
                  pl.BlockSpec((pl.Element(D), pl.Element(fr * W)),
                               lambda b, g: (0, pl.multiple_of((b * nrows + frame0(g)) * W, 2 * W))),
                  pl.BlockSpec((Lc, D), lambda b, g: (ctx_blk0 + b, 1)),
                  pl.BlockSpec((D, Lc), lambda b, g: (0, ctx_blk0 + b)),
                  _const_spec(bias.shape)],
        out_specs=pl.BlockSpec((2 * W, D), lambda b, g: (b * (nrows // 2) + g, 0)),
        out_shape=jax.ShapeDtypeStruct((B * L, D), BF16),
        compiler_params=_cparams(("arbitrary", "arbitrary")),
        name="na_attn",
    )(qkv, qkv, vt, qkv, vt, bias)


def _ctx_attn_kernel(q_ref, k_ref, v_ref, o_ref, *, heads, split64, q_scale):
    lo = _lane_lo((q_ref.shape[0], LANES))
    for hp in range(heads // 2):
        vsl = slice(hp * LANES, (hp + 1) * LANES)
        v2 = v_ref[:, vsl]
        outs = []
        for e in range(2):
            if split64:
                q2 = q_ref[:, vsl] * q_scale
                qh = jnp.where(lo if e == 0 else ~lo, q2, jnp.zeros_like(q2))
                kh = k_ref[:, vsl]
            else:
                hsl = slice((2 * hp + e) * LANES, (2 * hp + e + 1) * LANES)
                qh, kh = q_ref[:, hsl], k_ref[:, hsl]
            outs.append(_softmax_pv([_dot_nt(qh, kh)], [v2]))
        o_ref[:, vsl] = jnp.where(lo, outs[0], outs[1]).astype(o_ref.dtype)


def _ctx_attention(q_src, k_src, v_src, B, L, Lc, heads, split64, q_scale=1.0):
    blk0 = B * L // Lc
    spec = lambda src: pl.BlockSpec((Lc, src[1]), lambda b: (blk0 + b, src[2]))
    Dv = v_src[1]
    return pl.pallas_call(
        functools.partial(_ctx_attn_kernel, heads=heads, split64=split64, q_scale=q_scale),
        grid=(B,),
        in_specs=[spec(q_src), spec(k_src), spec(v_src)],
        out_specs=pl.BlockSpec((Lc, Dv), lambda b: (b, 0)),
        out_shape=jax.ShapeDtypeStruct((B * Lc, Dv), BF16),
        compiler_params=_cparams(("arbitrary",)),
        name="ctx_attn",
    )(q_src[0], k_src[0], v_src[0])


def _rope_slot(x, cos, sina, sinb):
    return x * cos + pltpu.roll(x, LANES - 16, axis=1) * sina + pltpu.roll(x, 16, axis=1) * sinb


def _mla_proj_kernel(x_ref, mod_ref, wd_ref, qn_ref, kvn_ref, wq_ref, wk_ref, wvt_ref, vone_ref,
                     cos_ref, sina_ref, sinb_ref, q_ref, k_ref, vt_ref, *, heads, scale):
    sh = mod_ref[0, 0:1, :]
    sc = mod_ref[0, 1:2, :]
    a = (x_ref[...] * (1.0 + sc) + sh).astype(BF16)
    d = _dot(a, wd_ref[...])
    cq = d[:, :MLA_Q_LORA]
    ckv = d[:, MLA_Q_LORA:MLA_Q_LORA + MLA_KV_LORA]
    kr = d[:, MLA_Q_LORA + MLA_KV_LORA:]
    rms = lambda t, g: (t * lax.rsqrt(jnp.mean(t * t, axis=-1, keepdims=True) + RMS_EPS) * g).astype(BF16)
    cqn = rms(cq, qn_ref[...])
    ckvn = rms(ckv, kvn_ref[...])
    cos, sina, sinb = cos_ref[...], sina_ref[...], sinb_ref[...]
    krr = _rope_slot(kr, cos, sina, sinb)
    qf = _dot(cqn, wq_ref[...])
    kf = _dot(ckvn, wk_ref[...])
    for h in range(heads):
        sl = slice(h * LANES, (h + 1) * LANES)
        q_ref[:, sl] = (_rope_slot(qf[:, sl], cos, sina, sinb) * scale).astype(q_ref.dtype)
        k_ref[:, sl] = (kf[:, sl] + krr).astype(k_ref.dtype)
    vt_ref[...] = (_dot_nt(wvt_ref[...], ckvn) + vone_ref[...]).astype(vt_ref.dtype)


def _mla_proj(rows, x, mod, wd, qn, kvn, wq, wk, wvt, vone, tables):
    R, D = x.shape
    tm = rows.tm
    H = MLA_HEADS
    tab_spec = pl.BlockSpec((tm, LANES), lambda i: (rows.pos_block(i), 0))
    return pl.pallas_call(
        functools.partial(_mla_proj_kernel, heads=H,
                          scale=(MLA_NOPE + MLA_ROPE) ** -0.5 * float(np.log2(np.e))),
        grid=(rows.n_all,),
        in_specs=[pl.BlockSpec((tm, D), lambda i: (i, 0)),
                  pl.BlockSpec((1, 6, D), lambda i: (rows.group(i), 0, 0)),
                  _const_spec(wd.shape), _const_spec((1, MLA_Q_LORA)), _const_spec((1, MLA_KV_LORA)),
                  _const_spec(wq.shape), _const_spec(wk.shape), _const_spec(wvt.shape),
                  _const_spec(vone.shape), tab_spec, tab_spec, tab_spec],
        out_specs=[pl.BlockSpec((tm, H * LANES), lambda i: (i, 0)),
                   pl.BlockSpec((tm, H * LANES), lambda i: (i, 0)),
                   pl.BlockSpec((H * LANES, tm), lambda i: (0, i))],
        out_shape=[jax.ShapeDtypeStruct((R, H * LANES), BF16),
                   jax.ShapeDtypeStruct((R, H * LANES), BF16),
                   jax.ShapeDtypeStruct((H * LANES, R), BF16)],
        compiler_params=_cparams(("arbitrary",)),
        name="mla_proj",
    )(x, mod, wd, qn.reshape(1, -1), kvn.reshape(1, -1), wq, wk, wvt, vone, *tables)


def _mla_flash_kernel(q_ref, kc_ref, vtc_ref, *rest, tk, with_latent):
    if with_latent:
        k_ref, vt_ref, o_ref, sa_ref, sb_ref = rest
    else:
        (o_ref,) = rest
    tq = q_ref.shape[0]
    hsl = [slice(e * LANES, (e + 1) * LANES) for e in range(2)]
    qs = [q_ref[:, sl] for sl in hsl]

    def scores(kblk):
        return [_dot_nt(kblk[:, hsl[e]], qs[e]) for e in range(2)]

    def update(ss, vtblk, state):
        out = []
        for e in range(2):
            m, acc = state[e]
            s = ss[e]
            m_new = jnp.maximum(m, jnp.max(s, axis=0, keepdims=True))
            p = jnp.exp2(s - m_new).astype(BF16)
            acc = jnp.exp2(m - m_new) * acc + _dot(vtblk[hsl[e], :], p)
            out.append((m_new, acc))
        return tuple(out)

    init = (jnp.full((1, tq), NEG_INF, F32), jnp.zeros((LANES, tq), F32))
    state = update(scores(kc_ref[...]), vtc_ref[...], (init, init))
    if with_latent:
        n = k_ref.shape[0] // tk
        assert n % 2 == 0

        def put(buf, c):
            ss = scores(k_ref[pl.ds(pl.multiple_of(c * tk, tk), tk), :])
            for e in range(2):
                buf[e] = ss[e]

        def take(buf, c, st):
            vtblk = vt_ref[:, pl.ds(pl.multiple_of(c * tk, tk), tk)]
            return update([buf[0], buf[1]], vtblk, st)

        put(sa_ref, 0)

        def body(i, st):
            put(sb_ref, 2 * i + 1)
            st = take(sa_ref, 2 * i, st)
            put(sa_ref, 2 * i + 2)
            return take(sb_ref, 2 * i + 1, st)

        state = lax.fori_loop(0, n // 2 - 1, body, state)
        put(sb_ref, n - 1)
        state = take(sa_ref, n - 2, state)
        state = take(sb_ref, n - 1, state)
    ot = jnp.concatenate([acc[:MLA_V] / acc[MLA_V:MLA_V + 1] for _, acc in state], axis=0)
    o_ref[...] = ot.T.astype(o_ref.dtype)


def _mla_attention(q, k, vt, B, L, Lc, tq, tk, latent_queries):
    H = MLA_HEADS
    ctx_blk0 = B * L // Lc
    if latent_queries:
        nq, q0, n_out = L // tq, 0, B * L
    else:
        assert tq == Lc
        nq, q0, n_out = 1, ctx_blk0, B * Lc
    in_specs = [pl.BlockSpec((tq, 2 * LANES), lambda b, hp, i: (q0 + b * nq + i, hp)),
                pl.BlockSpec((Lc, 2 * LANES), lambda b, hp, i: (ctx_blk0 + b, hp)),
                pl.BlockSpec((2 * LANES, Lc), lambda b, hp, i: (hp, ctx_blk0 + b))]
    args = [q, k, vt]
    scratch = []
    if latent_queries:
        in_specs += [pl.BlockSpec((L, 2 * LANES), lambda b, hp, i: (b, hp)),
                     pl.BlockSpec((2 * LANES, L), lambda b, hp, i: (hp, b))]
        args += [k, vt]
        scratch = [pltpu.VMEM((2, tk, tq), F32)] * 2
    return pl.pallas_call(
        functools.partial(_mla_flash_kernel, tk=tk, with_latent=latent_queries),
        grid=(B, H // 2, nq),
        in_specs=in_specs,
        out_specs=pl.BlockSpec((tq, LANES), lambda b, hp, i: (b * nq + i, hp)),
        out_shape=jax.ShapeDtypeStruct((n_out, H * MLA_V), BF16),
        scratch_shapes=scratch,
        compiler_params=_cparams(("arbitrary", "arbitrary", "arbitrary")),
        name="mla_flash" if latent_queries else "mla_flash_ctx",
    )(*args)


def _axial_angles(L, rot_dim):
    t = jnp.arange(L)
    rows = (t // GRID_W).astype(F32)
    cols = (t % GRID_W).astype(F32)
    n_freq = rot_dim // 4
    inv = ROPE_BASE ** (-jnp.arange(n_freq, dtype=F32) / n_freq)
    return jnp.concatenate([rows[:, None] * inv, cols[:, None] * inv], -1)


def _ret_rope_tables(L, tm):
    ang = _axial_angles(L, 256)
    cos = jnp.concatenate([jnp.cos(ang), jnp.ones((tm, LANES), F32)], 0)
    sin = jnp.concatenate([jnp.sin(ang), jnp.zeros((tm, LANES), F32)], 0)
    return cos, sin


def _mla_rope_tables(L, tm):
    ang = _axial_angles(L, MLA_ROPE)
    c, s = jnp.cos(ang), jnp.sin(ang)
    one = jnp.ones((L, MLA_NOPE), F32)
    z16 = jnp.zeros((L, 16), F32)
    z32 = jnp.zeros((L, 32), F32)
    z64 = jnp.zeros((L, MLA_NOPE), F32)
    cos = jnp.concatenate([one, c, c, jnp.ones((L, 32), F32)], -1)
    sina = jnp.concatenate([z64, -s, z16, z32], -1)
    sinb = jnp.concatenate([z64, z16, s, z32], -1)
    ident = lambda t, fill: jnp.concatenate([t, jnp.full((tm, LANES), fill, F32)], 0)
    return ident(cos, 1.0), ident(sina, 0.0), ident(sinb, 0.0)


def kernel(x, c, ctx, c_ctx, ada_w, ada_b, ln_g, ln_b, ffn_w13, ffn_w2, ret_w_in, ret_decay, ret_w_out, na_w_qkv, na_rpb, na_w_out, mla_w_down, mla_q_norm, mla_kv_norm, mla_w_uq, mla_w_ukv, mla_w_out, hg_w_in, hg_lower_bounds, hg_norm_g, hg_w_out):
    B, L, D = x.shape
    Lc = ctx.shape[1]
    depth = ada_w.shape[0]
    alpha = (2 * depth) ** 0.25
    tm = 512 if (B * Lc) % 512 == 0 else 256
    rows = _Rows(B, L, Lc, tm)
    n_lat_rows = B * L

    h = jnp.concatenate([x.reshape(B * L, D), ctx.reshape(B * Lc, D)], 0)

    G = 8 * (-(-(B + 1) // 8))
    cond_in = jnp.zeros((G, D), F32).at[0].set(c_ctx).at[1:B + 1].set(c)
    mods = _adaln(cond_in, ada_w, ada_b).reshape(depth, G, 6, D)

    bf = lambda w: w.astype(BF16)
    H = MLA_HEADS
    wd = jnp.zeros((D, MLA_Q_LORA + MLA_KV_LORA + LANES), F32)
    wd = wd.at[:, :MLA_Q_LORA + MLA_KV_LORA].set(mla_w_down[:, :MLA_Q_LORA + MLA_KV_LORA])
    wd = wd.at[:, MLA_Q_LORA + MLA_KV_LORA + MLA_NOPE:MLA_Q_LORA + MLA_KV_LORA + MLA_NOPE + MLA_ROPE].set(
        mla_w_down[:, MLA_Q_LORA + MLA_KV_LORA:])
    wq = jnp.pad(mla_w_uq.reshape(MLA_Q_LORA, H, MLA_NOPE + MLA_ROPE),
                 ((0, 0), (0, 0), (0, LANES - MLA_NOPE - MLA_ROPE))).reshape(MLA_Q_LORA, H * LANES)
    wukv = mla_w_ukv.reshape(MLA_KV_LORA, H, MLA_NOPE + MLA_V)
    wk = jnp.pad(wukv[:, :, :MLA_NOPE], ((0, 0), (0, 0), (0, LANES - MLA_NOPE))).reshape(MLA_KV_LORA, H * LANES)
    wvt = jnp.pad(wukv[:, :, MLA_NOPE:], ((0, 0), (0, 0), (0, LANES - MLA_V))).reshape(MLA_KV_LORA, H * LANES).T
    vone = jnp.tile((jnp.arange(LANES) == MLA_V).astype(F32), H).reshape(H * LANES, 1)

    for i in range(depth):
        mod = mods[i]
        kind = i % 4
        want_ctx = i < depth - 1
        if kind == 0:
            lg = -jnp.exp(ret_decay.astype(F32))
            proj = _modmm(rows, h, mod, bf(ret_w_in), BF16, 1024,
                          rope=_ret_rope_tables(L, tm), k_scale=256 ** -0.5)
            y = _ret_scan(proj, lg, B, L, Lc)
            h1 = _outproj_ln(rows, h, mod, y, bf(ret_w_out), ln_g[i, 0], ln_b[i, 0], alpha,
                             gate_src=(proj, 2))
        elif kind == 1:
            qkv, vt = _modmm(rows, h, mod, bf(na_w_qkv), BF16, 1024, vt_tile=2)
            o_lat = _na_attention(qkv, vt, _na_bias_table(na_rpb), B, L, Lc)
            o_ctx = _ctx_attention((qkv, D, 0), (qkv, D, 1), (qkv, D, 2), B, L, Lc, NA_HEADS,
                                   split64=True, q_scale=0.125)
            y = jnp.concatenate([o_lat, o_ctx], 0)
            h1 = _outproj_ln(rows, h, mod, y, bf(na_w_out), ln_g[i, 0], ln_b[i, 0], alpha)
        elif kind == 2:
            q, k, vt = _mla_proj(rows, h, mod, bf(wd), mla_q_norm, mla_kv_norm, bf(wq), bf(wk), bf(wvt), vone,
                                 _mla_rope_tables(L, tm))
            tk = 512 if L % 512 == 0 else L
            o_lat = _mla_attention(q, k, vt, B, L, Lc, 256, tk, latent_queries=True)
            o_ctx = _mla_attention(q, k, vt, B, L, Lc, Lc, tk, latent_queries=False)
            y = jnp.concatenate([o_lat, o_ctx], 0)
            h1 = _outproj_ln(rows, h, mod, y, bf(mla_w_out), ln_g[i, 0], ln_b[i, 0], alpha)
        else:
            lb_soft = jax.nn.softmax(hg_lower_bounds.astype(F32), axis=0)
            lb = (jnp.cumsum(lb_soft, axis=0) - lb_soft[0])[i]
            proj = _modmm(rows, h, mod, bf(hg_w_in), F32, 1024)
            y = _hg_scan(proj, lb, hg_norm_g, B, L, Lc)
            h1 = _outproj_ln(rows, h, mod, y, bf(hg_w_out), ln_g[i, 0], ln_b[i, 0], alpha,
                             gate_src=(proj, 4))
        n_tiles = rows.n_all if want_ctx else rows.n_lat
        h = _ffn(rows, h1, mod, bf(ffn_w13[i]), bf(ffn_w2[i]), ln_g[i, 1], ln_b[i, 1], alpha, n_tiles)
    return h[:n_lat_rows].reshape(B, L, D)
```

```python
import functools

import numpy as np
import jax
import jax.numpy as jnp
from jax import lax
from jax.experimental import pallas as pl
from jax.experimental.pallas import tpu as pltpu

F32 = jnp.float32
BF16 = jnp.bfloat16

GRID_W = 64
LN_EPS = 1e-5
RMS_EPS = 1e-6
ROPE_BASE = 10000.0
NEG_INF = -1e30

RET_HEADS = 4
NA_HEADS = 16
NA_WIN_ROWS = 8
NA_WIN_COLS = 16
MLA_HEADS = 16
MLA_NOPE = 64
MLA_ROPE = 32
MLA_V = 64
MLA_Q_LORA = 512
MLA_KV_LORA = 256
HG_EXPAND = 128
HG_CHUNK = 64
RET_CHUNK = 256

VMEM_LIMIT = 56 * 1024 * 1024
LANES = 128


def _cparams(sem, flags=None):
    return pltpu.CompilerParams(dimension_semantics=sem, vmem_limit_bytes=VMEM_LIMIT, flags=flags)


def _const_spec(shape):
    nd = len(shape)
    return pl.BlockSpec(shape, lambda *_: (0,) * nd, pipeline_mode=pl.Buffered(1))


def _dot(a, b):
    return jnp.dot(a, b, preferred_element_type=F32)


def _dot_nt(a, b):
    return lax.dot_general(a, b, (((1,), (1,)), ((), ())), preferred_element_type=F32)


def _dot_tn(a, b):
    return lax.dot_general(a, b, (((0,), (0,)), ((), ())), preferred_element_type=F32)


def _silu(x):
    return x * jax.nn.sigmoid(x)


def _layer_norm(r, g, b):
    mu = jnp.mean(r, axis=-1, keepdims=True)
    rc = r - mu
    var = jnp.mean(rc * rc, axis=-1, keepdims=True)
    return rc * lax.rsqrt(var + LN_EPS) * g + b


class _Rows:
    def __init__(self, B, L, Lc, tm):
        assert L % tm == 0 and (B * Lc) % tm == 0
        self.B, self.L, self.Lc, self.tm = B, L, Lc, tm
        self.n_lat = B * L // tm
        self.n_all = (B * L + B * Lc) // tm
        self.per_b = L // tm

    def group(self, i):
        return jnp.where(i < self.n_lat, 1 + i // self.per_b, 0)

    def pos_block(self, i):
        return jnp.where(i < self.n_lat, i % self.per_b, self.per_b)


def _adaln_kernel(c_ref, w_ref, b_ref, o_ref):
    cond = _silu(c_ref[...])
    o_ref[0] = jnp.dot(cond, w_ref[0], preferred_element_type=F32,
                       precision=lax.Precision.HIGHEST) + b_ref[0]


def _adaln(cond_in, ada_w, ada_b):
    depth, D, N = ada_w.shape
    G = cond_in.shape[0]
    tn = 1536
    return pl.pallas_call(
        _adaln_kernel,
        grid=(depth, N // tn),
        in_specs=[pl.BlockSpec((G, D), lambda l, j: (0, 0)),
                  pl.BlockSpec((1, D, tn), lambda l, j: (l, 0, j)),
                  pl.BlockSpec((1, 1, tn), lambda l, j: (l, 0, j))],
        out_specs=pl.BlockSpec((1, G, tn), lambda l, j: (l, 0, j)),
        out_shape=jax.ShapeDtypeStruct((depth, G, N), F32),
        compiler_params=_cparams(("arbitrary", "arbitrary")),
        name="adaln",
    )(cond_in, ada_w, ada_b.reshape(depth, 1, N))


def _modmm_kernel(x_ref, mod_ref, w_ref, *rest, rope_tiles, k_scale, vt_tile):
    vt_ref = None
    if rope_tiles:
        cos_ref, sin_ref, o_ref, a_scr = rest
    elif vt_tile is not None:
        o_ref, vt_ref, a_scr = rest
    else:
        o_ref, a_scr = rest
    j = pl.program_id(1)

    @pl.when(j == 0)
    def _():
        sh = mod_ref[0, 0:1, :]
        sc = mod_ref[0, 1:2, :]
        a_scr[...] = (x_ref[...] * (1.0 + sc) + sh).astype(BF16)

    acc = _dot(a_scr[...], w_ref[...])
    if vt_ref is not None:
        @pl.when(j == vt_tile)
        def _():
            vt_ref[...] = acc.T.astype(vt_ref.dtype)

    if not rope_tiles:
        o_ref[...] = acc.astype(o_ref.dtype)
        return

    @pl.when(j >= rope_tiles)
    def _():
        o_ref[...] = acc.astype(o_ref.dtype)

    @pl.when(j < rope_tiles)
    def _():
        cos = cos_ref[...]
        sin = sin_ref[...]
        scale = jnp.where(j == 1, k_scale, 1.0).astype(F32)
        tn = acc.shape[1]
        for h in range(tn // (2 * LANES)):
            x1 = acc[:, h * 256:h * 256 + LANES]
            x2 = acc[:, h * 256 + LANES:(h + 1) * 256]
            o_ref[:, h * 256:h * 256 + LANES] = ((x1 * cos - x2 * sin) * scale).astype(o_ref.dtype)
            o_ref[:, h * 256 + LANES:(h + 1) * 256] = ((x1 * sin + x2 * cos) * scale).astype(o_ref.dtype)


def _modmm(rows, x, mod, w, out_dtype, tn, rope=None, k_scale=1.0, vt_tile=None):
    R, D = x.shape
    N = w.shape[1]
    tm = rows.tm
    out_specs = pl.BlockSpec((tm, tn), lambda i, j: (i, j))
    out_shape = jax.ShapeDtypeStruct((R, N), out_dtype)
    if vt_tile is not None:
        out_specs = [out_specs, pl.BlockSpec((tn, tm), lambda i, j: (0, i))]
        out_shape = [out_shape, jax.ShapeDtypeStruct((tn, R), out_dtype)]
    in_specs = [pl.BlockSpec((tm, D), lambda i, j: (i, 0)),
                pl.BlockSpec((1, 6, D), lambda i, j: (rows.group(i), 0, 0)),
                pl.BlockSpec((D, tn), lambda i, j: (0, j))]
    args = [x, mod, w]
    if rope is not None:
        in_specs += [pl.BlockSpec((tm, LANES), lambda i, j: (rows.pos_block(i), 0))] * 2
        args += list(rope)
    return pl.pallas_call(
        functools.partial(_modmm_kernel, rope_tiles=2 if rope is not None else 0, k_scale=k_scale,
                          vt_tile=vt_tile),
        grid=(rows.n_all, N // tn),
        in_specs=in_specs,
        out_specs=out_specs,
        out_shape=out_shape,
        scratch_shapes=[pltpu.VMEM((tm, D), BF16)],
        compiler_params=_cparams(("arbitrary", "arbitrary")),
        name="modmm",
    )(*args)


def _outproj_kernel(x_ref, mod_ref, y_ref, *rest, gated, alpha):
    if gated:
        g_ref, w_ref, lng_ref, lnb_ref, o_ref = rest
        g = g_ref[...].astype(F32)
        a = (_silu(g) * y_ref[...].astype(F32)).astype(BF16)
    else:
        w_ref, lng_ref, lnb_ref, o_ref = rest
        a = y_ref[...]
    acc = _dot(a, w_ref[...])
    gate = mod_ref[0, 2:3, :]
    r = alpha * x_ref[...] + gate * acc
    o_ref[...] = _layer_norm(r, lng_ref[...], lnb_ref[...])


def _outproj_ln(rows, x, mod, y, w, ln_g, ln_b, alpha, gate_src=None):
    R, D = x.shape
    K = w.shape[0]
    tm = rows.tm
    in_specs = [pl.BlockSpec((tm, D), lambda i: (i, 0)),
                pl.BlockSpec((1, 6, D), lambda i: (rows.group(i), 0, 0)),
                pl.BlockSpec((tm, K), lambda i: (i, 0))]
    args = [x, mod, y]
    if gate_src is not None:
        g_arr, g_blk = gate_src
        in_specs.append(pl.BlockSpec((tm, K), lambda i: (i, g_blk)))
        args.append(g_arr)
    in_specs += [_const_spec((K, D)), _const_spec((1, D)), _const_spec((1, D))]
    args += [w, ln_g.reshape(1, D), ln_b.reshape(1, D)]
    return pl.pallas_call(
        functools.partial(_outproj_kernel, gated=gate_src is not None, alpha=alpha),
        grid=(rows.n_all,),
        in_specs=in_specs,
        out_specs=pl.BlockSpec((tm, D), lambda i: (i, 0)),
        out_shape=jax.ShapeDtypeStruct((R, D), F32),
        compiler_params=_cparams(("arbitrary",)),
        name="outproj_ln",
    )(*args)


def _ffn_kernel(x_ref, mod_ref, w13_ref, w2_ref, lng_ref, lnb_ref, o_ref, *, alpha, fc):
    x = x_ref[...]
    sh = mod_ref[0, 3:4, :]
    sc = mod_ref[0, 4:5, :]
    gate = mod_ref[0, 5:6, :]
    a = (x * (1.0 + sc) + sh).astype(BF16)
    F = w2_ref.shape[0]
    acc = jnp.zeros(x.shape, F32)
    for c in range(F // fc):
        g = _dot(a, w13_ref[:, c * fc:(c + 1) * fc])
        u = _dot(a, w13_ref[:, F + c * fc:F + (c + 1) * fc])
        act = (_silu(g) * u).astype(BF16)
        acc = acc + _dot(act, w2_ref[c * fc:(c + 1) * fc, :])
    r = alpha * x + gate * acc
    o_ref[...] = _layer_norm(r, lng_ref[...], lnb_ref[...])


def _ffn(rows, x, mod, w13, w2, ln_g, ln_b, alpha, n_tiles):
    R, D = x.shape
    F = w2.shape[0]
    tm = rows.tm
    return pl.pallas_call(
        functools.partial(_ffn_kernel, alpha=alpha, fc=256),
        grid=(n_tiles,),
        in_specs=[pl.BlockSpec((tm, D), lambda i: (i, 0)),
                  pl.BlockSpec((1, 6, D), lambda i: (rows.group(i), 0, 0)),
                  _const_spec((D, 2 * F)), _const_spec((F, D)),
                  _const_spec((1, D)), _const_spec((1, D))],
        out_specs=pl.BlockSpec((tm, D), lambda i: (i, 0)),
        out_shape=jax.ShapeDtypeStruct((n_tiles * tm, D), F32),
        compiler_params=_cparams(("arbitrary",)),
        name="ffn",
    )(x, mod, w13, w2, ln_g.reshape(1, D), ln_b.reshape(1, D))


def _chunk_order(B, L, Lc, C, backward):
    ncc, ncl = Lc // C, L // C

    def blk(b, j):
        if backward:
            in_ctx = j < ncc
            return jnp.where(in_ctx, B * ncl + b * ncc + (ncc - 1 - j), b * ncl + (ncl - 1 - (j - ncc)))
        return jnp.where(j < ncc, B * ncl + b * ncc + j, b * ncl + (j - ncc))

    return blk, ncc + ncl


def _ret_scan_kernel(lg_ref, q_ref, k_ref, v_ref, *rest, backward, C):
    if backward:
        of_ref, o_ref, st_scr = rest
    else:
        o_ref, st_scr = rest
    h = pl.program_id(1)
    j = pl.program_id(2)
    lg = lg_ref[1 if backward else 0, h]

    @pl.when(j == 0)
    def _():
        st_scr[...] = jnp.zeros_like(st_scr)

    q = q_ref[...]
    k = k_ref[...]
    v = v_ref[...]
    ti = lax.broadcasted_iota(jnp.int32, (C, C), 0)
    si = lax.broadcasted_iota(jnp.int32, (C, C), 1)
    dist = (si - ti) if backward else (ti - si)
    decay = jnp.where(dist >= 0, jnp.exp(lg * jnp.maximum(dist, 0).astype(F32)), 0.0)
    att = (_dot_nt(q, k) * decay).astype(BF16)
    tcol = lax.broadcasted_iota(jnp.int32, (C, 1), 0)
    eq = (C - tcol) if backward else (tcol + 1)
    ek = tcol if backward else (C - 1 - tcol)
    st = st_scr[...]
    o = _dot(att, v) + _dot_nt(q, st.astype(BF16)) * jnp.exp(lg * eq.astype(F32))
    kd = (k.astype(F32) * jnp.exp(lg * ek.astype(F32))).astype(BF16)
    st_scr[...] = st * jnp.exp(lg * jnp.full((1, 1), float(C), F32)) + _dot_tn(v, kd)
    if backward:
        tot = of_ref[...] + o
        y = tot * lax.rsqrt(jnp.mean(tot * tot, axis=-1, keepdims=True) + RMS_EPS)
        o_ref[...] = y.astype(o_ref.dtype)
    else:
        o_ref[...] = o


def _ret_scan(proj, lg, B, L, Lc):
    R = proj.shape[0]
    C = RET_CHUNK
    H, dk, dv = RET_HEADS, 256, 512
    outs = None
    for backward in (False, True):
        blk, nch = _chunk_order(B, L, Lc, C, backward)
        in_specs = [pl.BlockSpec(memory_space=pltpu.SMEM),
                    pl.BlockSpec((C, dk), lambda b, h, j: (blk(b, j), h)),
                    pl.BlockSpec((C, dk), lambda b, h, j: (blk(b, j), H + h)),
                    pl.BlockSpec((C, dv), lambda b, h, j: (blk(b, j), H + h))]
        args = [lg, proj, proj, proj]
        if backward:
            in_specs.append(pl.BlockSpec((C, dv), lambda b, h, j: (blk(b, j), h)))
            args.append(outs)
        outs = pl.pallas_call(
            functools.partial(_ret_scan_kernel, backward=backward, C=C),
            grid=(B, H, nch),
            in_specs=in_specs,
            out_specs=pl.BlockSpec((C, dv), lambda b, h, j: (blk(b, j), h)),
            out_shape=jax.ShapeDtypeStruct((R, H * dv), BF16 if backward else F32),
            scratch_shapes=[pltpu.VMEM((dv, dk), F32)],
            compiler_params=_cparams(("arbitrary", "arbitrary", "arbitrary")),
            name="ret_scan_bwd" if backward else "ret_scan_fwd",
        )(*args)
    return outs


def _cumsum_rows(x):
    n = x.shape[0]
    r = lax.broadcasted_iota(jnp.int32, (n, 1), 0)
    s = 1
    while s < n:
        x = x + jnp.where(r >= s, pltpu.roll(x, s, axis=0), 0.0)
        s *= 2
    return x


def _hg_scan_kernel(q_ref, f_ref, v_ref, lb_ref, *rest, backward, C, heads, scale):
    if backward:
        of_ref, ng_ref, o_ref, st_scr = rest
    else:
        o_ref, st_scr = rest
    j = pl.program_id(1)

    @pl.when(j == 0)
    def _():
        st_scr[...] = jnp.zeros_like(st_scr)

    lb = lb_ref[...]
    forget = lb + (1.0 - lb) * jax.nn.sigmoid(f_ref[...])
    kk = 1.0 - forget
    gl = jnp.log(forget)
    pre = _cumsum_rows(gl)
    tot = pre[C - 1:C, :]
    bc = (tot - pre + gl) if backward else pre
    qd = (_silu(q_ref[...]) * scale * jnp.exp(bc)).astype(BF16)
    kd = (kk * jnp.exp(-bc)).astype(BF16)
    ke = (kk * jnp.exp(tot - bc)).astype(BF16)
    v = v_ref[...].astype(BF16)
    etot = jnp.exp(tot)
    ti = lax.broadcasted_iota(jnp.int32, (C, C), 0)
    si = lax.broadcasted_iota(jnp.int32, (C, C), 1)
    keep = (si >= ti) if backward else (ti >= si)
    d = HG_EXPAND
    for h in range(heads):
        sl = slice(h * d, (h + 1) * d)
        att = jnp.where(keep, _dot_nt(qd[:, sl], kd[:, sl]), 0.0).astype(BF16)
        st = st_scr[h]
        o = _dot(att, v[:, sl]) + _dot_nt(qd[:, sl], st.astype(BF16))
        st_scr[h] = st * etot[:, sl] + _dot_tn(v[:, sl], ke[:, sl])
        if backward:
            t = of_ref[:, sl] + o
            y = t * lax.rsqrt(jnp.mean(t * t, axis=-1, keepdims=True) + RMS_EPS) * ng_ref[...]
            o_ref[:, sl] = y.astype(o_ref.dtype)
        else:
            o_ref[:, sl] = o


def _hg_scan(proj, lb, norm_g, B, L, Lc):
    R = proj.shape[0]
    C = HG_CHUNK
    Dm = proj.shape[1] // 5
    heads = Dm // HG_EXPAND
    outs = None
    for backward in (False, True):
        blk, nch = _chunk_order(B, L, Lc, C, backward)
        fcol = 2 if backward else 1
        in_specs = [pl.BlockSpec((C, Dm), lambda b, j: (blk(b, j), 0)),
                    pl.BlockSpec((C, Dm), lambda b, j: (blk(b, j), fcol)),
                    pl.BlockSpec((C, Dm), lambda b, j: (blk(b, j), 3)),
                    pl.BlockSpec((1, Dm), lambda b, j: (0, 0))]
        args = [proj, proj, proj, lb.reshape(1, Dm)]
        if backward:
            in_specs += [pl.BlockSpec((C, Dm), lambda b, j: (blk(b, j), 0)),
                         pl.BlockSpec((1, HG_EXPAND), lambda b, j: (0, 0))]
            args += [outs, norm_g.reshape(1, HG_EXPAND)]
        outs = pl.pallas_call(
            functools.partial(_hg_scan_kernel, backward=backward, C=C, heads=heads,
                              scale=HG_EXPAND ** -0.5),
            grid=(B, nch),
            in_specs=in_specs,
            out_specs=pl.BlockSpec((C, Dm), lambda b, j: (blk(b, j), 0)),
            out_shape=jax.ShapeDtypeStruct((R, Dm), BF16 if backward else F32),
            scratch_shapes=[pltpu.VMEM((heads, HG_EXPAND, HG_EXPAND), F32)],
            compiler_params=_cparams(("arbitrary", "arbitrary")),
            name="hg_scan_bwd" if backward else "hg_scan_fwd",
        )(*args)
    return outs


def _softmax_pv(scores, values):
    m = functools.reduce(jnp.maximum, [jnp.max(s, axis=-1, keepdims=True) for s in scores])
    ps = [jnp.exp(s - m) for s in scores]
    l = functools.reduce(lambda a, b: a + b, [jnp.sum(p, axis=-1, keepdims=True) for p in ps])
    o = functools.reduce(lambda a, b: a + b, [_dot(p.astype(BF16), v) for p, v in zip(ps, values)])
    return o / l


def _lane_lo(shape):
    return lax.broadcasted_iota(jnp.int32, shape, 1) < (LANES // 2)


NA_FRAME_ROWS = NA_WIN_ROWS + 2


def _na_kernel(q_ref, kw_ref, vtw_ref, kc_ref, vtc_ref, bias_ref, o_ref, *, heads, nrows):
    W, wr, fr = GRID_W, NA_WIN_ROWS, NA_FRAME_ROWS
    r0 = 2 * pl.program_id(1)
    u = jnp.minimum(jnp.clip(r0 - wr // 2, 0, nrows - wr), nrows - fr)
    tile = []
    for j in range(fr):
        per = []
        for rho in range(2):
            r = r0 + rho
            rs = jnp.clip(r - wr // 2, 0, nrows - wr)
            ok = (u + j >= rs) & (u + j < rs + wr)
            per.append(jnp.where(ok, u + j - r + wr - 1, 2 * wr - 1))
        tile.append(per)
    lo = _lane_lo((2 * W, LANES))
    q = q_ref[...] * 0.125
    ones_l = jnp.ones((8, fr * W), BF16)
    ones_c = jnp.ones((8, kc_ref.shape[0]), BF16)
    ri = lax.broadcasted_iota(jnp.int32, (LANES, 4 * W), 0)
    li = lax.broadcasted_iota(jnp.int32, (LANES, 4 * W), 1)
    own_head = (ri >= W) == ((li % LANES) >= W)
    for hp in range(heads // 2):
        sl = slice(hp * LANES, (hp + 1) * LANES)
        q2 = q[:, sl]
        qlo = jnp.where(lo, q2, jnp.zeros_like(q2))
        qhi = jnp.where(lo, jnp.zeros_like(q2), q2)
        qblk = jnp.concatenate([qlo[:W], qhi[:W], qlo[W:], qhi[W:]], axis=0)
        s_raw = _dot_nt(kw_ref[:, sl], qblk)
        s_l = jnp.concatenate(
            [s_raw[j * W:(j + 1) * W]
             + jnp.concatenate([bias_ref[hp, tile[j][0]], bias_ref[hp, tile[j][1]]], axis=1)
             for j in range(fr)], axis=0)
        s_c = _dot_nt(kc_ref[:, sl], qblk)
        m = jnp.maximum(jnp.max(s_l, axis=0, keepdims=True), jnp.max(s_c, axis=0, keepdims=True))
        p_l = jnp.exp(s_l - m).astype(BF16)
        p_c = jnp.exp(s_c - m).astype(BF16)
        acc = (_dot(jnp.concatenate([vtw_ref[sl, :], ones_l], axis=0), p_l)
               + _dot(jnp.concatenate([vtc_ref[sl, :], ones_c], axis=0), p_c))
        ot = jnp.where(own_head, acc[:LANES] / acc[LANES:LANES + 1], 0.0)
        tr = ot.T
        for rho in range(2):
            blk = tr[rho * LANES:rho * LANES + W] + tr[rho * LANES + W:(rho + 1) * LANES]
            o_ref[rho * W:(rho + 1) * W, sl] = blk.astype(o_ref.dtype)


def _na_bias_table(rpb):
    H = rpb.shape[0]
    W, wr, wc = GRID_W, NA_WIN_ROWS, NA_WIN_COLS
    qcol = np.arange(W)[:, None]
    kcol = np.arange(W)[None, :]
    ws = np.clip(qcol - wc // 2, 0, W - wc)
    ok = (kcol >= ws) & (kcol < ws + wc)
    r_pad = jnp.pad(rpb.astype(F32), ((0, 0), (0, 0), (W - wc, W + wc - (2 * wc - 1))))
    skew = jnp.tile(r_pad, (1, 1, W))[:, :, :W * (2 * W - 1)].reshape(H, 2 * wr - 1, W, 2 * W - 1)
    t15 = jnp.where(ok[None, None], skew[:, :, :, W - 1:], NEG_INF)
    t = t15.reshape(H // 2, 2, 2 * wr - 1, W, W).transpose(0, 2, 4, 1, 3).reshape(H // 2, 2 * wr - 1, W, 2 * W)
    return jnp.concatenate([t, jnp.full((H // 2, 1, W, 2 * W), NEG_INF, F32)], axis=1)


def _na_attention(qkv, vt, bias, B, L, Lc):
    D = qkv.shape[1] // 3
    W, wr, fr = GRID_W, NA_WIN_ROWS, NA_FRAME_ROWS
    nrows = L // W
    assert nrows % 2 == 0 and nrows >= fr
    frame0 = lambda g: jnp.minimum(jnp.clip(2 * g - wr // 2, 0, nrows - wr), nrows - fr)
    ctx_blk0 = B * L // Lc
    return pl.pallas_call(
        functools.partial(_na_kernel, heads=NA_HEADS, nrows=nrows),
        grid=(B, nrows // 2),
        in_specs=[pl.BlockSpec((2 * W, D), lambda b, g: (b * (nrows // 2) + g, 0)),
                  pl.BlockSpec((pl.Element(fr * W), pl.Element(D)),
                               lambda b, g: ((b * nrows + frame0(g)) * W, D)),
                  pl.BlockSpec((pl.Element(D), pl.Element(fr * W)),
                               lambda b, g: (0, pl.multiple_of((b * nrows + frame0(g)) * W, 2 * W))),
                  pl.BlockSpec((Lc, D), lambda b, g: (ctx_blk0 + b, 1)),
                  pl.BlockSpec((D, Lc), lambda b, g: (0, ctx_blk0 + b)),
                  _const_spec(bias.shape)],
        out_specs=pl.BlockSpec((2 * W, D), lambda b, g: (b * (nrows // 2) + g, 0)),
        out_shape=jax.ShapeDtypeStruct((B * L, D), BF16),
        compiler_params=_cparams(("arbitrary", "arbitrary")),
        name="na_attn",
    )(qkv, qkv, vt, qkv, vt, bias)


def _ctx_attn_kernel(q_ref, k_ref, v_ref, o_ref, *, heads, split64, q_scale):
    lo = _lane_lo((q_ref.shape[0], LANES))
    for hp in range(heads // 2):
        vsl = slice(hp * LANES, (hp + 1) * LANES)
        v2 = v_ref[:, vsl]
        outs = []
        for e in range(2):
            if split64:
                q2 = q_ref[:, vsl] * q_scale
                qh = jnp.where(lo if e == 0 else ~lo, q2, jnp.zeros_like(q2))
                kh = k_ref[:, vsl]
            else:
                hsl = slice((2 * hp + e) * LANES, (2 * hp + e + 1) * LANES)
                qh, kh = q_ref[:, hsl], k_ref[:, hsl]
            outs.append(_softmax_pv([_dot_nt(qh, kh)], [v2]))
        o_ref[:, vsl] = jnp.where(lo, outs[0], outs[1]).astype(o_ref.dtype)


def _ctx_attention(q_src, k_src, v_src, B, L, Lc, heads, split64, q_scale=1.0):
    blk0 = B * L // Lc
    spec = lambda src: pl.BlockSpec((Lc, src[1]), lambda b: (blk0 + b, src[2]))
    Dv = v_src[1]
    return pl.pallas_call(
        functools.partial(_ctx_attn_kernel, heads=heads, split64=split64, q_scale=q_scale),
        grid=(B,),
        in_specs=[spec(q_src), spec(k_src), spec(v_src)],
        out_specs=pl.BlockSpec((Lc, Dv), lambda b: (b, 0)),
        out_shape=jax.ShapeDtypeStruct((B * Lc, Dv), BF16),
        compiler_params=_cparams(("arbitrary",)),
        name="ctx_attn",
    )(q_src[0], k_src[0], v_src[0])


def _rope_slot(x, cos, sina, sinb):
    return x * cos + pltpu.roll(x, LANES - 16, axis=1) * sina + pltpu.roll(x, 16, axis=1) * sinb


def _mla_proj_kernel(x_ref, mod_ref, wd_ref, qn_ref, kvn_ref, wq_ref, wk_ref, wvt_ref, vone_ref,
                     cos_ref, sina_ref, sinb_ref, q_ref, k_ref, vt_ref, *, heads, scale):
    sh = mod_ref[0, 0:1, :]
    sc = mod_ref[0, 1:2, :]
    a = (x_ref[...] * (1.0 + sc) + sh).astype(BF16)
    d = _dot(a, wd_ref[...])
    cq = d[:, :MLA_Q_LORA]
    ckv = d[:, MLA_Q_LORA:MLA_Q_LORA + MLA_KV_LORA]
    kr = d[:, MLA_Q_LORA + MLA_KV_LORA:]
    rms = lambda t, g: (t * lax.rsqrt(jnp.mean(t * t, axis=-1, keepdims=True) + RMS_EPS) * g).astype(BF16)
    cqn = rms(cq, qn_ref[...])
    ckvn = rms(ckv, kvn_ref[...])
    cos, sina, sinb = cos_ref[...], sina_ref[...], sinb_ref[...]
    krr = _rope_slot(kr, cos, sina, sinb)
    qf = _dot(cqn, wq_ref[...])
    kf = _dot(ckvn, wk_ref[...])
    for h in range(heads):
        sl = slice(h * LANES, (h + 1) * LANES)
        q_ref[:, sl] = (_rope_slot(qf[:, sl], cos, sina, sinb) * scale).astype(q_ref.dtype)
        k_ref[:, sl] = (kf[:, sl] + krr).astype(k_ref.dtype)
    vt_ref[...] = (_dot_nt(wvt_ref[...], ckvn) + vone_ref[...]).astype(vt_ref.dtype)


def _mla_proj(rows, x, mod, wd, qn, kvn, wq, wk, wvt, vone, tables):
    R, D = x.shape
    tm = rows.tm
    H = MLA_HEADS
    tab_spec = pl.BlockSpec((tm, LANES), lambda i: (rows.pos_block(i), 0))
    return pl.pallas_call(
        functools.partial(_mla_proj_kernel, heads=H,
                          scale=(MLA_NOPE + MLA_ROPE) ** -0.5 * float(np.log2(np.e))),
        grid=(rows.n_all,),
        in_specs=[pl.BlockSpec((tm, D), lambda i: (i, 0)),
                  pl.BlockSpec((1, 6, D), lambda i: (rows.group(i), 0, 0)),
                  _const_spec(wd.shape), _const_spec((1, MLA_Q_LORA)), _const_spec((1, MLA_KV_LORA)),
                  _const_spec(wq.shape), _const_spec(wk.shape), _const_spec(wvt.shape),
                  _const_spec(vone.shape), tab_spec, tab_spec, tab_spec],
        out_specs=[pl.BlockSpec((tm, H * LANES), lambda i: (i, 0)),
                   pl.BlockSpec((tm, H * LANES), lambda i: (i, 0)),
                   pl.BlockSpec((H * LANES, tm), lambda i: (0, i))],
        out_shape=[jax.ShapeDtypeStruct((R, H * LANES), BF16),
                   jax.ShapeDtypeStruct((R, H * LANES), BF16),
                   jax.ShapeDtypeStruct((H * LANES, R), BF16)],
        compiler_params=_cparams(("arbitrary",)),
        name="mla_proj",
    )(x, mod, wd, qn.reshape(1, -1), kvn.reshape(1, -1), wq, wk, wvt, vone, *tables)


def _mla_flash_kernel(q_ref, kc_ref, vtc_ref, *rest, tk, cpi, with_latent):
    if with_latent:
        k_ref, vt_ref, o_ref, sa_ref, sb_ref = rest
    else:
        (o_ref,) = rest
    tq = q_ref.shape[0]
    hsl = [slice(e * LANES, (e + 1) * LANES) for e in range(2)]
    qs = [q_ref[:, sl] for sl in hsl]

    def scores(kblk):
        return [_dot_nt(kblk[:, hsl[e]], qs[e]) for e in range(2)]

    vrows = MLA_V + 8

    def update(ss, vtblk, state):
        out = []
        for e in range(2):
            m, acc = state[e]
            s = ss[e]
            m_new = jnp.maximum(m, jnp.max(s, axis=0, keepdims=True))
            p = jnp.exp2(s - m_new).astype(BF16)
            acc = jnp.exp2(m - m_new) * acc + _dot(vtblk[e * LANES:e * LANES + vrows, :], p)
            out.append((m_new, acc))
        return tuple(out)

    init = (jnp.full((1, tq), NEG_INF, F32), jnp.zeros((vrows, tq), F32))
    if not with_latent:
        state = update(scores(kc_ref[...]), vtc_ref[...], (init, init))
    else:
        n = k_ref.shape[0] // tk
        assert cpi % 2 == 0 and n % cpi == 0
        bufs = (sa_ref, sb_ref)

        def put(buf, c):
            ss = scores(k_ref[pl.ds(pl.multiple_of(c * tk, tk), tk), :])
            for e in range(2):
                buf[e] = ss[e]

        def group(c0, st, last):
            for t in range(cpi):
                if not (last and t == cpi - 1):
                    put(bufs[(t + 1) % 2], c0 + t + 1)
                buf = bufs[t % 2]
                vtblk = vt_ref[:, pl.ds(pl.multiple_of((c0 + t) * tk, tk), tk)]
                st = update([buf[0], buf[1]], vtblk, st)
            return st

        put(bufs[0], 0)
        state = update(scores(kc_ref[...]), vtc_ref[...], (init, init))
        state = lax.fori_loop(0, n // cpi - 1, lambda i, st: group(i * cpi, st, False), state)
        state = group(n - cpi, state, True)
    ot = jnp.concatenate([acc[:MLA_V] / acc[MLA_V:MLA_V + 1] for _, acc in state], axis=0)
    o_ref[...] = ot.T.astype(o_ref.dtype)


def _mla_attention(q, k, vt, B, L, Lc, tq, tk, latent_queries):
    H = MLA_HEADS
    ctx_blk0 = B * L // Lc
    if latent_queries:
        nq, q0, n_out = L // tq, 0, B * L
    else:
        assert tq == Lc
        nq, q0, n_out = 1, ctx_blk0, B * Lc
    in_specs = [pl.BlockSpec((tq, 2 * LANES), lambda b, hp, i: (q0 + b * nq + i, hp)),
                pl.BlockSpec((Lc, 2 * LANES), lambda b, hp, i: (ctx_blk0 + b, hp)),
                pl.BlockSpec((2 * LANES, Lc), lambda b, hp, i: (hp, ctx_blk0 + b))]
    args = [q, k, vt]
    scratch = []
    if latent_queries:
        in_specs += [pl.BlockSpec((L, 2 * LANES), lambda b, hp, i: (b, hp)),
                     pl.BlockSpec((2 * LANES, L), lambda b, hp, i: (hp, b))]
        args += [k, vt]
        scratch = [pltpu.VMEM((2, tk, tq), F32)] * 2
    return pl.pallas_call(
        functools.partial(_mla_flash_kernel, tk=tk, cpi=4 if (L // tk) % 4 == 0 else 2,
                          with_latent=latent_queries),
        grid=(B, H // 2, nq),
        in_specs=in_specs,
        out_specs=pl.BlockSpec((tq, LANES), lambda b, hp, i: (b * nq + i, hp)),
        out_shape=jax.ShapeDtypeStruct((n_out, H * MLA_V), BF16),
        scratch_shapes=scratch,
        compiler_params=_cparams(("arbitrary", "arbitrary", "arbitrary")),
        name="mla_flash" if latent_queries else "mla_flash_ctx",
    )(*args)


def _axial_angles(L, rot_dim):
    t = jnp.arange(L)
    rows = (t // GRID_W).astype(F32)
    cols = (t % GRID_W).astype(F32)
    n_freq = rot_dim // 4
    inv = ROPE_BASE ** (-jnp.arange(n_freq, dtype=F32) / n_freq)
    return jnp.concatenate([rows[:, None] * inv, cols[:, None] * inv], -1)


def _ret_rope_tables(L, tm):
    ang = _axial_angles(L, 256)
    cos = jnp.concatenate([jnp.cos(ang), jnp.ones((tm, LANES), F32)], 0)
    sin = jnp.concatenate([jnp.sin(ang), jnp.zeros((tm, LANES), F32)], 0)
    return cos, sin


def _mla_rope_tables(L, tm):
    ang = _axial_angles(L, MLA_ROPE)
    c, s = jnp.cos(ang), jnp.sin(ang)
    one = jnp.ones((L, MLA_NOPE), F32)
    z16 = jnp.zeros((L, 16), F32)
    z32 = jnp.zeros((L, 32), F32)
    z64 = jnp.zeros((L, MLA_NOPE), F32)
    cos = jnp.concatenate([one, c, c, jnp.ones((L, 32), F32)], -1)
    sina = jnp.concatenate([z64, -s, z16, z32], -1)
    sinb = jnp.concatenate([z64, z16, s, z32], -1)
    ident = lambda t, fill: jnp.concatenate([t, jnp.full((tm, LANES), fill, F32)], 0)
    return ident(cos, 1.0), ident(sina, 0.0), ident(sinb, 0.0)


def kernel(x, c, ctx, c_ctx, ada_w, ada_b, ln_g, ln_b, ffn_w13, ffn_w2, ret_w_in, ret_decay, ret_w_out, na_w_qkv, na_rpb, na_w_out, mla_w_down, mla_q_norm, mla_kv_norm, mla_w_uq, mla_w_ukv, mla_w_out, hg_w_in, hg_lower_bounds, hg_norm_g, hg_w_out):
    B, L, D = x.shape
    Lc = ctx.shape[1]
    depth = ada_w.shape[0]
    alpha = (2 * depth) ** 0.25
    tm = 512 if (B * Lc) % 512 == 0 else 256
    rows = _Rows(B, L, Lc, tm)
    n_lat_rows = B * L

    h = jnp.concatenate([x.reshape(B * L, D), ctx.reshape(B * Lc, D)], 0)

    G = 8 * (-(-(B + 1) // 8))
    cond_in = jnp.zeros((G, D), F32).at[0].set(c_ctx).at[1:B + 1].set(c)
    mods = _adaln(cond_in, ada_w, ada_b).reshape(depth, G, 6, D)

    bf = lambda w: w.astype(BF16)
    H = MLA_HEADS
    wd = jnp.zeros((D, MLA_Q_LORA + MLA_KV_LORA + LANES), F32)
    wd = wd.at[:, :MLA_Q_LORA + MLA_KV_LORA].set(mla_w_down[:, :MLA_Q_LORA + MLA_KV_LORA])
    wd = wd.at[:, MLA_Q_LORA + MLA_KV_LORA + MLA_NOPE:MLA_Q_LORA + MLA_KV_LORA + MLA_NOPE + MLA_ROPE].set(
        mla_w_down[:, MLA_Q_LORA + MLA_KV_LORA:])
    wq = jnp.pad(mla_w_uq.reshape(MLA_Q_LORA, H, MLA_NOPE + MLA_ROPE),
                 ((0, 0), (0, 0), (0, LANES - MLA_NOPE - MLA_ROPE))).reshape(MLA_Q_LORA, H * LANES)
    wukv = mla_w_ukv.reshape(MLA_KV_LORA, H, MLA_NOPE + MLA_V)
    wk = jnp.pad(wukv[:, :, :MLA_NOPE], ((0, 0), (0, 0), (0, LANES - MLA_NOPE))).reshape(MLA_KV_LORA, H * LANES)
    wvt = jnp.pad(wukv[:, :, MLA_NOPE:], ((0, 0), (0, 0), (0, LANES - MLA_V))).reshape(MLA_KV_LORA, H * LANES).T
    vone = jnp.tile((jnp.arange(LANES) == MLA_V).astype(F32), H).reshape(H * LANES, 1)

    for i in range(depth):
        mod = mods[i]
        kind = i % 4
        want_ctx = i < depth - 1
        if kind == 0:
            lg = -jnp.exp(ret_decay.astype(F32))
            proj = _modmm(rows, h, mod, bf(ret_w_in), BF16, 1024,
                          rope=_ret_rope_tables(L, tm), k_scale=256 ** -0.5)
            y = _ret_scan(proj, lg, B, L, Lc)
            h1 = _outproj_ln(rows, h, mod, y, bf(ret_w_out), ln_g[i, 0], ln_b[i, 0], alpha,
                             gate_src=(proj, 2))
        elif kind == 1:
            qkv, vt = _modmm(rows, h, mod, bf(na_w_qkv), BF16, 1024, vt_tile=2)
            o_lat = _na_attention(qkv, vt, _na_bias_table(na_rpb), B, L, Lc)
            o_ctx = _ctx_attention((qkv, D, 0), (qkv, D, 1), (qkv, D, 2), B, L, Lc, NA_HEADS,
                                   split64=True, q_scale=0.125)
            y = jnp.concatenate([o_lat, o_ctx], 0)
            h1 = _outproj_ln(rows, h, mod, y, bf(na_w_out), ln_g[i, 0], ln_b[i, 0], alpha)
        elif kind == 2:
            q, k, vt = _mla_proj(rows, h, mod, bf(wd), mla_q_norm, mla_kv_norm, bf(wq), bf(wk), bf(wvt), vone,
                                 _mla_rope_tables(L, tm))
            tk = 256
            o_lat = _mla_attention(q, k, vt, B, L, Lc, 512, tk, latent_queries=True)
            o_ctx = _mla_attention(q, k, vt, B, L, Lc, Lc, tk, latent_queries=False)
            y = jnp.concatenate([o_lat, o_ctx], 0)
            h1 = _outproj_ln(rows, h, mod, y, bf(mla_w_out), ln_g[i, 0], ln_b[i, 0], alpha)
        else:
            lb_soft = jax.nn.softmax(hg_lower_bounds.astype(F32), axis=0)
            lb = (jnp.cumsum(lb_soft, axis=0) - lb_soft[0])[i]
            proj = _modmm(rows, h, mod, bf(hg_w_in), F32, 1024)
            y = _hg_scan(proj, lb, hg_norm_g, B, L, Lc)
            h1 = _outproj_ln(rows, h, mod, y, bf(hg_w_out), ln_g[i, 0], ln_b[i, 0], alpha,
                             gate_src=(proj, 4))
        n_tiles = rows.n_all if want_ctx else rows.n_lat
        h = _ffn(rows, h1, mod, bf(ffn_w13[i]), bf(ffn_w2[i]), ln_g[i, 1], ln_b[i, 1], alpha, n_tiles)
    return h[:n_lat_rows].reshape(B, L, D)
```

```python
import functools

import numpy as np
import jax
import jax.numpy as jnp
from jax import lax
from jax.experimental import pallas as pl
from jax.experimental.pallas import tpu as pltpu

F32 = jnp.float32
BF16 = jnp.bfloat16

GRID_W = 64
LN_EPS = 1e-5
RMS_EPS = 1e-6
ROPE_BASE = 10000.0
NEG_INF = -1e30

RET_HEADS = 4
NA_HEADS = 16
NA_WIN_ROWS = 8
NA_WIN_COLS = 16
MLA_HEADS = 16
MLA_NOPE = 64
MLA_ROPE = 32
MLA_V = 64
MLA_Q_LORA = 512
MLA_KV_LORA = 256
HG_EXPAND = 128
HG_CHUNK = 64
RET_CHUNK = 256

VMEM_LIMIT = 56 * 1024 * 1024
LANES = 128


def _cparams(sem):
    return pltpu.CompilerParams(dimension_semantics=sem, vmem_limit_bytes=VMEM_LIMIT)


def _const_spec(shape):
    nd = len(shape)
    return pl.BlockSpec(shape, lambda *_: (0,) * nd, pipeline_mode=pl.Buffered(1))


def _dot(a, b):
    return jnp.dot(a, b, preferred_element_type=F32)


def _dot_nt(a, b):
    return lax.dot_general(a, b, (((1,), (1,)), ((), ())), preferred_element_type=F32)


def _dot_tn(a, b):
    return lax.dot_general(a, b, (((0,), (0,)), ((), ())), preferred_element_type=F32)


def _silu(x):
    return x * jax.nn.sigmoid(x)


def _layer_norm(r, g, b):
    mu = jnp.mean(r, axis=-1, keepdims=True)
    rc = r - mu
    var = jnp.mean(rc * rc, axis=-1, keepdims=True)
    return rc * lax.rsqrt(var + LN_EPS) * g + b


class _Rows:
    def __init__(self, B, L, Lc, tm):
        assert L % tm == 0 and (B * Lc) % tm == 0
        self.B, self.L, self.Lc, self.tm = B, L, Lc, tm
        self.n_lat = B * L // tm
        self.n_all = (B * L + B * Lc) // tm
        self.per_b = L // tm

    def group(self, i):
        return jnp.where(i < self.n_lat, 1 + i // self.per_b, 0)

    def pos_block(self, i):
        return jnp.where(i < self.n_lat, i % self.per_b, self.per_b)


def _adaln_kernel(c_ref, w_ref, b_ref, o_ref):
    cond = _silu(c_ref[...])
    o_ref[0] = jnp.dot(cond, w_ref[0], preferred_element_type=F32,
                       precision=lax.Precision.HIGHEST) + b_ref[0]


def _adaln(cond_in, ada_w, ada_b):
    depth, D, N = ada_w.shape
    G = cond_in.shape[0]
    tn = 1536
    return pl.pallas_call(
        _adaln_kernel,
        grid=(depth, N // tn),
        in_specs=[pl.BlockSpec((G, D), lambda l, j: (0, 0)),
                  pl.BlockSpec((1, D, tn), lambda l, j: (l, 0, j)),
                  pl.BlockSpec((1, 1, tn), lambda l, j: (l, 0, j))],
        out_specs=pl.BlockSpec((1, G, tn), lambda l, j: (l, 0, j)),
        out_shape=jax.ShapeDtypeStruct((depth, G, N), F32),
        compiler_params=_cparams(("arbitrary", "arbitrary")),
        name="adaln",
    )(cond_in, ada_w, ada_b.reshape(depth, 1, N))


def _proj_kernel(x_ref, mod_ref, w_ref, *rest, plan, rope):
    if rope:
        cos_ref, sin_ref = rest[:2]
        outs = rest[2:]
    else:
        outs = rest
    sh = mod_ref[0, 0:1, :]
    sc = mod_ref[0, 1:2, :]
    a = (x_ref[...] * (1.0 + sc) + sh).astype(BF16)
    for c0, width, kind, oi, oc0, scale, t_out in plan:
        acc = _dot(a, w_ref[:, c0:c0 + width])
        o_ref = outs[oi]
        if kind == "rope":
            cos = cos_ref[...]
            sin = sin_ref[...]
            for h in range(width // (2 * LANES)):
                x1 = acc[:, h * 256:h * 256 + LANES]
                x2 = acc[:, h * 256 + LANES:(h + 1) * 256]
                lo = oc0 + h * 256
                o_ref[:, lo:lo + LANES] = ((x1 * cos - x2 * sin) * scale).astype(o_ref.dtype)
                o_ref[:, lo + LANES:lo + 256] = ((x1 * sin + x2 * cos) * scale).astype(o_ref.dtype)
        else:
            o_ref[:, oc0:oc0 + width] = acc.astype(o_ref.dtype)
        if t_out is not None:
            outs[t_out][...] = acc.T.astype(outs[t_out].dtype)


def _proj(rows, x, mod, w, plan, outs, rope=None):
    R, D = x.shape
    tm = rows.tm
    in_specs = [pl.BlockSpec((tm, D), lambda i: (i, 0)),
                pl.BlockSpec((1, 6, D), lambda i: (rows.group(i), 0, 0)),
                _const_spec(w.shape)]
    args = [x, mod, w]
    if rope is not None:
        in_specs += [pl.BlockSpec((tm, LANES), lambda i: (rows.pos_block(i), 0))] * 2
        args += list(rope)
    out_specs, out_shape = [], []
    for cols, dtype, transposed in outs:
        if transposed:
            out_specs.append(pl.BlockSpec((cols, tm), lambda i: (0, i)))
            out_shape.append(jax.ShapeDtypeStruct((cols, R), dtype))
        else:
            out_specs.append(pl.BlockSpec((tm, cols), lambda i: (i, 0)))
            out_shape.append(jax.ShapeDtypeStruct((R, cols), dtype))
    return pl.pallas_call(
        functools.partial(_proj_kernel, plan=tuple(plan), rope=rope is not None),
        grid=(rows.n_all,),
        in_specs=in_specs,
        out_specs=out_specs,
        out_shape=out_shape,
        compiler_params=_cparams(("arbitrary",)),
        name="proj",
    )(*args)


def _post_kernel(x_ref, mod_ref, *rest, scan_heads, has_norm_g, alpha, fc):
    if scan_heads:
        of_ref, ob_ref, g_ref = rest[:3]
        rest = rest[3:]
        if has_norm_g:
            ng_ref, rest = rest[0], rest[1:]
        t = of_ref[...].astype(F32) + ob_ref[...].astype(F32)
        dh = t.shape[1] // scan_heads
        parts = []
        for h in range(scan_heads):
            th = t[:, h * dh:(h + 1) * dh]
            yh = th * lax.rsqrt(jnp.mean(th * th, axis=-1, keepdims=True) + RMS_EPS)
            parts.append(yh * ng_ref[...] if has_norm_g else yh)
        mix = (jnp.concatenate(parts, axis=1) * _silu(g_ref[...].astype(F32))).astype(BF16)
    else:
        mix = rest[0][...]
        rest = rest[1:]
    wo_ref, ln1g_ref, ln1b_ref, w13_ref, w2_ref, ln2g_ref, ln2b_ref, o_ref = rest
    r = alpha * x_ref[...] + mod_ref[0, 2:3, :] * _dot(mix, wo_ref[...])
    h1 = _layer_norm(r, ln1g_ref[...], ln1b_ref[...])
    a = (h1 * (1.0 + mod_ref[0, 4:5, :]) + mod_ref[0, 3:4, :]).astype(BF16)
    F = w2_ref.shape[0]
    acc = jnp.zeros(h1.shape, F32)
    for c in range(F // fc):
        g = _dot(a, w13_ref[:, c * fc:(c + 1) * fc])
        u = _dot(a, w13_ref[:, F + c * fc:F + (c + 1) * fc])
        acc = acc + _dot((_silu(g) * u).astype(BF16), w2_ref[c * fc:(c + 1) * fc, :])
    r2 = alpha * h1 + mod_ref[0, 5:6, :] * acc
    o_ref[...] = _layer_norm(r2, ln2g_ref[...], ln2b_ref[...])


def _post(rows, x, mod, mix, w_out, ln_g, ln_b, w13, w2, alpha, n_tiles, scan_heads=0, norm_g=None):
    R, D = x.shape
    K = w_out.shape[0]
    F = w2.shape[0]
    tm = rows.tm
    row = lambda cols, blk=0: pl.BlockSpec((tm, cols), lambda i: (i, blk))
    in_specs = [row(D), pl.BlockSpec((1, 6, D), lambda i: (rows.group(i), 0, 0))]
    args = [x, mod]
    if scan_heads:
        o_f, o_b, (g_arr, g_blk) = mix
        in_specs += [row(K), row(K), row(K, g_blk)]
        args += [o_f, o_b, g_arr]
        if norm_g is not None:
            in_specs.append(_const_spec((1, norm_g.shape[0])))
            args.append(norm_g.reshape(1, -1))
    else:
        in_specs.append(row(K))
        args.append(mix)
    vec = lambda v: v.reshape(1, D)
    in_specs += [_const_spec((K, D)), _const_spec((1, D)), _const_spec((1, D)),
                 _const_spec((D, 2 * F)), _const_spec((F, D)), _const_spec((1, D)), _const_spec((1, D))]
    args += [w_out, vec(ln_g[0]), vec(ln_b[0]), w13, w2, vec(ln_g[1]), vec(ln_b[1])]
    return pl.pallas_call(
        functools.partial(_post_kernel, scan_heads=scan_heads, has_norm_g=norm_g is not None,
                          alpha=alpha, fc=256),
        grid=(n_tiles,),
        in_specs=in_specs,
        out_specs=row(D),
        out_shape=jax.ShapeDtypeStruct((n_tiles * tm, D), F32),
        compiler_params=_cparams(("arbitrary",)),
        name="post",
    )(*args)


def _chunk_order(B, L, Lc, C, backward):
    ncc, ncl = Lc // C, L // C

    def blk(b, j):
        if backward:
            in_ctx = j < ncc
            return jnp.where(in_ctx, B * ncl + b * ncc + (ncc - 1 - j), b * ncl + (ncl - 1 - (j - ncc)))
        return jnp.where(j < ncc, B * ncl + b * ncc + j, b * ncl + (j - ncc))

    return blk, ncc + ncl


def _ret_scan_kernel(lg_ref, qf_ref, kf_ref, vf_ref, qb_ref, kb_ref, vb_ref, of_ref, ob_ref, st_scr,
                     *, C, heads):
    j = pl.program_id(1)

    @pl.when(j == 0)
    def _():
        st_scr[...] = jnp.zeros_like(st_scr)

    dk = qf_ref.shape[1] // heads
    dv = vf_ref.shape[1] // heads
    ti = lax.broadcasted_iota(jnp.int32, (C, C), 0)
    si = lax.broadcasted_iota(jnp.int32, (C, C), 1)
    tcol = lax.broadcasted_iota(jnp.int32, (C, 1), 0)
    for d, (q_ref, k_ref, v_ref, o_ref) in enumerate(((qf_ref, kf_ref, vf_ref, of_ref),
                                                      (qb_ref, kb_ref, vb_ref, ob_ref))):
        backward = d == 1
        dist = (si - ti) if backward else (ti - si)
        keep = dist >= 0
        fdist = jnp.maximum(dist, 0).astype(F32)
        eq = ((C - tcol) if backward else (tcol + 1)).astype(F32)
        ek = (tcol if backward else (C - 1 - tcol)).astype(F32)
        for h in range(heads):
            lg = lg_ref[d, h]
            q = q_ref[:, h * dk:(h + 1) * dk]
            k = k_ref[:, h * dk:(h + 1) * dk]
            v = v_ref[:, h * dv:(h + 1) * dv]
            decay = jnp.where(keep, jnp.exp(lg * fdist), 0.0)
            att = (_dot_nt(q, k) * decay).astype(BF16)
            st = st_scr[d, h]
            o = _dot(att, v) + _dot_nt(q, st.astype(BF16)) * jnp.exp(lg * eq)
            kd = (k.astype(F32) * jnp.exp(lg * ek)).astype(BF16)
            st_scr[d, h] = st * jnp.exp(lg * jnp.full((1, 1), float(C), F32)) + _dot_tn(v, kd)
            o_ref[:, h * dv:(h + 1) * dv] = o.astype(o_ref.dtype)


def _ret_scan(proj, lg, B, L, Lc):
    R = proj.shape[0]
    C = RET_CHUNK
    H, dk, dv = RET_HEADS, 256, 512
    blk_f, nch = _chunk_order(B, L, Lc, C, False)
    blk_b, _ = _chunk_order(B, L, Lc, C, True)
    specs = lambda blk: [pl.BlockSpec((C, H * dk), lambda b, j: (blk(b, j), 0)),
                         pl.BlockSpec((C, H * dk), lambda b, j: (blk(b, j), 1)),
                         pl.BlockSpec((C, H * dv), lambda b, j: (blk(b, j), 1))]
    return pl.pallas_call(
        functools.partial(_ret_scan_kernel, C=C, heads=H),
        grid=(B, nch),
        in_specs=[pl.BlockSpec(memory_space=pltpu.SMEM)] + specs(blk_f) + specs(blk_b),
        out_specs=[pl.BlockSpec((C, H * dv), lambda b, j: (blk_f(b, j), 0)),
                   pl.BlockSpec((C, H * dv), lambda b, j: (blk_b(b, j), 0))],
        out_shape=[jax.ShapeDtypeStruct((R, H * dv), BF16)] * 2,
        scratch_shapes=[pltpu.VMEM((2, H, dv, dk), F32)],
        compiler_params=_cparams(("arbitrary", "arbitrary")),
        name="ret_scan",
    )(lg, proj, proj, proj, proj, proj, proj)


def _cumsum_rows(x):
    n = x.shape[0]
    r = lax.broadcasted_iota(jnp.int32, (n, 1), 0)
    s = 1
    while s < n:
        x = x + jnp.where(r >= s, pltpu.roll(x, s, axis=0), 0.0)
        s *= 2
    return x


def _hg_scan_kernel(qf_ref, ff_ref, vf_ref, qb_ref, fb_ref, vb_ref, lb_ref, of_ref, ob_ref, st_scr,
                    *, C, heads, scale):
    j = pl.program_id(1)

    @pl.when(j == 0)
    def _():
        st_scr[...] = jnp.zeros_like(st_scr)

    lb = lb_ref[...]
    ti = lax.broadcasted_iota(jnp.int32, (C, C), 0)
    si = lax.broadcasted_iota(jnp.int32, (C, C), 1)
    d = HG_EXPAND
    for dr, (q_ref, f_ref, v_ref, o_ref) in enumerate(((qf_ref, ff_ref, vf_ref, of_ref),
                                                       (qb_ref, fb_ref, vb_ref, ob_ref))):
        backward = dr == 1
        forget = lb + (1.0 - lb) * jax.nn.sigmoid(f_ref[...])
        kk = 1.0 - forget
        gl = jnp.log(forget)
        pre = _cumsum_rows(gl)
        tot = pre[C - 1:C, :]
        bc = (tot - pre + gl) if backward else pre
        qd = (_silu(q_ref[...].astype(F32)) * scale * jnp.exp(bc)).astype(BF16)
        kd = (kk * jnp.exp(-bc)).astype(BF16)
        ke = (kk * jnp.exp(tot - bc)).astype(BF16)
        v = v_ref[...]
        etot = jnp.exp(tot)
        keep = (si >= ti) if backward else (ti >= si)
        for h in range(heads):
            sl = slice(h * d, (h + 1) * d)
            att = jnp.where(keep, _dot_nt(qd[:, sl], kd[:, sl]), 0.0).astype(BF16)
            st = st_scr[dr, h]
            o = _dot(att, v[:, sl]) + _dot_nt(qd[:, sl], st.astype(BF16))
            st_scr[dr, h] = st * etot[:, sl] + _dot_tn(v[:, sl], ke[:, sl])
            o_ref[:, sl] = o.astype(o_ref.dtype)


def _hg_scan(qig, ff, lb, B, L, Lc):
    R = qig.shape[0]
    C = HG_CHUNK
    Dm = qig.shape[1] // 3
    heads = Dm // HG_EXPAND
    blk_f, nch = _chunk_order(B, L, Lc, C, False)
    blk_b, _ = _chunk_order(B, L, Lc, C, True)
    specs = lambda blk, fcol: [pl.BlockSpec((C, Dm), lambda b, j: (blk(b, j), 0)),
                               pl.BlockSpec((C, Dm), lambda b, j: (blk(b, j), fcol)),
                               pl.BlockSpec((C, Dm), lambda b, j: (blk(b, j), 1))]
    return pl.pallas_call(
        functools.partial(_hg_scan_kernel, C=C, heads=heads, scale=HG_EXPAND ** -0.5),
        grid=(B, nch),
        in_specs=specs(blk_f, 0) + specs(blk_b, 1) + [pl.BlockSpec((1, Dm), lambda b, j: (0, 0))],
        out_specs=[pl.BlockSpec((C, Dm), lambda b, j: (blk_f(b, j), 0)),
                   pl.BlockSpec((C, Dm), lambda b, j: (blk_b(b, j), 0))],
        out_shape=[jax.ShapeDtypeStruct((R, Dm), BF16)] * 2,
        scratch_shapes=[pltpu.VMEM((2, heads, HG_EXPAND, HG_EXPAND), F32)],
        compiler_params=_cparams(("arbitrary", "arbitrary")),
        name="hg_scan",
    )(qig, ff, qig, qig, ff, qig, lb.reshape(1, Dm))


def _softmax_pv(scores, values):
    m = functools.reduce(jnp.maximum, [jnp.max(s, axis=-1, keepdims=True) for s in scores])
    ps = [jnp.exp(s - m) for s in scores]
    l = functools.reduce(lambda a, b: a + b, [jnp.sum(p, axis=-1, keepdims=True) for p in ps])
    o = functools.reduce(lambda a, b: a + b, [_dot(p.astype(BF16), v) for p, v in zip(ps, values)])
    return o / l


def _lane_lo(shape):
    return lax.broadcasted_iota(jnp.int32, shape, 1) < (LANES // 2)


NA_FRAME_ROWS = NA_WIN_ROWS + 2


def _na_kernel(q_ref, kw_ref, vtw_ref, kc_ref, vtc_ref, bias_ref, o_ref, *, heads, nrows):
    W, wr, fr = GRID_W, NA_WIN_ROWS, NA_FRAME_ROWS
    r0 = 2 * pl.program_id(1)
    u = jnp.minimum(jnp.clip(r0 - wr // 2, 0, nrows - wr), nrows - fr)
    tile = []
    for j in range(fr):
        per = []
        for rho in range(2):
            r = r0 + rho
            rs = jnp.clip(r - wr // 2, 0, nrows - wr)
            ok = (u + j >= rs) & (u + j < rs + wr)
            per.append(jnp.where(ok, u + j - r + wr - 1, 2 * wr - 1))
        tile.append(per)
    lo = _lane_lo((2 * W, LANES))
    q = q_ref[...] * 0.125
    ones_l = jnp.ones((8, fr * W), BF16)
    ones_c = jnp.ones((8, kc_ref.shape[0]), BF16)
    ri = lax.broadcasted_iota(jnp.int32, (LANES, 4 * W), 0)
    li = lax.broadcasted_iota(jnp.int32, (LANES, 4 * W), 1)
    own_head = (ri >= W) == ((li % LANES) >= W)
    for hp in range(heads // 2):
        sl = slice(hp * LANES, (hp + 1) * LANES)
        q2 = q[:, sl]
        qlo = jnp.where(lo, q2, jnp.zeros_like(q2))
        qhi = jnp.where(lo, jnp.zeros_like(q2), q2)
        qblk = jnp.concatenate([qlo[:W], qhi[:W], qlo[W:], qhi[W:]], axis=0)
        s_raw = _dot_nt(kw_ref[:, sl], qblk)
        s_l = jnp.concatenate(
            [s_raw[j * W:(j + 1) * W]
             + jnp.concatenate([bias_ref[hp, tile[j][0]], bias_ref[hp, tile[j][1]]], axis=1)
             for j in range(fr)], axis=0)
        s_c = _dot_nt(kc_ref[:, sl], qblk)
        m = jnp.maximum(jnp.max(s_l, axis=0, keepdims=True), jnp.max(s_c, axis=0, keepdims=True))
        p_l = jnp.exp(s_l - m).astype(BF16)
        p_c = jnp.exp(s_c - m).astype(BF16)
        acc = (_dot(jnp.concatenate([vtw_ref[sl, :], ones_l], axis=0), p_l)
               + _dot(jnp.concatenate([vtc_ref[sl, :], ones_c], axis=0), p_c))
        ot = jnp.where(own_head, acc[:LANES] / acc[LANES:LANES + 1], 0.0)
        tr = ot.T
        for rho in range(2):
            blk = tr[rho * LANES:rho * LANES + W] + tr[rho * LANES + W:(rho + 1) * LANES]
            o_ref[rho * W:(rho + 1) * W, sl] = blk.astype(o_ref.dtype)


def _na_bias_table(rpb):
    H = rpb.shape[0]
    W, wr, wc = GRID_W, NA_WIN_ROWS, NA_WIN_COLS
    qcol = np.arange(W)[:, None]
    kcol = np.arange(W)[None, :]
    ws = np.clip(qcol - wc // 2, 0, W - wc)
    ok = (kcol >= ws) & (kcol < ws + wc)
    r_pad = jnp.pad(rpb.astype(F32), ((0, 0), (0, 0), (W - wc, W + wc - (2 * wc - 1))))
    skew = jnp.tile(r_pad, (1, 1, W))[:, :, :W * (2 * W - 1)].reshape(H, 2 * wr - 1, W, 2 * W - 1)
    t15 = jnp.where(ok[None, None], skew[:, :, :, W - 1:], NEG_INF)
    t = t15.reshape(H // 2, 2, 2 * wr - 1, W, W).transpose(0, 2, 4, 1, 3).reshape(H // 2, 2 * wr - 1, W, 2 * W)
    return jnp.concatenate([t, jnp.full((H // 2, 1, W, 2 * W), NEG_INF, F32)], axis=1)


def _na_attention(qkv, vt, bias, B, L, Lc):
    D = qkv.shape[1] // 3
    W, wr, fr = GRID_W, NA_WIN_ROWS, NA_FRAME_ROWS
    nrows = L // W
    assert nrows % 2 == 0 and nrows >= fr
    frame0 = lambda g: jnp.minimum(jnp.clip(2 * g - wr // 2, 0, nrows - wr), nrows - fr)
    ctx_blk0 = B * L // Lc
    return pl.pallas_call(
        functools.partial(_na_kernel, heads=NA_HEADS, nrows=nrows),
        grid=(B, nrows // 2),
        in_specs=[pl.BlockSpec((2 * W, D), lambda b, g: (b * (nrows // 2) + g, 0)),
                  pl.BlockSpec((pl.Element(fr * W), pl.Element(D)),
                               lambda b, g: ((b * nrows + frame0(g)) * W, D)),
                  pl.BlockSpec((pl.Element(D), pl.Element(fr * W)),
                               lambda b, g: (0, pl.multiple_of((b * nrows + frame0(g)) * W, 2 * W))),
                  pl.BlockSpec((Lc, D), lambda b, g: (ctx_blk0 + b, 1)),
                  pl.BlockSpec((D, Lc), lambda b, g: (0, ctx_blk0 + b)),
                  _const_spec(bias.shape)],
        out_specs=pl.BlockSpec((2 * W, D), lambda b, g: (b * (nrows // 2) + g, 0)),
        out_shape=jax.ShapeDtypeStruct((B * L, D), BF16),
        compiler_params=_cparams(("arbitrary", "arbitrary")),
        name="na_attn",
    )(qkv, qkv, vt, qkv, vt, bias)


def _ctx_attn_kernel(q_ref, k_ref, v_ref, o_ref, *, heads, split64, q_scale):
    lo = _lane_lo((q_ref.shape[0], LANES))
    for hp in range(heads // 2):
        vsl = slice(hp * LANES, (hp + 1) * LANES)
        v2 = v_ref[:, vsl]
        outs = []
        for e in range(2):
            if split64:
                q2 = q_ref[:, vsl] * q_scale
                qh = jnp.where(lo if e == 0 else ~lo, q2, jnp.zeros_like(q2))
                kh = k_ref[:, vsl]
            else:
                hsl = slice((2 * hp + e) * LANES, (2 * hp + e + 1) * LANES)
                qh, kh = q_ref[:, hsl], k_ref[:, hsl]
            outs.append(_softmax_pv([_dot_nt(qh, kh)], [v2]))
        o_ref[:, vsl] = jnp.where(lo, outs[0], outs[1]).astype(o_ref.dtype)


def _ctx_attention(q_src, k_src, v_src, B, L, Lc, heads, split64, q_scale=1.0):
    blk0 = B * L // Lc
    spec = lambda src: pl.BlockSpec((Lc, src[1]), lambda b: (blk0 + b, src[2]))
    Dv = v_src[1]
    return pl.pallas_call(
        functools.partial(_ctx_attn_kernel, heads=heads, split64=split64, q_scale=q_scale),
        grid=(B,),
        in_specs=[spec(q_src), spec(k_src), spec(v_src)],
        out_specs=pl.BlockSpec((Lc, Dv), lambda b: (b, 0)),
        out_shape=jax.ShapeDtypeStruct((B * Lc, Dv), BF16),
        compiler_params=_cparams(("arbitrary",)),
        name="ctx_attn",
    )(q_src[0], k_src[0], v_src[0])


def _rope_slot(x, cos, sina, sinb):
    return x * cos + pltpu.roll(x, LANES - 16, axis=1) * sina + pltpu.roll(x, 16, axis=1) * sinb


def _mla_proj_kernel(x_ref, mod_ref, wd_ref, qn_ref, kvn_ref, wq_ref, wk_ref, wvt_ref, vone_ref,
                     cos_ref, sina_ref, sinb_ref, q_ref, k_ref, vt_ref, *, heads, scale):
    sh = mod_ref[0, 0:1, :]
    sc = mod_ref[0, 1:2, :]
    a = (x_ref[...] * (1.0 + sc) + sh).astype(BF16)
    d = _dot(a, wd_ref[...])
    cq = d[:, :MLA_Q_LORA]
    ckv = d[:, MLA_Q_LORA:MLA_Q_LORA + MLA_KV_LORA]
    kr = d[:, MLA_Q_LORA + MLA_KV_LORA:]
    rms = lambda t, g: (t * lax.rsqrt(jnp.mean(t * t, axis=-1, keepdims=True) + RMS_EPS) * g).astype(BF16)
    cqn = rms(cq, qn_ref[...])
    ckvn = rms(ckv, kvn_ref[...])
    cos, sina, sinb = cos_ref[...], sina_ref[...], sinb_ref[...]
    krr = _rope_slot(kr, cos, sina, sinb)
    qf = _dot(cqn, wq_ref[...])
    kf = _dot(ckvn, wk_ref[...])
    for h in range(heads):
        sl = slice(h * LANES, (h + 1) * LANES)
        q_ref[:, sl] = (_rope_slot(qf[:, sl], cos, sina, sinb) * scale).astype(q_ref.dtype)
        k_ref[:, sl] = (kf[:, sl] + krr).astype(k_ref.dtype)
    vt_ref[...] = (_dot_nt(wvt_ref[...], ckvn) + vone_ref[...]).astype(vt_ref.dtype)


def _mla_proj(rows, x, mod, wd, qn, kvn, wq, wk, wvt, vone, tables):
    R, D = x.shape
    tm = rows.tm
    H = MLA_HEADS
    tab_spec = pl.BlockSpec((tm, LANES), lambda i: (rows.pos_block(i), 0))
    return pl.pallas_call(
        functools.partial(_mla_proj_kernel, heads=H,
                          scale=(MLA_NOPE + MLA_ROPE) ** -0.5 * float(np.log2(np.e))),
        grid=(rows.n_all,),
        in_specs=[pl.BlockSpec((tm, D), lambda i: (i, 0)),
                  pl.BlockSpec((1, 6, D), lambda i: (rows.group(i), 0, 0)),
                  _const_spec(wd.shape), _const_spec((1, MLA_Q_LORA)), _const_spec((1, MLA_KV_LORA)),
                  _const_spec(wq.shape), _const_spec(wk.shape), _const_spec(wvt.shape),
                  _const_spec(vone.shape), tab_spec, tab_spec, tab_spec],
        out_specs=[pl.BlockSpec((tm, H * LANES), lambda i: (i, 0)),
                   pl.BlockSpec((tm, H * LANES), lambda i: (i, 0)),
                   pl.BlockSpec((H * LANES, tm), lambda i: (0, i))],
        out_shape=[jax.ShapeDtypeStruct((R, H * LANES), BF16),
                   jax.ShapeDtypeStruct((R, H * LANES), BF16),
                   jax.ShapeDtypeStruct((H * LANES, R), BF16)],
        compiler_params=_cparams(("arbitrary",)),
        name="mla_proj",
    )(x, mod, wd, qn.reshape(1, -1), kvn.reshape(1, -1), wq, wk, wvt, vone, *tables)


def _mla_flash_kernel(q_ref, kc_ref, vtc_ref, *rest, tk, cpi, with_latent):
    if with_latent:
        k_ref, vt_ref, o_ref, sa_ref, sb_ref = rest
    else:
        (o_ref,) = rest
    tq = q_ref.shape[0]
    hsl = [slice(e * LANES, (e + 1) * LANES) for e in range(2)]
    qs = [q_ref[:, sl] for sl in hsl]

    def scores(kblk):
        return [_dot_nt(kblk[:, hsl[e]], qs[e]) for e in range(2)]

    vrows = MLA_V + 8

    def update(ss, vtblk, state):
        out = []
        for e in range(2):
            m, acc = state[e]
            s = ss[e]
            m_new = jnp.maximum(m, jnp.max(s, axis=0, keepdims=True))
            p = jnp.exp2(s - m_new).astype(BF16)
            acc = jnp.exp2(m - m_new) * acc + _dot(vtblk[e * LANES:e * LANES + vrows, :], p)
            out.append((m_new, acc))
        return tuple(out)

    init = (jnp.full((1, tq), NEG_INF, F32), jnp.zeros((vrows, tq), F32))
    if not with_latent:
        state = update(scores(kc_ref[...]), vtc_ref[...], (init, init))
    else:
        n = k_ref.shape[0] // tk
        assert cpi % 2 == 0 and n % cpi == 0
        bufs = (sa_ref, sb_ref)

        def put(buf, c):
            ss = scores(k_ref[pl.ds(pl.multiple_of(c * tk, tk), tk), :])
            for e in range(2):
                buf[e] = ss[e]

        def group(c0, st, last):
            for t in range(cpi):
                if not (last and t == cpi - 1):
                    put(bufs[(t + 1) % 2], c0 + t + 1)
                buf = bufs[t % 2]
                vtblk = vt_ref[:, pl.ds(pl.multiple_of((c0 + t) * tk, tk), tk)]
                st = update([buf[0], buf[1]], vtblk, st)
            return st

        put(bufs[0], 0)
        state = update(scores(kc_ref[...]), vtc_ref[...], (init, init))
        state = lax.fori_loop(0, n // cpi - 1, lambda i, st: group(i * cpi, st, False), state)
        state = group(n - cpi, state, True)
    ot = jnp.concatenate([acc[:MLA_V] / acc[MLA_V:MLA_V + 1] for _, acc in state], axis=0)
    o_ref[...] = ot.T.astype(o_ref.dtype)


def _mla_attention(q, k, vt, B, L, Lc, tq, tk, latent_queries):
    H = MLA_HEADS
    ctx_blk0 = B * L // Lc
    if latent_queries:
        nq, q0, n_out = L // tq, 0, B * L
    else:
        assert tq == Lc
        nq, q0, n_out = 1, ctx_blk0, B * Lc
    in_specs = [pl.BlockSpec((tq, 2 * LANES), lambda b, hp, i: (q0 + b * nq + i, hp)),
                pl.BlockSpec((Lc, 2 * LANES), lambda b, hp, i: (ctx_blk0 + b, hp)),
                pl.BlockSpec((2 * LANES, Lc), lambda b, hp, i: (hp, ctx_blk0 + b))]
    args = [q, k, vt]
    scratch = []
    if latent_queries:
        in_specs += [pl.BlockSpec((L, 2 * LANES), lambda b, hp, i: (b, hp)),
                     pl.BlockSpec((2 * LANES, L), lambda b, hp, i: (hp, b))]
        args += [k, vt]
        scratch = [pltpu.VMEM((2, tk, tq), F32)] * 2
    return pl.pallas_call(
        functools.partial(_mla_flash_kernel, tk=tk, cpi=4 if (L // tk) % 4 == 0 else 2,
                          with_latent=latent_queries),
        grid=(B, H // 2, nq),
        in_specs=in_specs,
        out_specs=pl.BlockSpec((tq, LANES), lambda b, hp, i: (b * nq + i, hp)),
        out_shape=jax.ShapeDtypeStruct((n_out, H * MLA_V), BF16),
        scratch_shapes=scratch,
        compiler_params=_cparams(("arbitrary", "arbitrary", "arbitrary")),
        name="mla_flash" if latent_queries else "mla_flash_ctx",
    )(*args)


def _axial_angles(L, rot_dim):
    t = jnp.arange(L)
    rows = (t // GRID_W).astype(F32)
    cols = (t % GRID_W).astype(F32)
    n_freq = rot_dim // 4
    inv = ROPE_BASE ** (-jnp.arange(n_freq, dtype=F32) / n_freq)
    return jnp.concatenate([rows[:, None] * inv, cols[:, None] * inv], -1)


def _ret_rope_tables(L, tm):
    ang = _axial_angles(L, 256)
    cos = jnp.concatenate([jnp.cos(ang), jnp.ones((tm, LANES), F32)], 0)
    sin = jnp.concatenate([jnp.sin(ang), jnp.zeros((tm, LANES), F32)], 0)
    return cos, sin


def _mla_rope_tables(L, tm):
    ang = _axial_angles(L, MLA_ROPE)
    c, s = jnp.cos(ang), jnp.sin(ang)
    one = jnp.ones((L, MLA_NOPE), F32)
    z16 = jnp.zeros((L, 16), F32)
    z32 = jnp.zeros((L, 32), F32)
    z64 = jnp.zeros((L, MLA_NOPE), F32)
    cos = jnp.concatenate([one, c, c, jnp.ones((L, 32), F32)], -1)
    sina = jnp.concatenate([z64, -s, z16, z32], -1)
    sinb = jnp.concatenate([z64, z16, s, z32], -1)
    ident = lambda t, fill: jnp.concatenate([t, jnp.full((tm, LANES), fill, F32)], 0)
    return ident(cos, 1.0), ident(sina, 0.0), ident(sinb, 0.0)


def kernel(x, c, ctx, c_ctx, ada_w, ada_b, ln_g, ln_b, ffn_w13, ffn_w2, ret_w_in, ret_decay, ret_w_out, na_w_qkv, na_rpb, na_w_out, mla_w_down, mla_q_norm, mla_kv_norm, mla_w_uq, mla_w_ukv, mla_w_out, hg_w_in, hg_lower_bounds, hg_norm_g, hg_w_out):
    B, L, D = x.shape
    Lc = ctx.shape[1]
    depth = ada_w.shape[0]
    alpha = (2 * depth) ** 0.25
    tm = 512 if (B * Lc) % 512 == 0 else 256
    rows = _Rows(B, L, Lc, tm)
    n_lat_rows = B * L

    h = jnp.concatenate([x.reshape(B * L, D), ctx.reshape(B * Lc, D)], 0)

    G = 8 * (-(-(B + 1) // 8))
    cond_in = jnp.zeros((G, D), F32).at[0].set(c_ctx).at[1:B + 1].set(c)
    mods = _adaln(cond_in, ada_w, ada_b).reshape(depth, G, 6, D)

    bf = lambda w: w.astype(BF16)
    H = MLA_HEADS
    wd = jnp.zeros((D, MLA_Q_LORA + MLA_KV_LORA + LANES), F32)
    wd = wd.at[:, :MLA_Q_LORA + MLA_KV_LORA].set(mla_w_down[:, :MLA_Q_LORA + MLA_KV_LORA])
    wd = wd.at[:, MLA_Q_LORA + MLA_KV_LORA + MLA_NOPE:MLA_Q_LORA + MLA_KV_LORA + MLA_NOPE + MLA_ROPE].set(
        mla_w_down[:, MLA_Q_LORA + MLA_KV_LORA:])
    wq = jnp.pad(mla_w_uq.reshape(MLA_Q_LORA, H, MLA_NOPE + MLA_ROPE),
                 ((0, 0), (0, 0), (0, LANES - MLA_NOPE - MLA_ROPE))).reshape(MLA_Q_LORA, H * LANES)
    wukv = mla_w_ukv.reshape(MLA_KV_LORA, H, MLA_NOPE + MLA_V)
    wk = jnp.pad(wukv[:, :, :MLA_NOPE], ((0, 0), (0, 0), (0, LANES - MLA_NOPE))).reshape(MLA_KV_LORA, H * LANES)
    wvt = jnp.pad(wukv[:, :, MLA_NOPE:], ((0, 0), (0, 0), (0, LANES - MLA_V))).reshape(MLA_KV_LORA, H * LANES).T
    vone = jnp.tile((jnp.arange(LANES) == MLA_V).astype(F32), H).reshape(H * LANES, 1)

    plain = lambda n, width=1024: [(c0, width, "plain", 0, c0, 1.0, None) for c0 in range(0, n, width)]
    for i in range(depth):
        mod = mods[i]
        kind = i % 4
        n_tiles = rows.n_all if i < depth - 1 else rows.n_lat
        post = functools.partial(_post, rows, h, mod, ln_g=ln_g[i], ln_b=ln_b[i], w13=bf(ffn_w13[i]),
                                 w2=bf(ffn_w2[i]), alpha=alpha, n_tiles=n_tiles)
        if kind == 0:
            lg = -jnp.exp(ret_decay.astype(F32))
            plan = [(0, 1024, "rope", 0, 0, 1.0, None), (1024, 1024, "rope", 0, 1024, 256 ** -0.5, None)]
            plan += plain(6144)[2:]
            (proj,) = _proj(rows, h, mod, bf(ret_w_in), plan, [(6144, BF16, False)],
                            rope=_ret_rope_tables(L, tm))
            o_f, o_b = _ret_scan(proj, lg, B, L, Lc)
            h = post(mix=(o_f, o_b, (proj, 2)), w_out=bf(ret_w_out), scan_heads=RET_HEADS)
        elif kind == 1:
            plan = plain(2048) + [(2048, 1024, "plain", 0, 2048, 1.0, 1)]
            qkv, vt = _proj(rows, h, mod, bf(na_w_qkv), plan, [(3072, BF16, False), (1024, BF16, True)])
            o_lat = _na_attention(qkv, vt, _na_bias_table(na_rpb), B, L, Lc)
            o_ctx = _ctx_attention((qkv, D, 0), (qkv, D, 1), (qkv, D, 2), B, L, Lc, NA_HEADS,
                                   split64=True, q_scale=0.125)
            h = post(mix=jnp.concatenate([o_lat, o_ctx], 0), w_out=bf(na_w_out))
        elif kind == 2:
            q, k, vt = _mla_proj(rows, h, mod, bf(wd), mla_q_norm, mla_kv_norm, bf(wq), bf(wk), bf(wvt), vone,
                                 _mla_rope_tables(L, tm))
            o_lat = _mla_attention(q, k, vt, B, L, Lc, 512, 256, latent_queries=True)
            o_ctx = _mla_attention(q, k, vt, B, L, Lc, Lc, 256, latent_queries=False)
            h = post(mix=jnp.concatenate([o_lat, o_ctx], 0), w_out=bf(mla_w_out))
        else:
            lb_soft = jax.nn.softmax(hg_lower_bounds.astype(F32), axis=0)
            lb = (jnp.cumsum(lb_soft, axis=0) - lb_soft[0])[i]
            plan = [(0, 1024, "plain", 0, 0, 1.0, None), (1024, 1024, "plain", 1, 0, 1.0, None),
                    (2048, 1024, "plain", 1, 1024, 1.0, None), (3072, 1024, "plain", 0, 1024, 1.0, None),
                    (4096, 1024, "plain", 0, 2048, 1.0, None)]
            qig, ff = _proj(rows, h, mod, bf(hg_w_in), plan, [(3072, BF16, False), (2048, F32, False)])
            o_f, o_b = _hg_scan(qig, ff, lb, B, L, Lc)
            h = post(mix=(o_f, o_b, (qig, 2)), w_out=bf(hg_w_out), scan_heads=D // HG_EXPAND,
                     norm_g=hg_norm_g)
    return h[:n_lat_rows].reshape(B, L, D)
```

```python
import functools

import numpy as np
import jax
import jax.numpy as jnp
from jax import lax
from jax.experimental import pallas as pl
from jax.experimental.pallas import tpu as pltpu

F32 = jnp.float32
BF16 = jnp.bfloat16

GRID_W = 64
LN_EPS = 1e-5
RMS_EPS = 1e-6
ROPE_BASE = 10000.0
NEG_INF = -1e30

RET_HEADS = 4
NA_HEADS = 16
NA_WIN_ROWS = 8
NA_WIN_COLS = 16
MLA_HEADS = 16
MLA_NOPE = 64
MLA_ROPE = 32
MLA_V = 64
MLA_Q_LORA = 512
MLA_KV_LORA = 256
HG_EXPAND = 128
HG_CHUNK = 64
RET_CHUNK = 256

VMEM_LIMIT = 56 * 1024 * 1024
LANES = 128


def _cparams(sem):
    return pltpu.CompilerParams(dimension_semantics=sem, vmem_limit_bytes=VMEM_LIMIT)


def _const_spec(shape):
    nd = len(shape)
    return pl.BlockSpec(shape, lambda *_: (0,) * nd, pipeline_mode=pl.Buffered(1))


def _dot(a, b):
    return jnp.dot(a, b, preferred_element_type=F32)


def _dot_nt(a, b):
    return lax.dot_general(a, b, (((1,), (1,)), ((), ())), preferred_element_type=F32)


def _dot_tn(a, b):
    return lax.dot_general(a, b, (((0,), (0,)), ((), ())), preferred_element_type=F32)


def _silu(x):
    return x * jax.nn.sigmoid(x)


def _layer_norm(r, g, b):
    mu = jnp.mean(r, axis=-1, keepdims=True)
    rc = r - mu
    var = jnp.mean(rc * rc, axis=-1, keepdims=True)
    return rc * lax.rsqrt(var + LN_EPS) * g + b


class _Rows:
    def __init__(self, B, L, Lc, tm):
        assert L % tm == 0 and (B * Lc) % tm == 0
        self.B, self.L, self.Lc, self.tm = B, L, Lc, tm
        self.n_lat = B * L // tm
        self.n_all = (B * L + B * Lc) // tm
        self.per_b = L // tm

    def group(self, i):
        return jnp.where(i < self.n_lat, 1 + i // self.per_b, 0)

    def pos_block(self, i):
        return jnp.where(i < self.n_lat, i % self.per_b, self.per_b)


def _adaln_kernel(c_ref, w_ref, b_ref, o_ref):
    cond = _silu(c_ref[...])
    o_ref[0] = jnp.dot(cond, w_ref[0], preferred_element_type=F32,
                       precision=lax.Precision.HIGHEST) + b_ref[0]


def _adaln(cond_in, ada_w, ada_b):
    depth, D, N = ada_w.shape
    G = cond_in.shape[0]
    tn = 1536
    return pl.pallas_call(
        _adaln_kernel,
        grid=(depth, N // tn),
        in_specs=[pl.BlockSpec((G, D), lambda l, j: (0, 0)),
                  pl.BlockSpec((1, D, tn), lambda l, j: (l, 0, j)),
                  pl.BlockSpec((1, 1, tn), lambda l, j: (l, 0, j))],
        out_specs=pl.BlockSpec((1, G, tn), lambda l, j: (l, 0, j)),
        out_shape=jax.ShapeDtypeStruct((depth, G, N), F32),
        compiler_params=_cparams(("arbitrary", "arbitrary")),
        name="adaln",
    )(cond_in, ada_w, ada_b.reshape(depth, 1, N))


def _proj_kernel(x_ref, mod_ref, w_ref, *rest, plan, rope):
    if rope:
        cos_ref, sin_ref = rest[:2]
        outs = rest[2:]
    else:
        outs = rest
    sh = mod_ref[0, 0:1, :]
    sc = mod_ref[0, 1:2, :]
    a = (x_ref[...] * (1.0 + sc) + sh).astype(BF16)
    for c0, width, kind, oi, oc0, scale, t_out in plan:
        acc = _dot(a, w_ref[:, c0:c0 + width])
        o_ref = outs[oi]
        if kind == "rope":
            cos = cos_ref[...]
            sin = sin_ref[...]
            for h in range(width // (2 * LANES)):
                x1 = acc[:, h * 256:h * 256 + LANES]
                x2 = acc[:, h * 256 + LANES:(h + 1) * 256]
                lo = oc0 + h * 256
                o_ref[:, lo:lo + LANES] = ((x1 * cos - x2 * sin) * scale).astype(o_ref.dtype)
                o_ref[:, lo + LANES:lo + 256] = ((x1 * sin + x2 * cos) * scale).astype(o_ref.dtype)
        else:
            o_ref[:, oc0:oc0 + width] = acc.astype(o_ref.dtype)
        if t_out is not None:
            outs[t_out][...] = acc.T.astype(outs[t_out].dtype)


def _proj(rows, x, mod, w, plan, outs, rope=None):
    R, D = x.shape
    tm = rows.tm
    in_specs = [pl.BlockSpec((tm, D), lambda i: (i, 0)),
                pl.BlockSpec((1, 6, D), lambda i: (rows.group(i), 0, 0)),
                _const_spec(w.shape)]
    args = [x, mod, w]
    if rope is not None:
        in_specs += [pl.BlockSpec((tm, LANES), lambda i: (rows.pos_block(i), 0))] * 2
        args += list(rope)
    out_specs, out_shape = [], []
    for cols, dtype, transposed in outs:
        if transposed:
            out_specs.append(pl.BlockSpec((cols, tm), lambda i: (0, i)))
            out_shape.append(jax.ShapeDtypeStruct((cols, R), dtype))
        else:
            out_specs.append(pl.BlockSpec((tm, cols), lambda i: (i, 0)))
            out_shape.append(jax.ShapeDtypeStruct((R, cols), dtype))
    return pl.pallas_call(
        functools.partial(_proj_kernel, plan=tuple(plan), rope=rope is not None),
        grid=(rows.n_all,),
        in_specs=in_specs,
        out_specs=out_specs,
        out_shape=out_shape,
        compiler_params=_cparams(("arbitrary",)),
        name="proj",
    )(*args)


def _post_kernel(x_ref, mod_ref, *rest, scan_heads, has_norm_g, alpha, fc, n_lat):
    if scan_heads:
        of_ref, ob_ref, g_ref = rest[:3]
        rest = rest[3:]
        if has_norm_g:
            ng_ref, rest = rest[0], rest[1:]
        t = of_ref[...].astype(F32) + ob_ref[...].astype(F32)
        dh = t.shape[1] // scan_heads
        parts = []
        for h in range(scan_heads):
            th = t[:, h * dh:(h + 1) * dh]
            yh = th * lax.rsqrt(jnp.mean(th * th, axis=-1, keepdims=True) + RMS_EPS)
            parts.append(yh * ng_ref[...] if has_norm_g else yh)
        mix = (jnp.concatenate(parts, axis=1) * _silu(g_ref[...].astype(F32))).astype(BF16)
    else:
        lat_ref, ctx_ref = rest[:2]
        rest, mix_scr = rest[2:-1], rest[-1]
        is_lat = pl.program_id(0) < n_lat

        @pl.when(is_lat)
        def _():
            mix_scr[...] = lat_ref[...]

        @pl.when(jnp.logical_not(is_lat))
        def _():
            mix_scr[...] = ctx_ref[...]

        mix = mix_scr[...]
    wo_ref, ln1g_ref, ln1b_ref, w13_ref, w2_ref, ln2g_ref, ln2b_ref, o_ref = rest
    r = alpha * x_ref[...] + mod_ref[0, 2:3, :] * _dot(mix, wo_ref[...])
    h1 = _layer_norm(r, ln1g_ref[...], ln1b_ref[...])
    a = (h1 * (1.0 + mod_ref[0, 4:5, :]) + mod_ref[0, 3:4, :]).astype(BF16)
    F = w2_ref.shape[0]
    acc = jnp.zeros(h1.shape, F32)
    for c in range(F // fc):
        g = _dot(a, w13_ref[:, c * fc:(c + 1) * fc])
        u = _dot(a, w13_ref[:, F + c * fc:F + (c + 1) * fc])
        acc = acc + _dot((_silu(g) * u).astype(BF16), w2_ref[c * fc:(c + 1) * fc, :])
    r2 = alpha * h1 + mod_ref[0, 5:6, :] * acc
    o_ref[...] = _layer_norm(r2, ln2g_ref[...], ln2b_ref[...])


def _post(rows, x, mod, mix, w_out, ln_g, ln_b, w13, w2, alpha, n_tiles, scan_heads=0, norm_g=None):
    R, D = x.shape
    K = w_out.shape[0]
    F = w2.shape[0]
    tm = rows.tm
    row = lambda cols, blk=0: pl.BlockSpec((tm, cols), lambda i: (i, blk))
    in_specs = [row(D), pl.BlockSpec((1, 6, D), lambda i: (rows.group(i), 0, 0))]
    args = [x, mod]
    scratch = []
    if scan_heads:
        o_f, o_b, (g_arr, g_blk) = mix
        in_specs += [row(K), row(K), row(K, g_blk)]
        args += [o_f, o_b, g_arr]
        if norm_g is not None:
            in_specs.append(_const_spec((1, norm_g.shape[0])))
            args.append(norm_g.reshape(1, -1))
    else:
        o_lat, o_ctx = mix
        in_specs += [pl.BlockSpec((tm, K), lambda i: (jnp.minimum(i, rows.n_lat - 1), 0)),
                     pl.BlockSpec((tm, K), lambda i: (jnp.maximum(i - rows.n_lat, 0), 0))]
        args += [o_lat, o_ctx]
        scratch = [pltpu.VMEM((tm, K), BF16)]
    vec = lambda v: v.reshape(1, D)
    in_specs += [_const_spec((K, D)), _const_spec((1, D)), _const_spec((1, D)),
                 _const_spec((D, 2 * F)), _const_spec((F, D)), _const_spec((1, D)), _const_spec((1, D))]
    args += [w_out, vec(ln_g[0]), vec(ln_b[0]), w13, w2, vec(ln_g[1]), vec(ln_b[1])]
    return pl.pallas_call(
        functools.partial(_post_kernel, scan_heads=scan_heads, has_norm_g=norm_g is not None,
                          alpha=alpha, fc=256, n_lat=rows.n_lat),
        grid=(n_tiles,),
        in_specs=in_specs,
        out_specs=row(D),
        out_shape=jax.ShapeDtypeStruct((n_tiles * tm, D), F32),
        scratch_shapes=scratch,
        compiler_params=_cparams(("arbitrary",)),
        name="post",
    )(*args)


def _chunk_order(B, L, Lc, C, backward):
    ncc, ncl = Lc // C, L // C

    def blk(b, j):
        if backward:
            in_ctx = j < ncc
            return jnp.where(in_ctx, B * ncl + b * ncc + (ncc - 1 - j), b * ncl + (ncl - 1 - (j - ncc)))
        return jnp.where(j < ncc, B * ncl + b * ncc + j, b * ncl + (j - ncc))

    return blk, ncc + ncl


def _ret_scan_kernel(lg_ref, qf_ref, kf_ref, vf_ref, qb_ref, kb_ref, vb_ref, of_ref, ob_ref, st_scr,
                     *, C, heads):
    j = pl.program_id(1)

    @pl.when(j == 0)
    def _():
        st_scr[...] = jnp.zeros_like(st_scr)

    dk = qf_ref.shape[1] // heads
    dv = vf_ref.shape[1] // heads
    ti = lax.broadcasted_iota(jnp.int32, (C, C), 0)
    si = lax.broadcasted_iota(jnp.int32, (C, C), 1)
    tcol = lax.broadcasted_iota(jnp.int32, (C, 1), 0)
    for d, (q_ref, k_ref, v_ref, o_ref) in enumerate(((qf_ref, kf_ref, vf_ref, of_ref),
                                                      (qb_ref, kb_ref, vb_ref, ob_ref))):
        backward = d == 1
        dist = (si - ti) if backward else (ti - si)
        keep = dist >= 0
        fdist = jnp.maximum(dist, 0).astype(F32)
        eq = ((C - tcol) if backward else (tcol + 1)).astype(F32)
        ek = (tcol if backward else (C - 1 - tcol)).astype(F32)
        for h in range(heads):
            lg = lg_ref[d, h]
            q = q_ref[:, h * dk:(h + 1) * dk]
            k = k_ref[:, h * dk:(h + 1) * dk]
            v = v_ref[:, h * dv:(h + 1) * dv]
            decay = jnp.where(keep, jnp.exp(lg * fdist), 0.0)
            att = (_dot_nt(q, k) * decay).astype(BF16)
            st = st_scr[d, h]
            o = _dot(att, v) + _dot_nt(q, st.astype(BF16)) * jnp.exp(lg * eq)
            kd = (k.astype(F32) * jnp.exp(lg * ek)).astype(BF16)
            st_scr[d, h] = st * jnp.exp(lg * jnp.full((1, 1), float(C), F32)) + _dot_tn(v, kd)
            o_ref[:, h * dv:(h + 1) * dv] = o.astype(o_ref.dtype)


def _ret_scan(proj, lg, B, L, Lc):
    R = proj.shape[0]
    C = RET_CHUNK
    H, dk, dv = RET_HEADS, 256, 512
    blk_f, nch = _chunk_order(B, L, Lc, C, False)
    blk_b, _ = _chunk_order(B, L, Lc, C, True)
    specs = lambda blk: [pl.BlockSpec((C, H * dk), lambda b, j: (blk(b, j), 0)),
                         pl.BlockSpec((C, H * dk), lambda b, j: (blk(b, j), 1)),
                         pl.BlockSpec((C, H * dv), lambda b, j: (blk(b, j), 1))]
    return pl.pallas_call(
        functools.partial(_ret_scan_kernel, C=C, heads=H),
        grid=(B, nch),
        in_specs=[pl.BlockSpec(memory_space=pltpu.SMEM)] + specs(blk_f) + specs(blk_b),
        out_specs=[pl.BlockSpec((C, H * dv), lambda b, j: (blk_f(b, j), 0)),
                   pl.BlockSpec((C, H * dv), lambda b, j: (blk_b(b, j), 0))],
        out_shape=[jax.ShapeDtypeStruct((R, H * dv), BF16)] * 2,
        scratch_shapes=[pltpu.VMEM((2, H, dv, dk), F32)],
        compiler_params=_cparams(("arbitrary", "arbitrary")),
        name="ret_scan",
    )(lg, proj, proj, proj, proj, proj, proj)


def _cumsum_rows(x):
    n = x.shape[0]
    r = lax.broadcasted_iota(jnp.int32, (n, 1), 0)
    s = 1
    while s < n:
        x = x + jnp.where(r >= s, pltpu.roll(x, s, axis=0), 0.0)
        s *= 2
    return x


def _hg_scan_kernel(qf_ref, ff_ref, vf_ref, qb_ref, fb_ref, vb_ref, lb_ref, of_ref, ob_ref, st_scr,
                    *, C, sub, heads, scale):
    j = pl.program_id(1)

    @pl.when(j == 0)
    def _():
        st_scr[...] = jnp.zeros_like(st_scr)

    lb = lb_ref[...]
    ti = lax.broadcasted_iota(jnp.int32, (C, C), 0)
    si = lax.broadcasted_iota(jnp.int32, (C, C), 1)
    d = HG_EXPAND
    for dr, (q_ref, f_ref, v_ref, o_ref) in enumerate(((qf_ref, ff_ref, vf_ref, of_ref),
                                                       (qb_ref, fb_ref, vb_ref, ob_ref))):
        backward = dr == 1
        keep = (si >= ti) if backward else (ti >= si)
        for c in (range(sub - 1, -1, -1) if backward else range(sub)):
            rs = slice(c * C, (c + 1) * C)
            forget = lb + (1.0 - lb) * jax.nn.sigmoid(f_ref[rs, :])
            kk = 1.0 - forget
            gl = jnp.log(forget)
            pre = _cumsum_rows(gl)
            tot = pre[C - 1:C, :]
            bc = (tot - pre + gl) if backward else pre
            qd = (_silu(q_ref[rs, :].astype(F32)) * scale * jnp.exp(bc)).astype(BF16)
            kd = (kk * jnp.exp(-bc)).astype(BF16)
            ke = (kk * jnp.exp(tot - bc)).astype(BF16)
            v = v_ref[rs, :]
            etot = jnp.exp(tot)
            for h in range(heads):
                sl = slice(h * d, (h + 1) * d)
                att = jnp.where(keep, _dot_nt(qd[:, sl], kd[:, sl]), 0.0).astype(BF16)
                st = st_scr[dr, h]
                o = _dot(att, v[:, sl]) + _dot_nt(qd[:, sl], st.astype(BF16))
                st_scr[dr, h] = st * etot[:, sl] + _dot_tn(v[:, sl], ke[:, sl])
                o_ref[rs, sl] = o.astype(o_ref.dtype)


def _hg_scan(qig, ff, lb, B, L, Lc):
    R = qig.shape[0]
    sub = 2
    C = HG_CHUNK * sub
    Dm = qig.shape[1] // 3
    heads = Dm // HG_EXPAND
    blk_f, nch = _chunk_order(B, L, Lc, C, False)
    blk_b, _ = _chunk_order(B, L, Lc, C, True)
    specs = lambda blk, fcol: [pl.BlockSpec((C, Dm), lambda b, j: (blk(b, j), 0)),
                               pl.BlockSpec((C, Dm), lambda b, j: (blk(b, j), fcol)),
                               pl.BlockSpec((C, Dm), lambda b, j: (blk(b, j), 1))]
    return pl.pallas_call(
        functools.partial(_hg_scan_kernel, C=HG_CHUNK, sub=sub, heads=heads, scale=HG_EXPAND ** -0.5),
        grid=(B, nch),
        in_specs=specs(blk_f, 0) + specs(blk_b, 1) + [pl.BlockSpec((1, Dm), lambda b, j: (0, 0))],
        out_specs=[pl.BlockSpec((C, Dm), lambda b, j: (blk_f(b, j), 0)),
                   pl.BlockSpec((C, Dm), lambda b, j: (blk_b(b, j), 0))],
        out_shape=[jax.ShapeDtypeStruct((R, Dm), BF16)] * 2,
        scratch_shapes=[pltpu.VMEM((2, heads, HG_EXPAND, HG_EXPAND), F32)],
        compiler_params=_cparams(("arbitrary", "arbitrary")),
        name="hg_scan",
    )(qig, ff, qig, qig, ff, qig, lb.reshape(1, Dm))


def _softmax_pv(scores, values):
    m = functools.reduce(jnp.maximum, [jnp.max(s, axis=-1, keepdims=True) for s in scores])
    ps = [jnp.exp(s - m) for s in scores]
    l = functools.reduce(lambda a, b: a + b, [jnp.sum(p, axis=-1, keepdims=True) for p in ps])
    o = functools.reduce(lambda a, b: a + b, [_dot(p.astype(BF16), v) for p, v in zip(ps, values)])
    return o / l


def _lane_lo(shape):
    return lax.broadcasted_iota(jnp.int32, shape, 1) < (LANES // 2)


NA_FRAME_ROWS = NA_WIN_ROWS + 2


def _na_kernel(q_ref, kw_ref, vtw_ref, kc_ref, vtc_ref, bias_ref, o_ref, *, heads, nrows):
    W, wr, fr = GRID_W, NA_WIN_ROWS, NA_FRAME_ROWS
    r0 = 2 * pl.program_id(1)
    u = jnp.minimum(jnp.clip(r0 - wr // 2, 0, nrows - wr), nrows - fr)
    tile = []
    for j in range(fr):
        per = []
        for rho in range(2):
            r = r0 + rho
            rs = jnp.clip(r - wr // 2, 0, nrows - wr)
            ok = (u + j >= rs) & (u + j < rs + wr)
            per.append(jnp.where(ok, u + j - r + wr - 1, 2 * wr - 1))
        tile.append(per)
    lo = _lane_lo((2 * W, LANES))
    q = q_ref[...] * 0.125
    ones_l = jnp.ones((8, fr * W), BF16)
    ones_c = jnp.ones((8, kc_ref.shape[0]), BF16)
    ri = lax.broadcasted_iota(jnp.int32, (LANES, 4 * W), 0)
    li = lax.broadcasted_iota(jnp.int32, (LANES, 4 * W), 1)
    own_head = (ri >= W) == ((li % LANES) >= W)
    for hp in range(heads // 2):
        sl = slice(hp * LANES, (hp + 1) * LANES)
        q2 = q[:, sl]
        qlo = jnp.where(lo, q2, jnp.zeros_like(q2))
        qhi = jnp.where(lo, jnp.zeros_like(q2), q2)
        qblk = jnp.concatenate([qlo[:W], qhi[:W], qlo[W:], qhi[W:]], axis=0)
        s_raw = _dot_nt(kw_ref[:, sl], qblk)
        s_l = jnp.concatenate(
            [s_raw[j * W:(j + 1) * W]
             + jnp.concatenate([bias_ref[hp, tile[j][0]], bias_ref[hp, tile[j][1]]], axis=1)
             for j in range(fr)], axis=0)
        s_c = _dot_nt(kc_ref[:, sl], qblk)
        m = jnp.maximum(jnp.max(s_l, axis=0, keepdims=True), jnp.max(s_c, axis=0, keepdims=True))
        p_l = jnp.exp(s_l - m).astype(BF16)
        p_c = jnp.exp(s_c - m).astype(BF16)
        acc = (_dot(jnp.concatenate([vtw_ref[sl, :], ones_l], axis=0), p_l)
               + _dot(jnp.concatenate([vtc_ref[sl, :], ones_c], axis=0), p_c))
        ot = jnp.where(own_head, acc[:LANES] / acc[LANES:LANES + 1], 0.0)
        tr = ot.T
        for rho in range(2):
            blk = tr[rho * LANES:rho * LANES + W] + tr[rho * LANES + W:(rho + 1) * LANES]
            o_ref[rho * W:(rho + 1) * W, sl] = blk.astype(o_ref.dtype)


def _na_bias_table(rpb):
    H = rpb.shape[0]
    W, wr, wc = GRID_W, NA_WIN_ROWS, NA_WIN_COLS
    qcol = np.arange(W)[:, None]
    kcol = np.arange(W)[None, :]
    ws = np.clip(qcol - wc // 2, 0, W - wc)
    ok = (kcol >= ws) & (kcol < ws + wc)
    r_pad = jnp.pad(rpb.astype(F32), ((0, 0), (0, 0), (W - wc, W + wc - (2 * wc - 1))))
    skew = jnp.tile(r_pad, (1, 1, W))[:, :, :W * (2 * W - 1)].reshape(H, 2 * wr - 1, W, 2 * W - 1)
    t15 = jnp.where(ok[None, None], skew[:, :, :, W - 1:], NEG_INF)
    t = t15.reshape(H // 2, 2, 2 * wr - 1, W, W).transpose(0, 2, 4, 1, 3).reshape(H // 2, 2 * wr - 1, W, 2 * W)
    return jnp.concatenate([t, jnp.full((H // 2, 1, W, 2 * W), NEG_INF, F32)], axis=1)


def _na_attention(qkv, vt, bias, B, L, Lc):
    D = qkv.shape[1] // 3
    W, wr, fr = GRID_W, NA_WIN_ROWS, NA_FRAME_ROWS
    nrows = L // W
    assert nrows % 2 == 0 and nrows >= fr
    frame0 = lambda g: jnp.minimum(jnp.clip(2 * g - wr // 2, 0, nrows - wr), nrows - fr)
    ctx_blk0 = B * L // Lc
    return pl.pallas_call(
        functools.partial(_na_kernel, heads=NA_HEADS, nrows=nrows),
        grid=(B, nrows // 2),
        in_specs=[pl.BlockSpec((2 * W, D), lambda b, g: (b * (nrows // 2) + g, 0)),
                  pl.BlockSpec((pl.Element(fr * W), pl.Element(D)),
                               lambda b, g: ((b * nrows + frame0(g)) * W, D)),
                  pl.BlockSpec((pl.Element(D), pl.Element(fr * W)),
                               lambda b, g: (0, pl.multiple_of((b * nrows + frame0(g)) * W, 2 * W))),
                  pl.BlockSpec((Lc, D), lambda b, g: (ctx_blk0 + b, 1)),
                  pl.BlockSpec((D, Lc), lambda b, g: (0, ctx_blk0 + b)),
                  _const_spec(bias.shape)],
        out_specs=pl.BlockSpec((2 * W, D), lambda b, g: (b * (nrows // 2) + g, 0)),
        out_shape=jax.ShapeDtypeStruct((B * L, D), BF16),
        compiler_params=_cparams(("arbitrary", "arbitrary")),
        name="na_attn",
    )(qkv, qkv, vt, qkv, vt, bias)


def _ctx_attn_kernel(q_ref, k_ref, v_ref, o_ref, *, heads, split64, q_scale):
    lo = _lane_lo((q_ref.shape[0], LANES))
    for hp in range(heads // 2):
        vsl = slice(hp * LANES, (hp + 1) * LANES)
        v2 = v_ref[:, vsl]
        outs = []
        for e in range(2):
            if split64:
                q2 = q_ref[:, vsl] * q_scale
                qh = jnp.where(lo if e == 0 else ~lo, q2, jnp.zeros_like(q2))
                kh = k_ref[:, vsl]
            else:
                hsl = slice((2 * hp + e) * LANES, (2 * hp + e + 1) * LANES)
                qh, kh = q_ref[:, hsl], k_ref[:, hsl]
            outs.append(_softmax_pv([_dot_nt(qh, kh)], [v2]))
        o_ref[:, vsl] = jnp.where(lo, outs[0], outs[1]).astype(o_ref.dtype)


def _ctx_attention(q_src, k_src, v_src, B, L, Lc, heads, split64, q_scale=1.0):
    blk0 = B * L // Lc
    spec = lambda src: pl.BlockSpec((Lc, src[1]), lambda b: (blk0 + b, src[2]))
    Dv = v_src[1]
    return pl.pallas_call(
        functools.partial(_ctx_attn_kernel, heads=heads, split64=split64, q_scale=q_scale),
        grid=(B,),
        in_specs=[spec(q_src), spec(k_src), spec(v_src)],
        out_specs=pl.BlockSpec((Lc, Dv), lambda b: (b, 0)),
        out_shape=jax.ShapeDtypeStruct((B * Lc, Dv), BF16),
        compiler_params=_cparams(("arbitrary",)),
        name="ctx_attn",
    )(q_src[0], k_src[0], v_src[0])


def _rope_slot(x, cos, sina, sinb):
    return x * cos + pltpu.roll(x, LANES - 16, axis=1) * sina + pltpu.roll(x, 16, axis=1) * sinb


def _mla_proj_kernel(x_ref, mod_ref, wd_ref, qn_ref, kvn_ref, wq_ref, wk_ref, wvt_ref, vone_ref,
                     cos_ref, sina_ref, sinb_ref, q_ref, k_ref, vt_ref, *, heads, scale):
    sh = mod_ref[0, 0:1, :]
    sc = mod_ref[0, 1:2, :]
    a = (x_ref[...] * (1.0 + sc) + sh).astype(BF16)
    d = _dot(a, wd_ref[...])
    cq = d[:, :MLA_Q_LORA]
    ckv = d[:, MLA_Q_LORA:MLA_Q_LORA + MLA_KV_LORA]
    kr = d[:, MLA_Q_LORA + MLA_KV_LORA:]
    rms = lambda t, g: (t * lax.rsqrt(jnp.mean(t * t, axis=-1, keepdims=True) + RMS_EPS) * g).astype(BF16)
    cqn = rms(cq, qn_ref[...])
    ckvn = rms(ckv, kvn_ref[...])
    cos, sina, sinb = cos_ref[...], sina_ref[...], sinb_ref[...]
    krr = _rope_slot(kr, cos, sina, sinb)
    qf = _dot(cqn, wq_ref[...])
    kf = _dot(ckvn, wk_ref[...])
    for h in range(heads):
        sl = slice(h * LANES, (h + 1) * LANES)
        q_ref[:, sl] = (_rope_slot(qf[:, sl], cos, sina, sinb) * scale).astype(q_ref.dtype)
        k_ref[:, sl] = (kf[:, sl] + krr).astype(k_ref.dtype)
    vt_ref[...] = (_dot_nt(wvt_ref[...], ckvn) + vone_ref[...]).astype(vt_ref.dtype)


def _mla_proj(rows, x, mod, wd, qn, kvn, wq, wk, wvt, vone, tables):
    R, D = x.shape
    tm = rows.tm
    H = MLA_HEADS
    tab_spec = pl.BlockSpec((tm, LANES), lambda i: (rows.pos_block(i), 0))
    return pl.pallas_call(
        functools.partial(_mla_proj_kernel, heads=H,
                          scale=(MLA_NOPE + MLA_ROPE) ** -0.5 * float(np.log2(np.e))),
        grid=(rows.n_all,),
        in_specs=[pl.BlockSpec((tm, D), lambda i: (i, 0)),
                  pl.BlockSpec((1, 6, D), lambda i: (rows.group(i), 0, 0)),
                  _const_spec(wd.shape), _const_spec((1, MLA_Q_LORA)), _const_spec((1, MLA_KV_LORA)),
                  _const_spec(wq.shape), _const_spec(wk.shape), _const_spec(wvt.shape),
                  _const_spec(vone.shape), tab_spec, tab_spec, tab_spec],
        out_specs=[pl.BlockSpec((tm, H * LANES), lambda i: (i, 0)),
                   pl.BlockSpec((tm, H * LANES), lambda i: (i, 0)),
                   pl.BlockSpec((H * LANES, tm), lambda i: (0, i))],
        out_shape=[jax.ShapeDtypeStruct((R, H * LANES), BF16),
                   jax.ShapeDtypeStruct((R, H * LANES), BF16),
                   jax.ShapeDtypeStruct((H * LANES, R), BF16)],
        compiler_params=_cparams(("arbitrary",)),
        name="mla_proj",
    )(x, mod, wd, qn.reshape(1, -1), kvn.reshape(1, -1), wq, wk, wvt, vone, *tables)


def _mla_flash_kernel(q_ref, kc_ref, vtc_ref, *rest, tk, cpi, with_latent):
    if with_latent:
        k_ref, vt_ref, o_ref, sa_ref, sb_ref = rest
    else:
        (o_ref,) = rest
    tq = q_ref.shape[0]
    hsl = [slice(e * LANES, (e + 1) * LANES) for e in range(2)]
    qs = [q_ref[:, sl] for sl in hsl]

    def scores(kblk):
        return [_dot_nt(kblk[:, hsl[e]], qs[e]) for e in range(2)]

    vrows = MLA_V + 8

    def colmax(ss):
        return [jnp.max(s, axis=0, keepdims=True) for s in ss]

    def update(ss, cms, vtblk, state):
        out = []
        for e in range(2):
            m, acc = state[e]
            m_new = jnp.maximum(m, cms[e])
            p = jnp.exp2(ss[e] - m_new).astype(BF16)
            acc = jnp.exp2(m - m_new) * acc + _dot(vtblk[e * LANES:e * LANES + vrows, :], p)
            out.append((m_new, acc))
        return tuple(out)

    init = (jnp.full((1, tq), NEG_INF, F32), jnp.zeros((vrows, tq), F32))
    s_ctx = scores(kc_ref[...])
    if not with_latent:
        state = update(s_ctx, colmax(s_ctx), vtc_ref[...], (init, init))
    else:
        n = k_ref.shape[0] // tk
        assert cpi % 2 == 0 and n % cpi == 0
        bufs = (sa_ref, sb_ref)

        def put(buf, c):
            ss = scores(k_ref[pl.ds(pl.multiple_of(c * tk, tk), tk), :])
            for e in range(2):
                buf[e] = ss[e]
            return tuple(colmax(ss))

        def group(c0, carry, last):
            st, cms = carry
            for t in range(cpi):
                nxt = None
                if not (last and t == cpi - 1):
                    nxt = put(bufs[(t + 1) % 2], c0 + t + 1)
                buf = bufs[t % 2]
                vtblk = vt_ref[:, pl.ds(pl.multiple_of((c0 + t) * tk, tk), tk)]
                st = update([buf[0], buf[1]], cms, vtblk, st)
                cms = nxt
            return st, cms

        cms = put(bufs[0], 0)
        state = update(s_ctx, colmax(s_ctx), vtc_ref[...], (init, init))
        carry = lax.fori_loop(0, n // cpi - 1, lambda i, cr: group(i * cpi, cr, False), (state, cms))
        state, _ = group(n - cpi, carry, True)
    ot = jnp.concatenate([acc[:MLA_V] / acc[MLA_V:MLA_V + 1] for _, acc in state], axis=0)
    o_ref[...] = ot.T.astype(o_ref.dtype)


def _mla_attention(q, k, vt, B, L, Lc, tq, tk, latent_queries):
    H = MLA_HEADS
    ctx_blk0 = B * L // Lc
    if latent_queries:
        nq, q0, n_out = L // tq, 0, B * L
    else:
        assert tq == Lc
        nq, q0, n_out = 1, ctx_blk0, B * Lc
    in_specs = [pl.BlockSpec((tq, 2 * LANES), lambda b, hp, i: (q0 + b * nq + i, hp)),
                pl.BlockSpec((Lc, 2 * LANES), lambda b, hp, i: (ctx_blk0 + b, hp)),
                pl.BlockSpec((2 * LANES, Lc), lambda b, hp, i: (hp, ctx_blk0 + b))]
    args = [q, k, vt]
    scratch = []
    if latent_queries:
        in_specs += [pl.BlockSpec((L, 2 * LANES), lambda b, hp, i: (b, hp)),
                     pl.BlockSpec((2 * LANES, L), lambda b, hp, i: (hp, b))]
        args += [k, vt]
        scratch = [pltpu.VMEM((2, tk, tq), F32)] * 2
    return pl.pallas_call(
        functools.partial(_mla_flash_kernel, tk=tk, cpi=4 if (L // tk) % 4 == 0 else 2,
                          with_latent=latent_queries),
        grid=(B, H // 2, nq),
        in_specs=in_specs,
        out_specs=pl.BlockSpec((tq, LANES), lambda b, hp, i: (b * nq + i, hp)),
        out_shape=jax.ShapeDtypeStruct((n_out, H * MLA_V), BF16),
        scratch_shapes=scratch,
        compiler_params=_cparams(("arbitrary", "arbitrary", "arbitrary")),
        name="mla_flash" if latent_queries else "mla_flash_ctx",
    )(*args)


def _axial_angles(L, rot_dim):
    t = jnp.arange(L)
    rows = (t // GRID_W).astype(F32)
    cols = (t % GRID_W).astype(F32)
    n_freq = rot_dim // 4
    inv = ROPE_BASE ** (-jnp.arange(n_freq, dtype=F32) / n_freq)
    return jnp.concatenate([rows[:, None] * inv, cols[:, None] * inv], -1)


def _ret_rope_tables(L, tm):
    ang = _axial_angles(L, 256)
    cos = jnp.concatenate([jnp.cos(ang), jnp.ones((tm, LANES), F32)], 0)
    sin = jnp.concatenate([jnp.sin(ang), jnp.zeros((tm, LANES), F32)], 0)
    return cos, sin


def _mla_rope_tables(L, tm):
    ang = _axial_angles(L, MLA_ROPE)
    c, s = jnp.cos(ang), jnp.sin(ang)
    one = jnp.ones((L, MLA_NOPE), F32)
    z16 = jnp.zeros((L, 16), F32)
    z32 = jnp.zeros((L, 32), F32)
    z64 = jnp.zeros((L, MLA_NOPE), F32)
    cos = jnp.concatenate([one, c, c, jnp.ones((L, 32), F32)], -1)
    sina = jnp.concatenate([z64, -s, z16, z32], -1)
    sinb = jnp.concatenate([z64, z16, s, z32], -1)
    ident = lambda t, fill: jnp.concatenate([t, jnp.full((tm, LANES), fill, F32)], 0)
    return ident(cos, 1.0), ident(sina, 0.0), ident(sinb, 0.0)


def kernel(x, c, ctx, c_ctx, ada_w, ada_b, ln_g, ln_b, ffn_w13, ffn_w2, ret_w_in, ret_decay, ret_w_out, na_w_qkv, na_rpb, na_w_out, mla_w_down, mla_q_norm, mla_kv_norm, mla_w_uq, mla_w_ukv, mla_w_out, hg_w_in, hg_lower_bounds, hg_norm_g, hg_w_out):
    B, L, D = x.shape
    Lc = ctx.shape[1]
    depth = ada_w.shape[0]
    alpha = (2 * depth) ** 0.25
    tm = 512 if (B * Lc) % 512 == 0 else 256
    rows = _Rows(B, L, Lc, tm)
    n_lat_rows = B * L

    h = jnp.concatenate([x.reshape(B * L, D), ctx.reshape(B * Lc, D)], 0)

    G = 8 * (-(-(B + 1) // 8))
    cond_in = jnp.zeros((G, D), F32).at[0].set(c_ctx).at[1:B + 1].set(c)
    mods = _adaln(cond_in, ada_w, ada_b).reshape(depth, G, 6, D)

    bf = lambda w: w.astype(BF16)
    H = MLA_HEADS
    wd = jnp.zeros((D, MLA_Q_LORA + MLA_KV_LORA + LANES), F32)
    wd = wd.at[:, :MLA_Q_LORA + MLA_KV_LORA].set(mla_w_down[:, :MLA_Q_LORA + MLA_KV_LORA])
    wd = wd.at[:, MLA_Q_LORA + MLA_KV_LORA + MLA_NOPE:MLA_Q_LORA + MLA_KV_LORA + MLA_NOPE + MLA_ROPE].set(
        mla_w_down[:, MLA_Q_LORA + MLA_KV_LORA:])
    wq = jnp.pad(mla_w_uq.reshape(MLA_Q_LORA, H, MLA_NOPE + MLA_ROPE),
                 ((0, 0), (0, 0), (0, LANES - MLA_NOPE - MLA_ROPE))).reshape(MLA_Q_LORA, H * LANES)
    wukv = mla_w_ukv.reshape(MLA_KV_LORA, H, MLA_NOPE + MLA_V)
    wk = jnp.pad(wukv[:, :, :MLA_NOPE], ((0, 0), (0, 0), (0, LANES - MLA_NOPE))).reshape(MLA_KV_LORA, H * LANES)
    wvt = jnp.pad(wukv[:, :, MLA_NOPE:], ((0, 0), (0, 0), (0, LANES - MLA_V))).reshape(MLA_KV_LORA, H * LANES).T
    vone = jnp.tile((jnp.arange(LANES) == MLA_V).astype(F32), H).reshape(H * LANES, 1)

    plain = lambda n, width=1024: [(c0, width, "plain", 0, c0, 1.0, None) for c0 in range(0, n, width)]
    for i in range(depth):
        mod = mods[i]
        kind = i % 4
        n_tiles = rows.n_all if i < depth - 1 else rows.n_lat
        post = functools.partial(_post, rows, h, mod, ln_g=ln_g[i], ln_b=ln_b[i], w13=bf(ffn_w13[i]),
                                 w2=bf(ffn_w2[i]), alpha=alpha, n_tiles=n_tiles)
        if kind == 0:
            lg = -jnp.exp(ret_decay.astype(F32))
            plan = [(0, 1024, "rope", 0, 0, 1.0, None), (1024, 1024, "rope", 0, 1024, 256 ** -0.5, None)]
            plan += plain(6144)[2:]
            (proj,) = _proj(rows, h, mod, bf(ret_w_in), plan, [(6144, BF16, False)],
                            rope=_ret_rope_tables(L, tm))
            o_f, o_b = _ret_scan(proj, lg, B, L, Lc)
            h = post(mix=(o_f, o_b, (proj, 2)), w_out=bf(ret_w_out), scan_heads=RET_HEADS)
        elif kind == 1:
            plan = plain(2048) + [(2048, 1024, "plain", 0, 2048, 1.0, 1)]
            qkv, vt = _proj(rows, h, mod, bf(na_w_qkv), plan, [(3072, BF16, False), (1024, BF16, True)])
            o_lat = _na_attention(qkv, vt, _na_bias_table(na_rpb), B, L, Lc)
            o_ctx = _ctx_attention((qkv, D, 0), (qkv, D, 1), (qkv, D, 2), B, L, Lc, NA_HEADS,
                                   split64=True, q_scale=0.125)
            h = post(mix=(o_lat, o_ctx), w_out=bf(na_w_out))
        elif kind == 2:
            q, k, vt = _mla_proj(rows, h, mod, bf(wd), mla_q_norm, mla_kv_norm, bf(wq), bf(wk), bf(wvt), vone,
                                 _mla_rope_tables(L, tm))
            o_lat = _mla_attention(q, k, vt, B, L, Lc, 512, 256, latent_queries=True)
            o_ctx = _mla_attention(q, k, vt, B, L, Lc, Lc, 256, latent_queries=False)
            h = post(mix=(o_lat, o_ctx), w_out=bf(mla_w_out))
        else:
            lb_soft = jax.nn.softmax(hg_lower_bounds.astype(F32), axis=0)
            lb = (jnp.cumsum(lb_soft, axis=0) - lb_soft[0])[i]
            plan = [(0, 1024, "plain", 0, 0, 1.0, None), (1024, 1024, "plain", 1, 0, 1.0, None),
                    (2048, 1024, "plain", 1, 1024, 1.0, None), (3072, 1024, "plain", 0, 1024, 1.0, None),
                    (4096, 1024, "plain", 0, 2048, 1.0, None)]
            qig, ff = _proj(rows, h, mod, bf(hg_w_in), plan, [(3072, BF16, False), (2048, F32, False)])
            o_f, o_b = _hg_scan(qig, ff, lb, B, L, Lc)
            h = post(mix=(o_f, o_b, (qig, 2)), w_out=bf(hg_w_out), scan_heads=D // HG_EXPAND,
                     norm_g=hg_norm_g)
    return h[:n_lat_rows].reshape(B, L, D)
```

```python
import functools

import numpy as np
import jax
import jax.numpy as jnp
from jax import lax
from jax.experimental import pallas as pl
from jax.experimental.pallas import tpu as pltpu

F32 = jnp.float32
BF16 = jnp.bfloat16

GRID_W = 64
LN_EPS = 1e-5
RMS_EPS = 1e-6
ROPE_BASE = 10000.0
NEG_INF = -1e30

RET_HEADS = 4
NA_HEADS = 16
NA_WIN_ROWS = 8
NA_WIN_COLS = 16
MLA_HEADS = 16
MLA_NOPE = 64
MLA_ROPE = 32
MLA_V = 64
MLA_Q_LORA = 512
MLA_KV_LORA = 256
HG_EXPAND = 128
HG_CHUNK = 64
RET_CHUNK = 256

VMEM_LIMIT = 56 * 1024 * 1024
LANES = 128


def _cparams(sem):
    return pltpu.CompilerParams(dimension_semantics=sem, vmem_limit_bytes=VMEM_LIMIT)


def _const_spec(shape):
    nd = len(shape)
    return pl.BlockSpec(shape, lambda *_: (0,) * nd, pipeline_mode=pl.Buffered(1))


def _dot(a, b):
    return jnp.dot(a, b, preferred_element_type=F32)


def _dot_nt(a, b):
    return lax.dot_general(a, b, (((1,), (1,)), ((), ())), preferred_element_type=F32)


def _dot_tn(a, b):
    return lax.dot_general(a, b, (((0,), (0,)), ((), ())), preferred_element_type=F32)


def _silu(x):
    return x * jax.nn.sigmoid(x)


def _layer_norm(r, g, b):
    mu = jnp.mean(r, axis=-1, keepdims=True)
    rc = r - mu
    var = jnp.mean(rc * rc, axis=-1, keepdims=True)
    return rc * lax.rsqrt(var + LN_EPS) * g + b


def _pick_rows(lat_ref, ctx_ref, scr, n_lat):
    is_lat = pl.program_id(0) < n_lat

    @pl.when(is_lat)
    def _():
        scr[...] = lat_ref[...]

    @pl.when(jnp.logical_not(is_lat))
    def _():
        scr[...] = ctx_ref[...]

    return scr[...]


class _Rows:
    def __init__(self, B, L, Lc, tm):
        assert L % tm == 0 and (B * Lc) % tm == 0
        self.B, self.L, self.Lc, self.tm = B, L, Lc, tm
        self.n_lat = B * L // tm
        self.n_all = (B * L + B * Lc) // tm
        self.per_b = L // tm

    def group(self, i):
        return jnp.where(i < self.n_lat, 1 + i // self.per_b, 0)

    def pos_block(self, i):
        return jnp.where(i < self.n_lat, i % self.per_b, self.per_b)

    def pair_specs(self, cols):
        return [pl.BlockSpec((self.tm, cols), lambda i: (jnp.minimum(i, self.n_lat - 1), 0)),
                pl.BlockSpec((self.tm, cols), lambda i: (jnp.maximum(i - self.n_lat, 0), 0))]


def _adaln_kernel(c_ref, w_ref, b_ref, o_ref):
    cond = _silu(c_ref[...])
    o_ref[0] = jnp.dot(cond, w_ref[0], preferred_element_type=F32,
                       precision=lax.Precision.HIGHEST) + b_ref[0]


def _adaln(cond_in, ada_w, ada_b):
    depth, D, N = ada_w.shape
    G = cond_in.shape[0]
    tn = 1536
    return pl.pallas_call(
        _adaln_kernel,
        grid=(depth, N // tn),
        in_specs=[pl.BlockSpec((G, D), lambda l, j: (0, 0)),
                  pl.BlockSpec((1, D, tn), lambda l, j: (l, 0, j)),
                  pl.BlockSpec((1, 1, tn), lambda l, j: (l, 0, j))],
        out_specs=pl.BlockSpec((1, G, tn), lambda l, j: (l, 0, j)),
        out_shape=jax.ShapeDtypeStruct((depth, G, N), F32),
        compiler_params=_cparams(("arbitrary", "arbitrary")),
        name="adaln",
    )(cond_in, ada_w, ada_b.reshape(depth, 1, N))


def _proj_kernel(x_ref, mod_ref, w_ref, *rest, plan, rope, x_pair, n_lat):
    x = x_ref[...]
    if x_pair:
        x = _pick_rows(x_ref, rest[0], rest[-1], n_lat)
        rest = rest[1:-1]
    if rope:
        cos_ref, sin_ref = rest[:2]
        outs = rest[2:]
    else:
        outs = rest
    sh = mod_ref[0, 0:1, :]
    sc = mod_ref[0, 1:2, :]
    a = (x * (1.0 + sc) + sh).astype(BF16)
    for c0, width, kind, oi, oc0, scale, t_out in plan:
        acc = _dot(a, w_ref[:, c0:c0 + width])
        o_ref = outs[oi]
        if kind == "rope":
            cos = cos_ref[...]
            sin = sin_ref[...]
            for h in range(width // (2 * LANES)):
                x1 = acc[:, h * 256:h * 256 + LANES]
                x2 = acc[:, h * 256 + LANES:(h + 1) * 256]
                lo = oc0 + h * 256
                o_ref[:, lo:lo + LANES] = ((x1 * cos - x2 * sin) * scale).astype(o_ref.dtype)
                o_ref[:, lo + LANES:lo + 256] = ((x1 * sin + x2 * cos) * scale).astype(o_ref.dtype)
        else:
            o_ref[:, oc0:oc0 + width] = acc.astype(o_ref.dtype)
        if t_out is not None:
            outs[t_out][...] = acc.T.astype(outs[t_out].dtype)


def _proj(rows, x, mod, w, plan, outs, rope=None):
    x_pair = isinstance(x, tuple)
    D = w.shape[0]
    tm = rows.tm
    R = rows.n_all * tm
    mod_spec = pl.BlockSpec((1, 6, D), lambda i: (rows.group(i), 0, 0))
    if x_pair:
        lat_spec, ctx_spec = rows.pair_specs(D)
        in_specs = [lat_spec, mod_spec, _const_spec(w.shape), ctx_spec]
        args = [x[0], mod, w, x[1]]
    else:
        in_specs = [pl.BlockSpec((tm, D), lambda i: (i, 0)), mod_spec, _const_spec(w.shape)]
        args = [x, mod, w]
    if rope is not None:
        in_specs += [pl.BlockSpec((tm, LANES), lambda i: (rows.pos_block(i), 0))] * 2
        args += list(rope)
    out_specs, out_shape = [], []
    for cols, dtype, transposed in outs:
        if transposed:
            out_specs.append(pl.BlockSpec((cols, tm), lambda i: (0, i)))
            out_shape.append(jax.ShapeDtypeStruct((cols, R), dtype))
        else:
            out_specs.append(pl.BlockSpec((tm, cols), lambda i: (i, 0)))
            out_shape.append(jax.ShapeDtypeStruct((R, cols), dtype))
    return pl.pallas_call(
        functools.partial(_proj_kernel, plan=tuple(plan), rope=rope is not None, x_pair=x_pair,
                          n_lat=rows.n_lat),
        grid=(rows.n_all,),
        in_specs=in_specs,
        out_specs=out_specs,
        out_shape=out_shape,
        scratch_shapes=[pltpu.VMEM((tm, D), F32)] if x_pair else [],
        compiler_params=_cparams(("arbitrary",)),
        name="proj",
    )(*args)


def _post_kernel(x_ref, mod_ref, *rest, scan_heads, has_norm_g, alpha, fc, n_lat, x_pair):
    x = x_ref[...]
    if x_pair:
        x = _pick_rows(x_ref, rest[0], rest[-1], n_lat)
        rest = rest[1:-1]
    if scan_heads:
        of_ref, ob_ref, g_ref = rest[:3]
        rest = rest[3:]
        if has_norm_g:
            ng_ref, rest = rest[0], rest[1:]
        t = of_ref[...].astype(F32) + ob_ref[...].astype(F32)
        dh = t.shape[1] // scan_heads
        parts = []
        for h in range(scan_heads):
            th = t[:, h * dh:(h + 1) * dh]
            yh = th * lax.rsqrt(jnp.mean(th * th, axis=-1, keepdims=True) + RMS_EPS)
            parts.append(yh * ng_ref[...] if has_norm_g else yh)
        mix = (jnp.concatenate(parts, axis=1) * _silu(g_ref[...].astype(F32))).astype(BF16)
    else:
        mix = _pick_rows(rest[0], rest[1], rest[-1], n_lat)
        rest = rest[2:-1]
    wo_ref, ln1g_ref, ln1b_ref, w13_ref, w2_ref, ln2g_ref, ln2b_ref, o_ref = rest
    r = alpha * x + mod_ref[0, 2:3, :] * _dot(mix, wo_ref[...])
    h1 = _layer_norm(r, ln1g_ref[...], ln1b_ref[...])
    a = (h1 * (1.0 + mod_ref[0, 4:5, :]) + mod_ref[0, 3:4, :]).astype(BF16)
    F = w2_ref.shape[0]
    acc = jnp.zeros(h1.shape, F32)
    for c in range(F // fc):
        g = _dot(a, w13_ref[:, c * fc:(c + 1) * fc])
        u = _dot(a, w13_ref[:, F + c * fc:F + (c + 1) * fc])
        acc = acc + _dot((_silu(g) * u).astype(BF16), w2_ref[c * fc:(c + 1) * fc, :])
    r2 = alpha * h1 + mod_ref[0, 5:6, :] * acc
    o_ref[...] = _layer_norm(r2, ln2g_ref[...], ln2b_ref[...])


def _post(rows, x, mod, mix, w_out, ln_g, ln_b, w13, w2, alpha, n_tiles, scan_heads=0, norm_g=None):
    x_pair = isinstance(x, tuple)
    K, D = w_out.shape
    F = w2.shape[0]
    tm = rows.tm
    row = lambda cols, blk=0: pl.BlockSpec((tm, cols), lambda i: (i, blk))
    mod_spec = pl.BlockSpec((1, 6, D), lambda i: (rows.group(i), 0, 0))
    if x_pair:
        lat_spec, ctx_spec = rows.pair_specs(D)
        in_specs = [lat_spec, mod_spec, ctx_spec]
        args = [x[0], mod, x[1]]
    else:
        in_specs = [row(D), mod_spec]
        args = [x, mod]
    scratch = []
    if scan_heads:
        o_f, o_b, (g_arr, g_blk) = mix
        in_specs += [row(K), row(K), row(K, g_blk)]
        args += [o_f, o_b, g_arr]
        if norm_g is not None:
            in_specs.append(_const_spec((1, norm_g.shape[0])))
            args.append(norm_g.reshape(1, -1))
    else:
        in_specs += rows.pair_specs(K)
        args += list(mix)
        scratch = [pltpu.VMEM((tm, K), BF16)]
    vec = lambda v: v.reshape(1, D)
    in_specs += [_const_spec((K, D)), _const_spec((1, D)), _const_spec((1, D)),
                 _const_spec((D, 2 * F)), _const_spec((F, D)), _const_spec((1, D)), _const_spec((1, D))]
    args += [w_out, vec(ln_g[0]), vec(ln_b[0]), w13, w2, vec(ln_g[1]), vec(ln_b[1])]
    return pl.pallas_call(
        functools.partial(_post_kernel, scan_heads=scan_heads, has_norm_g=norm_g is not None,
                          alpha=alpha, fc=256, n_lat=rows.n_lat, x_pair=x_pair),
        grid=(n_tiles,),
        in_specs=in_specs,
        out_specs=row(D),
        out_shape=jax.ShapeDtypeStruct((n_tiles * tm, D), F32),
        scratch_shapes=scratch + ([pltpu.VMEM((tm, D), F32)] if x_pair else []),
        compiler_params=_cparams(("arbitrary",)),
        name="post",
    )(*args)


def _chunk_order(B, L, Lc, C, backward):
    ncc, ncl = Lc // C, L // C

    def blk(b, j):
        if backward:
            in_ctx = j < ncc
            return jnp.where(in_ctx, B * ncl + b * ncc + (ncc - 1 - j), b * ncl + (ncl - 1 - (j - ncc)))
        return jnp.where(j < ncc, B * ncl + b * ncc + j, b * ncl + (j - ncc))

    return blk, ncc + ncl


def _ret_scan_kernel(lg_ref, qf_ref, kf_ref, vf_ref, qb_ref, kb_ref, vb_ref, of_ref, ob_ref, st_scr,
                     *, C, heads):
    j = pl.program_id(1)

    @pl.when(j == 0)
    def _():
        st_scr[...] = jnp.zeros_like(st_scr)

    dk = qf_ref.shape[1] // heads
    dv = vf_ref.shape[1] // heads
    ti = lax.broadcasted_iota(jnp.int32, (C, C), 0)
    si = lax.broadcasted_iota(jnp.int32, (C, C), 1)
    tcol = lax.broadcasted_iota(jnp.int32, (C, 1), 0)
    for d, (q_ref, k_ref, v_ref, o_ref) in enumerate(((qf_ref, kf_ref, vf_ref, of_ref),
                                                      (qb_ref, kb_ref, vb_ref, ob_ref))):
        backward = d == 1
        dist = (si - ti) if backward else (ti - si)
        keep = dist >= 0
        fdist = jnp.maximum(dist, 0).astype(F32)
        eq = ((C - tcol) if backward else (tcol + 1)).astype(F32)
        ek = (tcol if backward else (C - 1 - tcol)).astype(F32)
        for h in range(heads):
            lg = lg_ref[d, h]
            q = q_ref[:, h * dk:(h + 1) * dk]
            k = k_ref[:, h * dk:(h + 1) * dk]
            v = v_ref[:, h * dv:(h + 1) * dv]
            decay = jnp.where(keep, jnp.exp(lg * fdist), 0.0)
            att = (_dot_nt(q, k) * decay).astype(BF16)
            st = st_scr[d, h]
            o = _dot(att, v) + _dot_nt(q, st.astype(BF16)) * jnp.exp(lg * eq)
            kd = (k.astype(F32) * jnp.exp(lg * ek)).astype(BF16)
            st_scr[d, h] = st * jnp.exp(lg * jnp.full((1, 1), float(C), F32)) + _dot_tn(v, kd)
            o_ref[:, h * dv:(h + 1) * dv] = o.astype(o_ref.dtype)


def _ret_scan(proj, lg, B, L, Lc):
    R = proj.shape[0]
    C = RET_CHUNK
    H, dk, dv = RET_HEADS, 256, 512
    blk_f, nch = _chunk_order(B, L, Lc, C, False)
    blk_b, _ = _chunk_order(B, L, Lc, C, True)
    specs = lambda blk: [pl.BlockSpec((C, H * dk), lambda b, j: (blk(b, j), 0)),
                         pl.BlockSpec((C, H * dk), lambda b, j: (blk(b, j), 1)),
                         pl.BlockSpec((C, H * dv), lambda b, j: (blk(b, j), 1))]
    return pl.pallas_call(
        functools.partial(_ret_scan_kernel, C=C, heads=H),
        grid=(B, nch),
        in_specs=[pl.BlockSpec(memory_space=pltpu.SMEM)] + specs(blk_f) + specs(blk_b),
        out_specs=[pl.BlockSpec((C, H * dv), lambda b, j: (blk_f(b, j), 0)),
                   pl.BlockSpec((C, H * dv), lambda b, j: (blk_b(b, j), 0))],
        out_shape=[jax.ShapeDtypeStruct((R, H * dv), BF16)] * 2,
        scratch_shapes=[pltpu.VMEM((2, H, dv, dk), F32)],
        compiler_params=_cparams(("arbitrary", "arbitrary")),
        name="ret_scan",
    )(lg, proj, proj, proj, proj, proj, proj)


def _cumsum_rows(x):
    n = x.shape[0]
    r = lax.broadcasted_iota(jnp.int32, (n, 1), 0)
    s = 1
    while s < n:
        x = x + jnp.where(r >= s, pltpu.roll(x, s, axis=0), 0.0)
        s *= 2
    return x


def _hg_scan_kernel(qf_ref, ff_ref, vf_ref, qb_ref, fb_ref, vb_ref, lb_ref, of_ref, ob_ref, st_scr,
                    *, C, sub, heads, scale):
    j = pl.program_id(1)

    @pl.when(j == 0)
    def _():
        st_scr[...] = jnp.zeros_like(st_scr)

    lb = lb_ref[...]
    ti = lax.broadcasted_iota(jnp.int32, (C, C), 0)
    si = lax.broadcasted_iota(jnp.int32, (C, C), 1)
    d = HG_EXPAND
    for dr, (q_ref, f_ref, v_ref, o_ref) in enumerate(((qf_ref, ff_ref, vf_ref, of_ref),
                                                       (qb_ref, fb_ref, vb_ref, ob_ref))):
        backward = dr == 1
        keep = (si >= ti) if backward else (ti >= si)
        for c in (range(sub - 1, -1, -1) if backward else range(sub)):
            rs = slice(c * C, (c + 1) * C)
            forget = lb + (1.0 - lb) * jax.nn.sigmoid(f_ref[rs, :])
            kk = 1.0 - forget
            gl = jnp.log(forget)
            pre = _cumsum_rows(gl)
            tot = pre[C - 1:C, :]
            bc = (tot - pre + gl) if backward else pre
            qd = (_silu(q_ref[rs, :].astype(F32)) * scale * jnp.exp(bc)).astype(BF16)
            kd = (kk * jnp.exp(-bc)).astype(BF16)
            ke = (kk * jnp.exp(tot - bc)).astype(BF16)
            v = v_ref[rs, :]
            etot = jnp.exp(tot)
            for h in range(heads):
                sl = slice(h * d, (h + 1) * d)
                att = jnp.where(keep, _dot_nt(qd[:, sl], kd[:, sl]), 0.0).astype(BF16)
                st = st_scr[dr, h]
                o = _dot(att, v[:, sl]) + _dot_nt(qd[:, sl], st.astype(BF16))
                st_scr[dr, h] = st * etot[:, sl] + _dot_tn(v[:, sl], ke[:, sl])
                o_ref[rs, sl] = o.astype(o_ref.dtype)


def _hg_scan(qig, ff, lb, B, L, Lc):
    R = qig.shape[0]
    sub = 2
    C = HG_CHUNK * sub
    Dm = qig.shape[1] // 3
    heads = Dm // HG_EXPAND
    blk_f, nch = _chunk_order(B, L, Lc, C, False)
    blk_b, _ = _chunk_order(B, L, Lc, C, True)
    specs = lambda blk, fcol: [pl.BlockSpec((C, Dm), lambda b, j: (blk(b, j), 0)),
                               pl.BlockSpec((C, Dm), lambda b, j: (blk(b, j), fcol)),
                               pl.BlockSpec((C, Dm), lambda b, j: (blk(b, j), 1))]
    return pl.pallas_call(
        functools.partial(_hg_scan_kernel, C=HG_CHUNK, sub=sub, heads=heads, scale=HG_EXPAND ** -0.5),
        grid=(B, nch),
        in_specs=specs(blk_f, 0) + specs(blk_b, 1) + [pl.BlockSpec((1, Dm), lambda b, j: (0, 0))],
        out_specs=[pl.BlockSpec((C, Dm), lambda b, j: (blk_f(b, j), 0)),
                   pl.BlockSpec((C, Dm), lambda b, j: (blk_b(b, j), 0))],
        out_shape=[jax.ShapeDtypeStruct((R, Dm), BF16)] * 2,
        scratch_shapes=[pltpu.VMEM((2, heads, HG_EXPAND, HG_EXPAND), F32)],
        compiler_params=_cparams(("arbitrary", "arbitrary")),
        name="hg_scan",
    )(qig, ff, qig, qig, ff, qig, lb.reshape(1, Dm))


def _softmax_pv(scores, values):
    m = functools.reduce(jnp.maximum, [jnp.max(s, axis=-1, keepdims=True) for s in scores])
    ps = [jnp.exp(s - m) for s in scores]
    l = functools.reduce(lambda a, b: a + b, [jnp.sum(p, axis=-1, keepdims=True) for p in ps])
    o = functools.reduce(lambda a, b: a + b, [_dot(p.astype(BF16), v) for p, v in zip(ps, values)])
    return o / l


def _lane_lo(shape):
    return lax.broadcasted_iota(jnp.int32, shape, 1) < (LANES // 2)


NA_FRAME_ROWS = NA_WIN_ROWS + 2
LOG2E = float(np.log2(np.e))
NA_Q_SCALE = 0.125 * LOG2E


def _na_kernel(q_ref, kw_ref, vtw_ref, kc_ref, vtc_ref, bias_ref, o_ref, sla_ref, slb_ref, sca_ref, scb_ref,
               *, heads, nrows):
    W, wr, fr = GRID_W, NA_WIN_ROWS, NA_FRAME_ROWS
    r0 = 2 * pl.program_id(1)
    u = jnp.minimum(jnp.clip(r0 - wr // 2, 0, nrows - wr), nrows - fr)
    tile = []
    for j in range(fr):
        per = []
        for rho in range(2):
            r = r0 + rho
            rs = jnp.clip(r - wr // 2, 0, nrows - wr)
            ok = (u + j >= rs) & (u + j < rs + wr)
            per.append(jnp.where(ok, u + j - r + wr - 1, 2 * wr - 1))
        tile.append(per)
    lo = _lane_lo((2 * W, LANES))
    ones_l = jnp.ones((8, fr * W), BF16)
    ones_c = jnp.ones((8, kc_ref.shape[0]), BF16)
    ri = lax.broadcasted_iota(jnp.int32, (LANES, 4 * W), 0)
    li = lax.broadcasted_iota(jnp.int32, (LANES, 4 * W), 1)
    own_head = (ri >= W) == ((li % LANES) >= W)
    npairs = heads // 2
    lanes = lambda hp: pl.ds(pl.multiple_of(hp * LANES, LANES), LANES)

    def put(bufs, hp):
        sl_buf, sc_buf = bufs
        q2 = q_ref[:, lanes(hp)] * NA_Q_SCALE
        qlo = jnp.where(lo, q2, jnp.zeros_like(q2))
        qhi = jnp.where(lo, jnp.zeros_like(q2), q2)
        qblk = jnp.concatenate([qlo[:W], qhi[:W], qlo[W:], qhi[W:]], axis=0)
        s_raw = _dot_nt(kw_ref[:, lanes(hp)], qblk)
        m = None
        for j in range(fr):
            sj = (s_raw[j * W:(j + 1) * W]
                  + jnp.concatenate([bias_ref[hp, tile[j][0]], bias_ref[hp, tile[j][1]]], axis=1))
            sl_buf[j * W:(j + 1) * W, :] = sj
            mj = jnp.max(sj, axis=0, keepdims=True)
            m = mj if m is None else jnp.maximum(m, mj)
        s_c = _dot_nt(kc_ref[:, lanes(hp)], qblk)
        sc_buf[...] = s_c
        return jnp.maximum(m, jnp.max(s_c, axis=0, keepdims=True))

    def fin(bufs, hp, m):
        sl_buf, sc_buf = bufs
        p_l = jnp.exp2(sl_buf[...] - m).astype(BF16)
        p_c = jnp.exp2(sc_buf[...] - m).astype(BF16)
        rows = pl.ds(pl.multiple_of(hp * LANES, LANES), LANES)
        acc = (_dot(jnp.concatenate([vtw_ref[rows, :], ones_l], axis=0), p_l)
               + _dot(jnp.concatenate([vtc_ref[rows, :], ones_c], axis=0), p_c))
        ot = jnp.where(own_head, acc[:LANES] / acc[LANES:LANES + 1], 0.0)
        tr = ot.T
        for rho in range(2):
            blk = tr[rho * LANES:rho * LANES + W] + tr[rho * LANES + W:(rho + 1) * LANES]
            o_ref[rho * W:(rho + 1) * W, lanes(hp)] = blk.astype(o_ref.dtype)

    buf_a, buf_b = (sla_ref, sca_ref), (slb_ref, scb_ref)
    m0 = put(buf_a, 0)

    def body(i, m_a):
        m_b = put(buf_b, 2 * i + 1)
        fin(buf_a, 2 * i, m_a)
        m_next = put(buf_a, 2 * i + 2)
        fin(buf_b, 2 * i + 1, m_b)
        return m_next

    m_a = lax.fori_loop(0, npairs // 2 - 1, body, m0)
    m_b = put(buf_b, npairs - 1)
    fin(buf_a, npairs - 2, m_a)
    fin(buf_b, npairs - 1, m_b)


def _na_bias_table(rpb):
    H = rpb.shape[0]
    W, wr, wc = GRID_W, NA_WIN_ROWS, NA_WIN_COLS
    qcol = np.arange(W)[:, None]
    kcol = np.arange(W)[None, :]
    ws = np.clip(qcol - wc // 2, 0, W - wc)
    ok = (kcol >= ws) & (kcol < ws + wc)
    r_pad = jnp.pad(rpb.astype(F32), ((0, 0), (0, 0), (W - wc, W + wc - (2 * wc - 1))))
    skew = jnp.tile(r_pad, (1, 1, W))[:, :, :W * (2 * W - 1)].reshape(H, 2 * wr - 1, W, 2 * W - 1)
    t15 = jnp.where(ok[None, None], skew[:, :, :, W - 1:] * LOG2E, NEG_INF)
    t = t15.reshape(H // 2, 2, 2 * wr - 1, W, W).transpose(0, 2, 4, 1, 3).reshape(H // 2, 2 * wr - 1, W, 2 * W)
    return jnp.concatenate([t, jnp.full((H // 2, 1, W, 2 * W), NEG_INF, F32)], axis=1)


def _na_attention(qkv, vt, bias, B, L, Lc):
    D = qkv.shape[1] // 3
    W, wr, fr = GRID_W, NA_WIN_ROWS, NA_FRAME_ROWS
    nrows = L // W
    assert nrows % 2 == 0 and nrows >= fr
    frame0 = lambda g: jnp.minimum(jnp.clip(2 * g - wr // 2, 0, nrows - wr), nrows - fr)
    ctx_blk0 = B * L // Lc
    return pl.pallas_call(
        functools.partial(_na_kernel, heads=NA_HEADS, nrows=nrows),
        grid=(B, nrows // 2),
        in_specs=[pl.BlockSpec((2 * W, D), lambda b, g: (b * (nrows // 2) + g, 0)),
                  pl.BlockSpec((pl.Element(fr * W), pl.Element(D)),
                               lambda b, g: ((b * nrows + frame0(g)) * W, D)),
                  pl.BlockSpec((pl.Element(D), pl.Element(fr * W)),
                               lambda b, g: (0, pl.multiple_of((b * nrows + frame0(g)) * W, 2 * W))),
                  pl.BlockSpec((Lc, D), lambda b, g: (ctx_blk0 + b, 1)),
                  pl.BlockSpec((D, Lc), lambda b, g: (0, ctx_blk0 + b)),
                  _const_spec(bias.shape)],
        out_specs=pl.BlockSpec((2 * W, D), lambda b, g: (b * (nrows // 2) + g, 0)),
        out_shape=jax.ShapeDtypeStruct((B * L, D), BF16),
        scratch_shapes=[pltpu.VMEM((fr * W, 4 * W), F32)] * 2 + [pltpu.VMEM((Lc, 4 * W), F32)] * 2,
        compiler_params=_cparams(("arbitrary", "arbitrary")),
        name="na_attn",
    )(qkv, qkv, vt, qkv, vt, bias)


def _ctx_attn_kernel(q_ref, k_ref, v_ref, o_ref, *, heads, split64, q_scale):
    lo = _lane_lo((q_ref.shape[0], LANES))
    for hp in range(heads // 2):
        vsl = slice(hp * LANES, (hp + 1) * LANES)
        v2 = v_ref[:, vsl]
        outs = []
        for e in range(2):
            if split64:
                q2 = q_ref[:, vsl] * q_scale
                qh = jnp.where(lo if e == 0 else ~lo, q2, jnp.zeros_like(q2))
                kh = k_ref[:, vsl]
            else:
                hsl = slice((2 * hp + e) * LANES, (2 * hp + e + 1) * LANES)
                qh, kh = q_ref[:, hsl], k_ref[:, hsl]
            outs.append(_softmax_pv([_dot_nt(qh, kh)], [v2]))
        o_ref[:, vsl] = jnp.where(lo, outs[0], outs[1]).astype(o_ref.dtype)


def _ctx_attention(q_src, k_src, v_src, B, L, Lc, heads, split64, q_scale=1.0):
    blk0 = B * L // Lc
    spec = lambda src: pl.BlockSpec((Lc, src[1]), lambda b: (blk0 + b, src[2]))
    Dv = v_src[1]
    return pl.pallas_call(
        functools.partial(_ctx_attn_kernel, heads=heads, split64=split64, q_scale=q_scale),
        grid=(B,),
        in_specs=[spec(q_src), spec(k_src), spec(v_src)],
        out_specs=pl.BlockSpec((Lc, Dv), lambda b: (b, 0)),
        out_shape=jax.ShapeDtypeStruct((B * Lc, Dv), BF16),
        compiler_params=_cparams(("arbitrary",)),
        name="ctx_attn",
    )(q_src[0], k_src[0], v_src[0])


def _rope_slot(x, cos, sina, sinb):
    return x * cos + pltpu.roll(x, LANES - 16, axis=1) * sina + pltpu.roll(x, 16, axis=1) * sinb


def _mla_proj_kernel(x_ref, mod_ref, wd_ref, qn_ref, kvn_ref, wq_ref, wk_ref, wvt_ref, vone_ref,
                     cos_ref, sina_ref, sinb_ref, q_ref, k_ref, vt_ref, *, heads, scale):
    sh = mod_ref[0, 0:1, :]
    sc = mod_ref[0, 1:2, :]
    a = (x_ref[...] * (1.0 + sc) + sh).astype(BF16)
    d = _dot(a, wd_ref[...])
    cq = d[:, :MLA_Q_LORA]
    ckv = d[:, MLA_Q_LORA:MLA_Q_LORA + MLA_KV_LORA]
    kr = d[:, MLA_Q_LORA + MLA_KV_LORA:]
    rms = lambda t, g: (t * lax.rsqrt(jnp.mean(t * t, axis=-1, keepdims=True) + RMS_EPS) * g).astype(BF16)
    cqn = rms(cq, qn_ref[...])
    ckvn = rms(ckv, kvn_ref[...])
    cos, sina, sinb = cos_ref[...], sina_ref[...], sinb_ref[...]
    krr = _rope_slot(kr, cos, sina, sinb)
    qf = _dot(cqn, wq_ref[...])
    kf = _dot(ckvn, wk_ref[...])
    for h in range(heads):
        sl = slice(h * LANES, (h + 1) * LANES)
        q_ref[:, sl] = (_rope_slot(qf[:, sl], cos, sina, sinb) * scale).astype(q_ref.dtype)
        k_ref[:, sl] = (kf[:, sl] + krr).astype(k_ref.dtype)
    vt_ref[...] = (_dot_nt(wvt_ref[...], ckvn) + vone_ref[...]).astype(vt_ref.dtype)


def _mla_proj(rows, x, mod, wd, qn, kvn, wq, wk, wvt, vone, tables):
    R, D = x.shape
    tm = rows.tm
    H = MLA_HEADS
    tab_spec = pl.BlockSpec((tm, LANES), lambda i: (rows.pos_block(i), 0))
    return pl.pallas_call(
        functools.partial(_mla_proj_kernel, heads=H,
                          scale=(MLA_NOPE + MLA_ROPE) ** -0.5 * float(np.log2(np.e))),
        grid=(rows.n_all,),
        in_specs=[pl.BlockSpec((tm, D), lambda i: (i, 0)),
                  pl.BlockSpec((1, 6, D), lambda i: (rows.group(i), 0, 0)),
                  _const_spec(wd.shape), _const_spec((1, MLA_Q_LORA)), _const_spec((1, MLA_KV_LORA)),
                  _const_spec(wq.shape), _const_spec(wk.shape), _const_spec(wvt.shape),
                  _const_spec(vone.shape), tab_spec, tab_spec, tab_spec],
        out_specs=[pl.BlockSpec((tm, H * LANES), lambda i: (i, 0)),
                   pl.BlockSpec((tm, H * LANES), lambda i: (i, 0)),
                   pl.BlockSpec((H * LANES, tm), lambda i: (0, i))],
        out_shape=[jax.ShapeDtypeStruct((R, H * LANES), BF16),
                   jax.ShapeDtypeStruct((R, H * LANES), BF16),
                   jax.ShapeDtypeStruct((H * LANES, R), BF16)],
        compiler_params=_cparams(("arbitrary",)),
        name="mla_proj",
    )(x, mod, wd, qn.reshape(1, -1), kvn.reshape(1, -1), wq, wk, wvt, vone, *tables)


def _mla_flash_kernel(q_ref, kc_ref, vtc_ref, *rest, tk, cpi, with_latent):
    if with_latent:
        k_ref, vt_ref, o_ref, sa_ref, sb_ref = rest
    else:
        (o_ref,) = rest
    tq = q_ref.shape[0]
    hsl = [slice(e * LANES, (e + 1) * LANES) for e in range(2)]
    qs = [q_ref[:, sl] for sl in hsl]

    def scores(kblk):
        return [_dot_nt(kblk[:, hsl[e]], qs[e]) for e in range(2)]

    vrows = MLA_V + 8

    def colmax(ss):
        return [jnp.max(s, axis=0, keepdims=True) for s in ss]

    def update(ss, cms, vtblk, state):
        out = []
        for e in range(2):
            m, acc = state[e]
            m_new = jnp.maximum(m, cms[e])
            p = jnp.exp2(ss[e] - m_new).astype(BF16)
            acc = jnp.exp2(m - m_new) * acc + _dot(vtblk[e * LANES:e * LANES + vrows, :], p)
            out.append((m_new, acc))
        return tuple(out)

    init = (jnp.full((1, tq), NEG_INF, F32), jnp.zeros((vrows, tq), F32))
    s_ctx = scores(kc_ref[...])
    if not with_latent:
        state = update(s_ctx, colmax(s_ctx), vtc_ref[...], (init, init))
    else:
        n = k_ref.shape[0] // tk
        assert cpi % 2 == 0 and n % cpi == 0
        bufs = (sa_ref, sb_ref)

        def put(buf, c):
            ss = scores(k_ref[pl.ds(pl.multiple_of(c * tk, tk), tk), :])
            for e in range(2):
                buf[e] = ss[e]
            return tuple(colmax(ss))

        def group(c0, carry, last):
            st, cms = carry
            for t in range(cpi):
                nxt = None
                if not (last and t == cpi - 1):
                    nxt = put(bufs[(t + 1) % 2], c0 + t + 1)
                buf = bufs[t % 2]
                vtblk = vt_ref[:, pl.ds(pl.multiple_of((c0 + t) * tk, tk), tk)]
                st = update([buf[0], buf[1]], cms, vtblk, st)
                cms = nxt
            return st, cms

        cms = put(bufs[0], 0)
        state = update(s_ctx, colmax(s_ctx), vtc_ref[...], (init, init))
        carry = lax.fori_loop(0, n // cpi - 1, lambda i, cr: group(i * cpi, cr, False), (state, cms))
        state, _ = group(n - cpi, carry, True)
    ot = jnp.concatenate([acc[:MLA_V] / acc[MLA_V:MLA_V + 1] for _, acc in state], axis=0)
    o_ref[...] = ot.T.astype(o_ref.dtype)


def _mla_attention(q, k, vt, B, L, Lc, tq, tk, latent_queries):
    H = MLA_HEADS
    ctx_blk0 = B * L // Lc
    if latent_queries:
        nq, q0, n_out = L // tq, 0, B * L
    else:
        assert tq == Lc
        nq, q0, n_out = 1, ctx_blk0, B * Lc
    in_specs = [pl.BlockSpec((tq, 2 * LANES), lambda b, hp, i: (q0 + b * nq + i, hp)),
                pl.BlockSpec((Lc, 2 * LANES), lambda b, hp, i: (ctx_blk0 + b, hp)),
                pl.BlockSpec((2 * LANES, Lc), lambda b, hp, i: (hp, ctx_blk0 + b))]
    args = [q, k, vt]
    scratch = []
    if latent_queries:
        in_specs += [pl.BlockSpec((L, 2 * LANES), lambda b, hp, i: (b, hp)),
                     pl.BlockSpec((2 * LANES, L), lambda b, hp, i: (hp, b))]
        args += [k, vt]
        scratch = [pltpu.VMEM((2, tk, tq), F32)] * 2
    return pl.pallas_call(
        functools.partial(_mla_flash_kernel, tk=tk, cpi=4 if (L // tk) % 4 == 0 else 2,
                          with_latent=latent_queries),
        grid=(B, H // 2, nq),
        in_specs=in_specs,
        out_specs=pl.BlockSpec((tq, LANES), lambda b, hp, i: (b * nq + i, hp)),
        out_shape=jax.ShapeDtypeStruct((n_out, H * MLA_V), BF16),
        scratch_shapes=scratch,
        compiler_params=_cparams(("arbitrary", "arbitrary", "arbitrary")),
        name="mla_flash" if latent_queries else "mla_flash_ctx",
    )(*args)


def _axial_angles(L, rot_dim):
    t = jnp.arange(L)
    rows = (t // GRID_W).astype(F32)
    cols = (t % GRID_W).astype(F32)
    n_freq = rot_dim // 4
    inv = ROPE_BASE ** (-jnp.arange(n_freq, dtype=F32) / n_freq)
    return jnp.concatenate([rows[:, None] * inv, cols[:, None] * inv], -1)


def _ret_rope_tables(L, tm):
    ang = _axial_angles(L, 256)
    cos = jnp.concatenate([jnp.cos(ang), jnp.ones((tm, LANES), F32)], 0)
    sin = jnp.concatenate([jnp.sin(ang), jnp.zeros((tm, LANES), F32)], 0)
    return cos, sin


def _mla_rope_tables(L, tm):
    ang = _axial_angles(L, MLA_ROPE)
    c, s = jnp.cos(ang), jnp.sin(ang)
    one = jnp.ones((L, MLA_NOPE), F32)
    z16 = jnp.zeros((L, 16), F32)
    z32 = jnp.zeros((L, 32), F32)
    z64 = jnp.zeros((L, MLA_NOPE), F32)
    cos = jnp.concatenate([one, c, c, jnp.ones((L, 32), F32)], -1)
    sina = jnp.concatenate([z64, -s, z16, z32], -1)
    sinb = jnp.concatenate([z64, z16, s, z32], -1)
    ident = lambda t, fill: jnp.concatenate([t, jnp.full((tm, LANES), fill, F32)], 0)
    return ident(cos, 1.0), ident(sina, 0.0), ident(sinb, 0.0)


def kernel(x, c, ctx, c_ctx, ada_w, ada_b, ln_g, ln_b, ffn_w13, ffn_w2, ret_w_in, ret_decay, ret_w_out, na_w_qkv, na_rpb, na_w_out, mla_w_down, mla_q_norm, mla_kv_norm, mla_w_uq, mla_w_ukv, mla_w_out, hg_w_in, hg_lower_bounds, hg_norm_g, hg_w_out):
    B, L, D = x.shape
    Lc = ctx.shape[1]
    depth = ada_w.shape[0]
    alpha = (2 * depth) ** 0.25
    tm = 512 if (B * Lc) % 512 == 0 else 256
    rows = _Rows(B, L, Lc, tm)
    n_lat_rows = B * L

    h = (x.reshape(B * L, D), ctx.reshape(B * Lc, D))

    G = 8 * (-(-(B + 1) // 8))
    cond_in = jnp.zeros((G, D), F32).at[0].set(c_ctx).at[1:B + 1].set(c)
    mods = _adaln(cond_in, ada_w, ada_b).reshape(depth, G, 6, D)

    bf = lambda w: w.astype(BF16)
    H = MLA_HEADS
    wd = jnp.zeros((D, MLA_Q_LORA + MLA_KV_LORA + LANES), F32)
    wd = wd.at[:, :MLA_Q_LORA + MLA_KV_LORA].set(mla_w_down[:, :MLA_Q_LORA + MLA_KV_LORA])
    wd = wd.at[:, MLA_Q_LORA + MLA_KV_LORA + MLA_NOPE:MLA_Q_LORA + MLA_KV_LORA + MLA_NOPE + MLA_ROPE].set(
        mla_w_down[:, MLA_Q_LORA + MLA_KV_LORA:])
    wq = jnp.pad(mla_w_uq.reshape(MLA_Q_LORA, H, MLA_NOPE + MLA_ROPE),
                 ((0, 0), (0, 0), (0, LANES - MLA_NOPE - MLA_ROPE))).reshape(MLA_Q_LORA, H * LANES)
    wukv = mla_w_ukv.reshape(MLA_KV_LORA, H, MLA_NOPE + MLA_V)
    wk = jnp.pad(wukv[:, :, :MLA_NOPE], ((0, 0), (0, 0), (0, LANES - MLA_NOPE))).reshape(MLA_KV_LORA, H * LANES)
    wvt = jnp.pad(wukv[:, :, MLA_NOPE:], ((0, 0), (0, 0), (0, LANES - MLA_V))).reshape(MLA_KV_LORA, H * LANES).T
    vone = jnp.tile((jnp.arange(LANES) == MLA_V).astype(F32), H).reshape(H * LANES, 1)

    plain = lambda n, width=1024: [(c0, width, "plain", 0, c0, 1.0, None) for c0 in range(0, n, width)]
    for i in range(depth):
        mod = mods[i]
        kind = i % 4
        n_tiles = rows.n_all if i < depth - 1 else rows.n_lat
        post = functools.partial(_post, rows, h, mod, ln_g=ln_g[i], ln_b=ln_b[i], w13=bf(ffn_w13[i]),
                                 w2=bf(ffn_w2[i]), alpha=alpha, n_tiles=n_tiles)
        if kind == 0:
            lg = -jnp.exp(ret_decay.astype(F32))
            plan = [(0, 1024, "rope", 0, 0, 1.0, None), (1024, 1024, "rope", 0, 1024, 256 ** -0.5, None)]
            plan += plain(6144)[2:]
            (proj,) = _proj(rows, h, mod, bf(ret_w_in), plan, [(6144, BF16, False)],
                            rope=_ret_rope_tables(L, tm))
            o_f, o_b = _ret_scan(proj, lg, B, L, Lc)
            h = post(mix=(o_f, o_b, (proj, 2)), w_out=bf(ret_w_out), scan_heads=RET_HEADS)
        elif kind == 1:
            plan = plain(2048) + [(2048, 1024, "plain", 0, 2048, 1.0, 1)]
            qkv, vt = _proj(rows, h, mod, bf(na_w_qkv), plan, [(3072, BF16, False), (1024, BF16, True)])
            o_lat = _na_attention(qkv, vt, _na_bias_table(na_rpb), B, L, Lc)
            o_ctx = _ctx_attention((qkv, D, 0), (qkv, D, 1), (qkv, D, 2), B, L, Lc, NA_HEADS,
                                   split64=True, q_scale=0.125)
            h = post(mix=(o_lat, o_ctx), w_out=bf(na_w_out))
        elif kind == 2:
            q, k, vt = _mla_proj(rows, h, mod, bf(wd), mla_q_norm, mla_kv_norm, bf(wq), bf(wk), bf(wvt), vone,
                                 _mla_rope_tables(L, tm))
            o_lat = _mla_attention(q, k, vt, B, L, Lc, 512, 256, latent_queries=True)
            o_ctx = _mla_attention(q, k, vt, B, L, Lc, Lc, 256, latent_queries=False)
            h = post(mix=(o_lat, o_ctx), w_out=bf(mla_w_out))
        else:
            lb_soft = jax.nn.softmax(hg_lower_bounds.astype(F32), axis=0)
            lb = (jnp.cumsum(lb_soft, axis=0) - lb_soft[0])[i]
            plan = [(0, 1024, "plain", 0, 0, 1.0, None), (1024, 1024, "plain", 1, 0, 1.0, None),
                    (2048, 1024, "plain", 1, 1024, 1.0, None), (3072, 1024, "plain", 0, 1024, 1.0, None),
                    (4096, 1024, "plain", 0, 2048, 1.0, None)]
            qig, ff = _proj(rows, h, mod, bf(hg_w_in), plan, [(3072, BF16, False), (2048, F32, False)])
            o_f, o_b = _hg_scan(qig, ff, lb, B, L, Lc)
            h = post(mix=(o_f, o_b, (qig, 2)), w_out=bf(hg_w_out), scan_heads=D // HG_EXPAND,
                     norm_g=hg_norm_g)
    return h[:n_lat_rows].reshape(B, L, D)
```

```python
import functools

import numpy as np
import jax
import jax.numpy as jnp
from jax import lax
from jax.experimental import pallas as pl
from jax.experimental.pallas import tpu as pltpu

F32 = jnp.float32
BF16 = jnp.bfloat16

GRID_W = 64
LN_EPS = 1e-5
RMS_EPS = 1e-6
ROPE_BASE = 10000.0
NEG_INF = -1e30

RET_HEADS = 4
NA_HEADS = 16
NA_WIN_ROWS = 8
NA_WIN_COLS = 16
MLA_HEADS = 16
MLA_NOPE = 64
MLA_ROPE = 32
MLA_V = 64
MLA_Q_LORA = 512
MLA_KV_LORA = 256
HG_EXPAND = 128
HG_CHUNK = 64
RET_CHUNK = 256

VMEM_LIMIT = 56 * 1024 * 1024
LANES = 128


def _cparams(sem):
    return pltpu.CompilerParams(dimension_semantics=sem, vmem_limit_bytes=VMEM_LIMIT)


def _const_spec(shape):
    nd = len(shape)
    return pl.BlockSpec(shape, lambda *_: (0,) * nd, pipeline_mode=pl.Buffered(1))


def _dot(a, b):
    return jnp.dot(a, b, preferred_element_type=F32)


def _dot_nt(a, b):
    return lax.dot_general(a, b, (((1,), (1,)), ((), ())), preferred_element_type=F32)


def _dot_tn(a, b):
    return lax.dot_general(a, b, (((0,), (0,)), ((), ())), preferred_element_type=F32)


def _silu(x):
    return x * jax.nn.sigmoid(x)


def _layer_norm(r, g, b):
    mu = jnp.mean(r, axis=-1, keepdims=True)
    rc = r - mu
    var = jnp.mean(rc * rc, axis=-1, keepdims=True)
    return rc * lax.rsqrt(var + LN_EPS) * g + b


def _pick_rows(lat_ref, ctx_ref, scr, n_lat):
    is_lat = pl.program_id(0) < n_lat

    @pl.when(is_lat)
    def _():
        scr[...] = lat_ref[...]

    @pl.when(jnp.logical_not(is_lat))
    def _():
        scr[...] = ctx_ref[...]

    return scr[...]


class _Rows:
    def __init__(self, B, L, Lc, tm):
        assert L % tm == 0 and (B * Lc) % tm == 0
        self.B, self.L, self.Lc, self.tm = B, L, Lc, tm
        self.n_lat = B * L // tm
        self.n_all = (B * L + B * Lc) // tm
        self.per_b = L // tm

    def group(self, i):
        return jnp.where(i < self.n_lat, 1 + i // self.per_b, 0)

    def pos_block(self, i):
        return jnp.where(i < self.n_lat, i % self.per_b, self.per_b)

    def pair_specs(self, cols):
        return [pl.BlockSpec((self.tm, cols), lambda i: (jnp.minimum(i, self.n_lat - 1), 0)),
                pl.BlockSpec((self.tm, cols), lambda i: (jnp.maximum(i - self.n_lat, 0), 0))]


def _adaln_kernel(c_ref, w_ref, b_ref, o_ref):
    cond = _silu(c_ref[...])
    o_ref[0] = jnp.dot(cond, w_ref[0], preferred_element_type=F32,
                       precision=lax.Precision.HIGHEST) + b_ref[0]


def _adaln(cond_in, ada_w, ada_b):
    depth, D, N = ada_w.shape
    G = cond_in.shape[0]
    tn = 1536
    return pl.pallas_call(
        _adaln_kernel,
        grid=(depth, N // tn),
        in_specs=[pl.BlockSpec((G, D), lambda l, j: (0, 0)),
                  pl.BlockSpec((1, D, tn), lambda l, j: (l, 0, j)),
                  pl.BlockSpec((1, 1, tn), lambda l, j: (l, 0, j))],
        out_specs=pl.BlockSpec((1, G, tn), lambda l, j: (l, 0, j)),
        out_shape=jax.ShapeDtypeStruct((depth, G, N), F32),
        compiler_params=_cparams(("arbitrary", "arbitrary")),
        name="adaln",
    )(cond_in, ada_w, ada_b.reshape(depth, 1, N))


def _proj_kernel(x_ref, mod_ref, w_ref, *rest, plan, rope, x_pair, n_lat):
    x = x_ref[...]
    if x_pair:
        x = _pick_rows(x_ref, rest[0], rest[-1], n_lat)
        rest = rest[1:-1]
    if rope:
        cos_ref, sin_ref = rest[:2]
        outs = rest[2:]
    else:
        outs = rest
    sh = mod_ref[0, 0:1, :]
    sc = mod_ref[0, 1:2, :]
    a = (x * (1.0 + sc) + sh).astype(BF16)
    for c0, width, kind, oi, oc0, scale, t_out in plan:
        acc = _dot(a, w_ref[:, c0:c0 + width])
        o_ref = outs[oi]
        if kind == "rope":
            cos = cos_ref[...]
            sin = sin_ref[...]
            for h in range(width // (2 * LANES)):
                x1 = acc[:, h * 256:h * 256 + LANES]
                x2 = acc[:, h * 256 + LANES:(h + 1) * 256]
                lo = oc0 + h * 256
                o_ref[:, lo:lo + LANES] = ((x1 * cos - x2 * sin) * scale).astype(o_ref.dtype)
                o_ref[:, lo + LANES:lo + 256] = ((x1 * sin + x2 * cos) * scale).astype(o_ref.dtype)
        else:
            o_ref[:, oc0:oc0 + width] = acc.astype(o_ref.dtype)
        if t_out is not None:
            outs[t_out][...] = acc.T.astype(outs[t_out].dtype)


def _proj(rows, x, mod, w, plan, outs, rope=None):
    x_pair = isinstance(x, tuple)
    D = w.shape[0]
    tm = rows.tm
    R = rows.n_all * tm
    mod_spec = pl.BlockSpec((1, 6, D), lambda i: (rows.group(i), 0, 0))
    if x_pair:
        lat_spec, ctx_spec = rows.pair_specs(D)
        in_specs = [lat_spec, mod_spec, _const_spec(w.shape), ctx_spec]
        args = [x[0], mod, w, x[1]]
    else:
        in_specs = [pl.BlockSpec((tm, D), lambda i: (i, 0)), mod_spec, _const_spec(w.shape)]
        args = [x, mod, w]
    if rope is not None:
        in_specs += [pl.BlockSpec((tm, LANES), lambda i: (rows.pos_block(i), 0))] * 2
        args += list(rope)
    out_specs, out_shape = [], []
    for cols, dtype, transposed in outs:
        if transposed:
            out_specs.append(pl.BlockSpec((cols, tm), lambda i: (0, i)))
            out_shape.append(jax.ShapeDtypeStruct((cols, R), dtype))
        else:
            out_specs.append(pl.BlockSpec((tm, cols), lambda i: (i, 0)))
            out_shape.append(jax.ShapeDtypeStruct((R, cols), dtype))
    return pl.pallas_call(
        functools.partial(_proj_kernel, plan=tuple(plan), rope=rope is not None, x_pair=x_pair,
                          n_lat=rows.n_lat),
        grid=(rows.n_all,),
        in_specs=in_specs,
        out_specs=out_specs,
        out_shape=out_shape,
        scratch_shapes=[pltpu.VMEM((tm, D), F32)] if x_pair else [],
        compiler_params=_cparams(("arbitrary",)),
        name="proj",
    )(*args)


def _post_kernel(x_ref, mod_ref, *rest, scan_heads, has_norm_g, alpha, fc, n_lat, x_pair):
    x = x_ref[...]
    if x_pair:
        x = _pick_rows(x_ref, rest[0], rest[-1], n_lat)
        rest = rest[1:-1]
    if scan_heads:
        of_ref, ob_ref, g_ref = rest[:3]
        rest = rest[3:]
        if has_norm_g:
            ng_ref, rest = rest[0], rest[1:]
        t = of_ref[...].astype(F32) + ob_ref[...].astype(F32)
        dh = t.shape[1] // scan_heads
        parts = []
        for h in range(scan_heads):
            th = t[:, h * dh:(h + 1) * dh]
            yh = th * lax.rsqrt(jnp.mean(th * th, axis=-1, keepdims=True) + RMS_EPS)
            parts.append(yh * ng_ref[...] if has_norm_g else yh)
        mix = (jnp.concatenate(parts, axis=1) * _silu(g_ref[...].astype(F32))).astype(BF16)
    else:
        mix = _pick_rows(rest[0], rest[1], rest[-1], n_lat)
        rest = rest[2:-1]
    wo_ref, ln1g_ref, ln1b_ref, w13_ref, w2_ref, ln2g_ref, ln2b_ref, o_ref = rest
    r = alpha * x + mod_ref[0, 2:3, :] * _dot(mix, wo_ref[...])
    h1 = _layer_norm(r, ln1g_ref[...], ln1b_ref[...])
    a = (h1 * (1.0 + mod_ref[0, 4:5, :]) + mod_ref[0, 3:4, :]).astype(BF16)
    F = w2_ref.shape[0]
    acc = jnp.zeros(h1.shape, F32)
    for c in range(F // fc):
        g = _dot(a, w13_ref[:, c * fc:(c + 1) * fc])
        u = _dot(a, w13_ref[:, F + c * fc:F + (c + 1) * fc])
        acc = acc + _dot((_silu(g) * u).astype(BF16), w2_ref[c * fc:(c + 1) * fc, :])
    r2 = alpha * h1 + mod_ref[0, 5:6, :] * acc
    o_ref[...] = _layer_norm(r2, ln2g_ref[...], ln2b_ref[...])


def _post(rows, x, mod, mix, w_out, ln_g, ln_b, w13, w2, alpha, n_tiles, scan_heads=0, norm_g=None):
    x_pair = isinstance(x, tuple)
    K, D = w_out.shape
    F = w2.shape[0]
    tm = rows.tm
    row = lambda cols, blk=0: pl.BlockSpec((tm, cols), lambda i: (i, blk))
    mod_spec = pl.BlockSpec((1, 6, D), lambda i: (rows.group(i), 0, 0))
    if x_pair:
        lat_spec, ctx_spec = rows.pair_specs(D)
        in_specs = [lat_spec, mod_spec, ctx_spec]
        args = [x[0], mod, x[1]]
    else:
        in_specs = [row(D), mod_spec]
        args = [x, mod]
    scratch = []
    if scan_heads:
        o_f, o_b, (g_arr, g_blk) = mix
        in_specs += [row(K), row(K), row(K, g_blk)]
        args += [o_f, o_b, g_arr]
        if norm_g is not None:
            in_specs.append(_const_spec((1, norm_g.shape[0])))
            args.append(norm_g.reshape(1, -1))
    else:
        in_specs += rows.pair_specs(K)
        args += list(mix)
        scratch = [pltpu.VMEM((tm, K), BF16)]
    vec = lambda v: v.reshape(1, D)
    in_specs += [_const_spec((K, D)), _const_spec((1, D)), _const_spec((1, D)),
                 _const_spec((D, 2 * F)), _const_spec((F, D)), _const_spec((1, D)), _const_spec((1, D))]
    args += [w_out, vec(ln_g[0]), vec(ln_b[0]), w13, w2, vec(ln_g[1]), vec(ln_b[1])]
    return pl.pallas_call(
        functools.partial(_post_kernel, scan_heads=scan_heads, has_norm_g=norm_g is not None,
                          alpha=alpha, fc=256, n_lat=rows.n_lat, x_pair=x_pair),
        grid=(n_tiles,),
        in_specs=in_specs,
        out_specs=row(D),
        out_shape=jax.ShapeDtypeStruct((n_tiles * tm, D), F32),
        scratch_shapes=scratch + ([pltpu.VMEM((tm, D), F32)] if x_pair else []),
        compiler_params=_cparams(("arbitrary",)),
        name="post",
    )(*args)


def _chunk_order(B, L, Lc, C, backward):
    ncc, ncl = Lc // C, L // C

    def blk(b, j):
        if backward:
            in_ctx = j < ncc
            return jnp.where(in_ctx, B * ncl + b * ncc + (ncc - 1 - j), b * ncl + (ncl - 1 - (j - ncc)))
        return jnp.where(j < ncc, B * ncl + b * ncc + j, b * ncl + (j - ncc))

    return blk, ncc + ncl


def _ret_scan_kernel(lg_ref, qf_ref, kf_ref, vf_ref, qb_ref, kb_ref, vb_ref, of_ref, ob_ref, st_scr,
                     *, C, heads):
    j = pl.program_id(1)

    @pl.when(j == 0)
    def _():
        st_scr[...] = jnp.zeros_like(st_scr)

    dk = qf_ref.shape[1] // heads
    dv = vf_ref.shape[1] // heads
    ti = lax.broadcasted_iota(jnp.int32, (C, C), 0)
    si = lax.broadcasted_iota(jnp.int32, (C, C), 1)
    tcol = lax.broadcasted_iota(jnp.int32, (C, 1), 0)
    for d, (q_ref, k_ref, v_ref, o_ref) in enumerate(((qf_ref, kf_ref, vf_ref, of_ref),
                                                      (qb_ref, kb_ref, vb_ref, ob_ref))):
        backward = d == 1
        dist = (si - ti) if backward else (ti - si)
        keep = dist >= 0
        fdist = jnp.maximum(dist, 0).astype(F32)
        eq = ((C - tcol) if backward else (tcol + 1)).astype(F32)
        ek = (tcol if backward else (C - 1 - tcol)).astype(F32)
        for h in range(heads):
            lg = lg_ref[d, h]
            q = q_ref[:, h * dk:(h + 1) * dk]
            k = k_ref[:, h * dk:(h + 1) * dk]
            v = v_ref[:, h * dv:(h + 1) * dv]
            decay = jnp.where(keep, jnp.exp(lg * fdist), 0.0)
            att = (_dot_nt(q, k) * decay).astype(BF16)
            st = st_scr[d, h]
            o = _dot(att, v) + _dot_nt(q, st.astype(BF16)) * jnp.exp(lg * eq)
            kd = (k.astype(F32) * jnp.exp(lg * ek)).astype(BF16)
            st_scr[d, h] = st * jnp.exp(lg * jnp.full((1, 1), float(C), F32)) + _dot_tn(v, kd)
            o_ref[:, h * dv:(h + 1) * dv] = o.astype(o_ref.dtype)


def _ret_scan(proj, lg, B, L, Lc):
    R = proj.shape[0]
    C = RET_CHUNK
    H, dk, dv = RET_HEADS, 256, 512
    blk_f, nch = _chunk_order(B, L, Lc, C, False)
    blk_b, _ = _chunk_order(B, L, Lc, C, True)
    specs = lambda blk: [pl.BlockSpec((C, H * dk), lambda b, j: (blk(b, j), 0)),
                         pl.BlockSpec((C, H * dk), lambda b, j: (blk(b, j), 1)),
                         pl.BlockSpec((C, H * dv), lambda b, j: (blk(b, j), 1))]
    return pl.pallas_call(
        functools.partial(_ret_scan_kernel, C=C, heads=H),
        grid=(B, nch),
        in_specs=[pl.BlockSpec(memory_space=pltpu.SMEM)] + specs(blk_f) + specs(blk_b),
        out_specs=[pl.BlockSpec((C, H * dv), lambda b, j: (blk_f(b, j), 0)),
                   pl.BlockSpec((C, H * dv), lambda b, j: (blk_b(b, j), 0))],
        out_shape=[jax.ShapeDtypeStruct((R, H * dv), BF16)] * 2,
        scratch_shapes=[pltpu.VMEM((2, H, dv, dk), F32)],
        compiler_params=_cparams(("arbitrary", "arbitrary")),
        name="ret_scan",
    )(lg, proj, proj, proj, proj, proj, proj)


def _cumsum_rows(x):
    n = x.shape[0]
    r = lax.broadcasted_iota(jnp.int32, (n, 1), 0)
    s = 1
    while s < n:
        x = x + jnp.where(r >= s, pltpu.roll(x, s, axis=0), 0.0)
        s *= 2
    return x


def _hg_scan_kernel(qf_ref, ff_ref, vf_ref, qb_ref, fb_ref, vb_ref, lb_ref, of_ref, ob_ref, st_scr,
                    *, C, sub, heads, scale):
    j = pl.program_id(1)

    @pl.when(j == 0)
    def _():
        st_scr[...] = jnp.zeros_like(st_scr)

    lb = lb_ref[...]
    ti = lax.broadcasted_iota(jnp.int32, (C, C), 0)
    si = lax.broadcasted_iota(jnp.int32, (C, C), 1)
    d = HG_EXPAND
    for dr, (q_ref, f_ref, v_ref, o_ref) in enumerate(((qf_ref, ff_ref, vf_ref, of_ref),
                                                       (qb_ref, fb_ref, vb_ref, ob_ref))):
        backward = dr == 1
        keep = (si >= ti) if backward else (ti >= si)
        for c in (range(sub - 1, -1, -1) if backward else range(sub)):
            rs = slice(c * C, (c + 1) * C)
            forget = lb + (1.0 - lb) * jax.nn.sigmoid(f_ref[rs, :])
            kk = 1.0 - forget
            gl = jnp.log(forget)
            pre = _cumsum_rows(gl)
            tot = pre[C - 1:C, :]
            bc = (tot - pre + gl) if backward else pre
            qd = (_silu(q_ref[rs, :].astype(F32)) * scale * jnp.exp(bc)).astype(BF16)
            kd = (kk * jnp.exp(-bc)).astype(BF16)
            ke = (kk * jnp.exp(tot - bc)).astype(BF16)
            v = v_ref[rs, :]
            etot = jnp.exp(tot)
            for h in range(heads):
                sl = slice(h * d, (h + 1) * d)
                att = jnp.where(keep, _dot_nt(qd[:, sl], kd[:, sl]), 0.0).astype(BF16)
                st = st_scr[dr, h]
                o = _dot(att, v[:, sl]) + _dot_nt(qd[:, sl], st.astype(BF16))
                st_scr[dr, h] = st * etot[:, sl] + _dot_tn(v[:, sl], ke[:, sl])
                o_ref[rs, sl] = o.astype(o_ref.dtype)


def _hg_scan(qig, ff, lb, B, L, Lc):
    R = qig.shape[0]
    sub = 4
    C = HG_CHUNK * sub
    Dm = qig.shape[1] // 3
    heads = Dm // HG_EXPAND
    blk_f, nch = _chunk_order(B, L, Lc, C, False)
    blk_b, _ = _chunk_order(B, L, Lc, C, True)
    specs = lambda blk, fcol: [pl.BlockSpec((C, Dm), lambda b, j: (blk(b, j), 0)),
                               pl.BlockSpec((C, Dm), lambda b, j: (blk(b, j), fcol)),
                               pl.BlockSpec((C, Dm), lambda b, j: (blk(b, j), 1))]
    return pl.pallas_call(
        functools.partial(_hg_scan_kernel, C=HG_CHUNK, sub=sub, heads=heads, scale=HG_EXPAND ** -0.5),
        grid=(B, nch),
        in_specs=specs(blk_f, 0) + specs(blk_b, 1) + [pl.BlockSpec((1, Dm), lambda b, j: (0, 0))],
        out_specs=[pl.BlockSpec((C, Dm), lambda b, j: (blk_f(b, j), 0)),
                   pl.BlockSpec((C, Dm), lambda b, j: (blk_b(b, j), 0))],
        out_shape=[jax.ShapeDtypeStruct((R, Dm), BF16)] * 2,
        scratch_shapes=[pltpu.VMEM((2, heads, HG_EXPAND, HG_EXPAND), F32)],
        compiler_params=_cparams(("arbitrary", "arbitrary")),
        name="hg_scan",
    )(qig, ff, qig, qig, ff, qig, lb.reshape(1, Dm))


def _softmax_pv(scores, values):
    m = functools.reduce(jnp.maximum, [jnp.max(s, axis=-1, keepdims=True) for s in scores])
    ps = [jnp.exp(s - m) for s in scores]
    l = functools.reduce(lambda a, b: a + b, [jnp.sum(p, axis=-1, keepdims=True) for p in ps])
    o = functools.reduce(lambda a, b: a + b, [_dot(p.astype(BF16), v) for p, v in zip(ps, values)])
    return o / l


def _lane_lo(shape):
    return lax.broadcasted_iota(jnp.int32, shape, 1) < (LANES // 2)


NA_FRAME_ROWS = NA_WIN_ROWS + 2
LOG2E = float(np.log2(np.e))
NA_Q_SCALE = 0.125 * LOG2E


def _na_kernel(q_ref, kw_ref, vtw_ref, kc_ref, vtc_ref, bias_ref, o_ref, sla_ref, slb_ref, sca_ref, scb_ref,
               *, heads, nrows):
    W, wr, fr = GRID_W, NA_WIN_ROWS, NA_FRAME_ROWS
    r0 = 2 * pl.program_id(1)
    u = jnp.minimum(jnp.clip(r0 - wr // 2, 0, nrows - wr), nrows - fr)
    tile = []
    for j in range(fr):
        per = []
        for rho in range(2):
            r = r0 + rho
            rs = jnp.clip(r - wr // 2, 0, nrows - wr)
            ok = (u + j >= rs) & (u + j < rs + wr)
            per.append(jnp.where(ok, u + j - r + wr - 1, 2 * wr - 1))
        tile.append(per)
    lo = _lane_lo((2 * W, LANES))
    ones_l = jnp.ones((8, fr * W), BF16)
    ones_c = jnp.ones((8, kc_ref.shape[0]), BF16)
    ri = lax.broadcasted_iota(jnp.int32, (LANES, 4 * W), 0)
    li = lax.broadcasted_iota(jnp.int32, (LANES, 4 * W), 1)
    own_head = (ri >= W) == ((li % LANES) >= W)
    npairs = heads // 2
    lanes = lambda hp: pl.ds(pl.multiple_of(hp * LANES, LANES), LANES)

    def put(bufs, hp):
        sl_buf, sc_buf = bufs
        q2 = q_ref[:, lanes(hp)] * NA_Q_SCALE
        qlo = jnp.where(lo, q2, jnp.zeros_like(q2))
        qhi = jnp.where(lo, jnp.zeros_like(q2), q2)
        qblk = jnp.concatenate([qlo[:W], qhi[:W], qlo[W:], qhi[W:]], axis=0)
        s_raw = _dot_nt(kw_ref[:, lanes(hp)], qblk)
        m = None
        for j in range(fr):
            sj = (s_raw[j * W:(j + 1) * W]
                  + jnp.concatenate([bias_ref[hp, tile[j][0]], bias_ref[hp, tile[j][1]]], axis=1))
            sl_buf[j * W:(j + 1) * W, :] = sj
            mj = jnp.max(sj, axis=0, keepdims=True)
            m = mj if m is None else jnp.maximum(m, mj)
        s_c = _dot_nt(kc_ref[:, lanes(hp)], qblk)
        sc_buf[...] = s_c
        return jnp.maximum(m, jnp.max(s_c, axis=0, keepdims=True))

    def fin(bufs, hp, m):
        sl_buf, sc_buf = bufs
        p_l = jnp.exp2(sl_buf[...] - m).astype(BF16)
        p_c = jnp.exp2(sc_buf[...] - m).astype(BF16)
        rows = pl.ds(pl.multiple_of(hp * LANES, LANES), LANES)
        acc = (_dot(jnp.concatenate([vtw_ref[rows, :], ones_l], axis=0), p_l)
               + _dot(jnp.concatenate([vtc_ref[rows, :], ones_c], axis=0), p_c))
        ot = jnp.where(own_head, acc[:LANES] / acc[LANES:LANES + 1], 0.0)
        tr = ot.T
        for rho in range(2):
            blk = tr[rho * LANES:rho * LANES + W] + tr[rho * LANES + W:(rho + 1) * LANES]
            o_ref[rho * W:(rho + 1) * W, lanes(hp)] = blk.astype(o_ref.dtype)

    buf_a, buf_b = (sla_ref, sca_ref), (slb_ref, scb_ref)
    m0 = put(buf_a, 0)

    def body(i, m_a):
        m_b = put(buf_b, 2 * i + 1)
        fin(buf_a, 2 * i, m_a)
        m_next = put(buf_a, 2 * i + 2)
        fin(buf_b, 2 * i + 1, m_b)
        return m_next

    m_a = lax.fori_loop(0, npairs // 2 - 1, body, m0)
    m_b = put(buf_b, npairs - 1)
    fin(buf_a, npairs - 2, m_a)
    fin(buf_b, npairs - 1, m_b)


def _na_bias_table(rpb):
    H = rpb.shape[0]
    W, wr, wc = GRID_W, NA_WIN_ROWS, NA_WIN_COLS
    qcol = np.arange(W)[:, None]
    kcol = np.arange(W)[None, :]
    ws = np.clip(qcol - wc // 2, 0, W - wc)
    ok = (kcol >= ws) & (kcol < ws + wc)
    r_pad = jnp.pad(rpb.astype(F32), ((0, 0), (0, 0), (W - wc, W + wc - (2 * wc - 1))))
    skew = jnp.tile(r_pad, (1, 1, W))[:, :, :W * (2 * W - 1)].reshape(H, 2 * wr - 1, W, 2 * W - 1)
    t15 = jnp.where(ok[None, None], skew[:, :, :, W - 1:] * LOG2E, NEG_INF)
    t = t15.reshape(H // 2, 2, 2 * wr - 1, W, W).transpose(0, 2, 4, 1, 3).reshape(H // 2, 2 * wr - 1, W, 2 * W)
    return jnp.concatenate([t, jnp.full((H // 2, 1, W, 2 * W), NEG_INF, F32)], axis=1)


def _na_attention(qkv, vt, bias, B, L, Lc):
    D = qkv.shape[1] // 3
    W, wr, fr = GRID_W, NA_WIN_ROWS, NA_FRAME_ROWS
    nrows = L // W
    assert nrows % 2 == 0 and nrows >= fr
    frame0 = lambda g: jnp.minimum(jnp.clip(2 * g - wr // 2, 0, nrows - wr), nrows - fr)
    ctx_blk0 = B * L // Lc
    return pl.pallas_call(
        functools.partial(_na_kernel, heads=NA_HEADS, nrows=nrows),
        grid=(B, nrows // 2),
        in_specs=[pl.BlockSpec((2 * W, D), lambda b, g: (b * (nrows // 2) + g, 0)),
                  pl.BlockSpec((pl.Element(fr * W), pl.Element(D)),
                               lambda b, g: ((b * nrows + frame0(g)) * W, D)),
                  pl.BlockSpec((pl.Element(D), pl.Element(fr * W)),
                               lambda b, g: (0, pl.multiple_of((b * nrows + frame0(g)) * W, 2 * W))),
                  pl.BlockSpec((Lc, D), lambda b, g: (ctx_blk0 + b, 1)),
                  pl.BlockSpec((D, Lc), lambda b, g: (0, ctx_blk0 + b)),
                  _const_spec(bias.shape)],
        out_specs=pl.BlockSpec((2 * W, D), lambda b, g: (b * (nrows // 2) + g, 0)),
        out_shape=jax.ShapeDtypeStruct((B * L, D), BF16),
        scratch_shapes=[pltpu.VMEM((fr * W, 4 * W), F32)] * 2 + [pltpu.VMEM((Lc, 4 * W), F32)] * 2,
        compiler_params=_cparams(("arbitrary", "arbitrary")),
        name="na_attn",
    )(qkv, qkv, vt, qkv, vt, bias)


def _ctx_attn_kernel(q_ref, k_ref, v_ref, o_ref, *, heads, split64, q_scale):
    lo = _lane_lo((q_ref.shape[0], LANES))
    for hp in range(heads // 2):
        vsl = slice(hp * LANES, (hp + 1) * LANES)
        v2 = v_ref[:, vsl]
        outs = []
        for e in range(2):
            if split64:
                q2 = q_ref[:, vsl] * q_scale
                qh = jnp.where(lo if e == 0 else ~lo, q2, jnp.zeros_like(q2))
                kh = k_ref[:, vsl]
            else:
                hsl = slice((2 * hp + e) * LANES, (2 * hp + e + 1) * LANES)
                qh, kh = q_ref[:, hsl], k_ref[:, hsl]
            outs.append(_softmax_pv([_dot_nt(qh, kh)], [v2]))
        o_ref[:, vsl] = jnp.where(lo, outs[0], outs[1]).astype(o_ref.dtype)


def _ctx_attention(q_src, k_src, v_src, B, L, Lc, heads, split64, q_scale=1.0):
    blk0 = B * L // Lc
    spec = lambda src: pl.BlockSpec((Lc, src[1]), lambda b: (blk0 + b, src[2]))
    Dv = v_src[1]
    return pl.pallas_call(
        functools.partial(_ctx_attn_kernel, heads=heads, split64=split64, q_scale=q_scale),
        grid=(B,),
        in_specs=[spec(q_src), spec(k_src), spec(v_src)],
        out_specs=pl.BlockSpec((Lc, Dv), lambda b: (b, 0)),
        out_shape=jax.ShapeDtypeStruct((B * Lc, Dv), BF16),
        compiler_params=_cparams(("arbitrary",)),
        name="ctx_attn",
    )(q_src[0], k_src[0], v_src[0])


def _rope_slot(x, cos, sina, sinb):
    return x * cos + pltpu.roll(x, LANES - 16, axis=1) * sina + pltpu.roll(x, 16, axis=1) * sinb


def _mla_proj_kernel(x_ref, mod_ref, wd_ref, qn_ref, kvn_ref, wq_ref, wk_ref, wvt_ref, vone_ref,
                     cos_ref, sina_ref, sinb_ref, q_ref, k_ref, vt_ref, *, heads, scale):
    sh = mod_ref[0, 0:1, :]
    sc = mod_ref[0, 1:2, :]
    a = (x_ref[...] * (1.0 + sc) + sh).astype(BF16)
    d = _dot(a, wd_ref[...])
    cq = d[:, :MLA_Q_LORA]
    ckv = d[:, MLA_Q_LORA:MLA_Q_LORA + MLA_KV_LORA]
    kr = d[:, MLA_Q_LORA + MLA_KV_LORA:]
    rms = lambda t, g: (t * lax.rsqrt(jnp.mean(t * t, axis=-1, keepdims=True) + RMS_EPS) * g).astype(BF16)
    cqn = rms(cq, qn_ref[...])
    ckvn = rms(ckv, kvn_ref[...])
    cos, sina, sinb = cos_ref[...], sina_ref[...], sinb_ref[...]
    krr = _rope_slot(kr, cos, sina, sinb)
    qf = _dot(cqn, wq_ref[...])
    kf = _dot(ckvn, wk_ref[...])
    for h in range(heads):
        sl = slice(h * LANES, (h + 1) * LANES)
        q_ref[:, sl] = (_rope_slot(qf[:, sl], cos, sina, sinb) * scale).astype(q_ref.dtype)
        k_ref[:, sl] = (kf[:, sl] + krr).astype(k_ref.dtype)
    vt_ref[...] = (_dot_nt(wvt_ref[...], ckvn) + vone_ref[...]).astype(vt_ref.dtype)


def _mla_proj(rows, x, mod, wd, qn, kvn, wq, wk, wvt, vone, tables):
    R, D = x.shape
    tm = rows.tm
    H = MLA_HEADS
    tab_spec = pl.BlockSpec((tm, LANES), lambda i: (rows.pos_block(i), 0))
    return pl.pallas_call(
        functools.partial(_mla_proj_kernel, heads=H,
                          scale=(MLA_NOPE + MLA_ROPE) ** -0.5 * float(np.log2(np.e))),
        grid=(rows.n_all,),
        in_specs=[pl.BlockSpec((tm, D), lambda i: (i, 0)),
                  pl.BlockSpec((1, 6, D), lambda i: (rows.group(i), 0, 0)),
                  _const_spec(wd.shape), _const_spec((1, MLA_Q_LORA)), _const_spec((1, MLA_KV_LORA)),
                  _const_spec(wq.shape), _const_spec(wk.shape), _const_spec(wvt.shape),
                  _const_spec(vone.shape), tab_spec, tab_spec, tab_spec],
        out_specs=[pl.BlockSpec((tm, H * LANES), lambda i: (i, 0)),
                   pl.BlockSpec((tm, H * LANES), lambda i: (i, 0)),
                   pl.BlockSpec((H * LANES, tm), lambda i: (0, i))],
        out_shape=[jax.ShapeDtypeStruct((R, H * LANES), BF16),
                   jax.ShapeDtypeStruct((R, H * LANES), BF16),
                   jax.ShapeDtypeStruct((H * LANES, R), BF16)],
        compiler_params=_cparams(("arbitrary",)),
        name="mla_proj",
    )(x, mod, wd, qn.reshape(1, -1), kvn.reshape(1, -1), wq, wk, wvt, vone, *tables)


def _mla_flash_kernel(q_ref, kc_ref, vtc_ref, *rest, tk, cpi, with_latent):
    if with_latent:
        k_ref, vt_ref, o_ref = rest[:3]
        bufs = rest[3:]
    else:
        (o_ref,) = rest
    tq = q_ref.shape[0]
    hsl = [slice(e * LANES, (e + 1) * LANES) for e in range(2)]
    qs = [q_ref[:, sl] for sl in hsl]

    def scores(kblk):
        return [_dot_nt(kblk[:, hsl[e]], qs[e]) for e in range(2)]

    vrows = MLA_V + 8

    def colmax(ss):
        return [jnp.max(s, axis=0, keepdims=True) for s in ss]

    def update(ss, cms, vtblk, state):
        out = []
        for e in range(2):
            m, acc = state[e]
            m_new = jnp.maximum(m, cms[e])
            p = jnp.exp2(ss[e] - m_new).astype(BF16)
            acc = jnp.exp2(m - m_new) * acc + _dot(vtblk[e * LANES:e * LANES + vrows, :], p)
            out.append((m_new, acc))
        return tuple(out)

    init = (jnp.full((1, tq), NEG_INF, F32), jnp.zeros((vrows, tq), F32))
    s_ctx = scores(kc_ref[...])
    if not with_latent:
        state = update(s_ctx, colmax(s_ctx), vtc_ref[...], (init, init))
    else:
        n = k_ref.shape[0] // tk
        nb = len(bufs)
        assert cpi % nb == 0 and n % cpi == 0

        def put(buf, c):
            ss = scores(k_ref[pl.ds(pl.multiple_of(c * tk, tk), tk), :])
            for e in range(2):
                buf[e] = ss[e]
            return tuple(colmax(ss))

        def group(c0, carry, last):
            st, cms = carry
            for t in range(cpi):
                nxt = None
                if not (last and t == cpi - 1):
                    nxt = put(bufs[(t + 1) % nb], c0 + t + 1)
                buf = bufs[t % nb]
                vtblk = vt_ref[:, pl.ds(pl.multiple_of((c0 + t) * tk, tk), tk)]
                st = update([buf[0], buf[1]], cms, vtblk, st)
                cms = nxt
            return st, cms

        cms = put(bufs[0], 0)
        state = update(s_ctx, colmax(s_ctx), vtc_ref[...], (init, init))
        carry = lax.fori_loop(0, n // cpi - 1, lambda i, cr: group(i * cpi, cr, False), (state, cms))
        state, _ = group(n - cpi, carry, True)
    ot = jnp.concatenate([acc[:MLA_V] / acc[MLA_V:MLA_V + 1] for _, acc in state], axis=0)
    o_ref[...] = ot.T.astype(o_ref.dtype)


def _mla_attention(q, k, vt, B, L, Lc, tq, tk, latent_queries):
    H = MLA_HEADS
    ctx_blk0 = B * L // Lc
    if latent_queries:
        nq, q0, n_out = L // tq, 0, B * L
    else:
        assert tq == Lc
        nq, q0, n_out = 1, ctx_blk0, B * Lc
    in_specs = [pl.BlockSpec((tq, 2 * LANES), lambda b, hp, i: (q0 + b * nq + i, hp)),
                pl.BlockSpec((Lc, 2 * LANES), lambda b, hp, i: (ctx_blk0 + b, hp)),
                pl.BlockSpec((2 * LANES, Lc), lambda b, hp, i: (hp, ctx_blk0 + b))]
    args = [q, k, vt]
    scratch = []
    if latent_queries:
        in_specs += [pl.BlockSpec((L, 2 * LANES), lambda b, hp, i: (b, hp)),
                     pl.BlockSpec((2 * LANES, L), lambda b, hp, i: (hp, b))]
        args += [k, vt]
        scratch = [pltpu.VMEM((2, tk, tq), F32)] * 2
    return pl.pallas_call(
        functools.partial(_mla_flash_kernel, tk=tk, cpi=4 if (L // tk) % 4 == 0 and L // tk > 8 else 2,
                          with_latent=latent_queries),
        grid=(B, H // 2, nq),
        in_specs=in_specs,
        out_specs=pl.BlockSpec((tq, LANES), lambda b, hp, i: (b * nq + i, hp)),
        out_shape=jax.ShapeDtypeStruct((n_out, H * MLA_V), BF16),
        scratch_shapes=scratch,
        compiler_params=_cparams(("arbitrary", "arbitrary", "arbitrary")),
        name="mla_flash" if latent_queries else "mla_flash_ctx",
    )(*args)


def _axial_angles(L, rot_dim):
    t = jnp.arange(L)
    rows = (t // GRID_W).astype(F32)
    cols = (t % GRID_W).astype(F32)
    n_freq = rot_dim // 4
    inv = ROPE_BASE ** (-jnp.arange(n_freq, dtype=F32) / n_freq)
    return jnp.concatenate([rows[:, None] * inv, cols[:, None] * inv], -1)


def _ret_rope_tables(L, tm):
    ang = _axial_angles(L, 256)
    cos = jnp.concatenate([jnp.cos(ang), jnp.ones((tm, LANES), F32)], 0)
    sin = jnp.concatenate([jnp.sin(ang), jnp.zeros((tm, LANES), F32)], 0)
    return cos, sin


def _mla_rope_tables(L, tm):
    ang = _axial_angles(L, MLA_ROPE)
    c, s = jnp.cos(ang), jnp.sin(ang)
    one = jnp.ones((L, MLA_NOPE), F32)
    z16 = jnp.zeros((L, 16), F32)
    z32 = jnp.zeros((L, 32), F32)
    z64 = jnp.zeros((L, MLA_NOPE), F32)
    cos = jnp.concatenate([one, c, c, jnp.ones((L, 32), F32)], -1)
    sina = jnp.concatenate([z64, -s, z16, z32], -1)
    sinb = jnp.concatenate([z64, z16, s, z32], -1)
    ident = lambda t, fill: jnp.concatenate([t, jnp.full((tm, LANES), fill, F32)], 0)
    return ident(cos, 1.0), ident(sina, 0.0), ident(sinb, 0.0)


def kernel(x, c, ctx, c_ctx, ada_w, ada_b, ln_g, ln_b, ffn_w13, ffn_w2, ret_w_in, ret_decay, ret_w_out, na_w_qkv, na_rpb, na_w_out, mla_w_down, mla_q_norm, mla_kv_norm, mla_w_uq, mla_w_ukv, mla_w_out, hg_w_in, hg_lower_bounds, hg_norm_g, hg_w_out):
    B, L, D = x.shape
    Lc = ctx.shape[1]
    depth = ada_w.shape[0]
    alpha = (2 * depth) ** 0.25
    tm = 512 if (B * Lc) % 512 == 0 else 256
    rows = _Rows(B, L, Lc, tm)
    n_lat_rows = B * L

    h = (x.reshape(B * L, D), ctx.reshape(B * Lc, D))

    G = 8 * (-(-(B + 1) // 8))
    cond_in = jnp.zeros((G, D), F32).at[0].set(c_ctx).at[1:B + 1].set(c)
    mods = _adaln(cond_in, ada_w, ada_b).reshape(depth, G, 6, D)

    bf = lambda w: w.astype(BF16)
    H = MLA_HEADS
    wd = jnp.zeros((D, MLA_Q_LORA + MLA_KV_LORA + LANES), F32)
    wd = wd.at[:, :MLA_Q_LORA + MLA_KV_LORA].set(mla_w_down[:, :MLA_Q_LORA + MLA_KV_LORA])
    wd = wd.at[:, MLA_Q_LORA + MLA_KV_LORA + MLA_NOPE:MLA_Q_LORA + MLA_KV_LORA + MLA_NOPE + MLA_ROPE].set(
        mla_w_down[:, MLA_Q_LORA + MLA_KV_LORA:])
    wq = jnp.pad(mla_w_uq.reshape(MLA_Q_LORA, H, MLA_NOPE + MLA_ROPE),
                 ((0, 0), (0, 0), (0, LANES - MLA_NOPE - MLA_ROPE))).reshape(MLA_Q_LORA, H * LANES)
    wukv = mla_w_ukv.reshape(MLA_KV_LORA, H, MLA_NOPE + MLA_V)
    wk = jnp.pad(wukv[:, :, :MLA_NOPE], ((0, 0), (0, 0), (0, LANES - MLA_NOPE))).reshape(MLA_KV_LORA, H * LANES)
    wvt = jnp.pad(wukv[:, :, MLA_NOPE:], ((0, 0), (0, 0), (0, LANES - MLA_V))).reshape(MLA_KV_LORA, H * LANES).T
    vone = jnp.tile((jnp.arange(LANES) == MLA_V).astype(F32), H).reshape(H * LANES, 1)

    plain = lambda n, width=1024: [(c0, width, "plain", 0, c0, 1.0, None) for c0 in range(0, n, width)]
    for i in range(depth):
        mod = mods[i]
        kind = i % 4
        n_tiles = rows.n_all if i < depth - 1 else rows.n_lat
        post = functools.partial(_post, rows, h, mod, ln_g=ln_g[i], ln_b=ln_b[i], w13=bf(ffn_w13[i]),
                                 w2=bf(ffn_w2[i]), alpha=alpha, n_tiles=n_tiles)
        if kind == 0:
            lg = -jnp.exp(ret_decay.astype(F32))
            plan = [(0, 1024, "rope", 0, 0, 1.0, None), (1024, 1024, "rope", 0, 1024, 256 ** -0.5, None)]
            plan += plain(6144)[2:]
            (proj,) = _proj(rows, h, mod, bf(ret_w_in), plan, [(6144, BF16, False)],
                            rope=_ret_rope_tables(L, tm))
            o_f, o_b = _ret_scan(proj, lg, B, L, Lc)
            h = post(mix=(o_f, o_b, (proj, 2)), w_out=bf(ret_w_out), scan_heads=RET_HEADS)
        elif kind == 1:
            plan = plain(2048) + [(2048, 1024, "plain", 0, 2048, 1.0, 1)]
            qkv, vt = _proj(rows, h, mod, bf(na_w_qkv), plan, [(3072, BF16, False), (1024, BF16, True)])
            o_lat = _na_attention(qkv, vt, _na_bias_table(na_rpb), B, L, Lc)
            o_ctx = _ctx_attention((qkv, D, 0), (qkv, D, 1), (qkv, D, 2), B, L, Lc, NA_HEADS,
                                   split64=True, q_scale=0.125)
            h = post(mix=(o_lat, o_ctx), w_out=bf(na_w_out))
        elif kind == 2:
            q, k, vt = _mla_proj(rows, h, mod, bf(wd), mla_q_norm, mla_kv_norm, bf(wq), bf(wk), bf(wvt), vone,
                                 _mla_rope_tables(L, tm))
            o_lat = _mla_attention(q, k, vt, B, L, Lc, 512, 512, latent_queries=True)
            o_ctx = _mla_attention(q, k, vt, B, L, Lc, Lc, 256, latent_queries=False)
            h = post(mix=(o_lat, o_ctx), w_out=bf(mla_w_out))
        else:
            lb_soft = jax.nn.softmax(hg_lower_bounds.astype(F32), axis=0)
            lb = (jnp.cumsum(lb_soft, axis=0) - lb_soft[0])[i]
            plan = [(0, 1024, "plain", 0, 0, 1.0, None), (1024, 1024, "plain", 1, 0, 1.0, None),
                    (2048, 1024, "plain", 1, 1024, 1.0, None), (3072, 1024, "plain", 0, 1024, 1.0, None),
                    (4096, 1024, "plain", 0, 2048, 1.0, None)]
            qig, ff = _proj(rows, h, mod, bf(hg_w_in), plan, [(3072, BF16, False), (2048, F32, False)])
            o_f, o_b = _hg_scan(qig, ff, lb, B, L, Lc)
            h = post(mix=(o_f, o_b, (qig, 2)), w_out=bf(hg_w_out), scan_heads=D // HG_EXPAND,
                     norm_g=hg_norm_g)
    return h[:n_lat_rows].reshape(B, L, D)
```

```python
import functools

import numpy as np
import jax
import jax.numpy as jnp
from jax import lax
from jax.experimental import pallas as pl
from jax.experimental.pallas import tpu as pltpu

F32 = jnp.float32
BF16 = jnp.bfloat16

GRID_W = 64
LN_EPS = 1e-5
RMS_EPS = 1e-6
ROPE_BASE = 10000.0
NEG_INF = -1e30

RET_HEADS = 4
NA_HEADS = 16
NA_WIN_ROWS = 8
NA_WIN_COLS = 16
MLA_HEADS = 16
MLA_NOPE = 64
MLA_ROPE = 32
MLA_V = 64
MLA_Q_LORA = 512
MLA_KV_LORA = 256
HG_EXPAND = 128
HG_CHUNK = 64
RET_CHUNK = 256

VMEM_LIMIT = 56 * 1024 * 1024
LANES = 128


def _cparams(sem):
    return pltpu.CompilerParams(dimension_semantics=sem, vmem_limit_bytes=VMEM_LIMIT)


def _const_spec(shape):
    nd = len(shape)
    return pl.BlockSpec(shape, lambda *_: (0,) * nd, pipeline_mode=pl.Buffered(1))


def _dot(a, b):
    return jnp.dot(a, b, preferred_element_type=F32)


def _dot_nt(a, b):
    return lax.dot_general(a, b, (((1,), (1,)), ((), ())), preferred_element_type=F32)


def _dot_tn(a, b):
    return lax.dot_general(a, b, (((0,), (0,)), ((), ())), preferred_element_type=F32)


def _silu(x):
    return x * jax.nn.sigmoid(x)


def _layer_norm(r, g, b):
    mu = jnp.mean(r, axis=-1, keepdims=True)
    rc = r - mu
    var = jnp.mean(rc * rc, axis=-1, keepdims=True)
    return rc * lax.rsqrt(var + LN_EPS) * g + b


def _pick_rows(lat_ref, ctx_ref, scr, n_lat):
    is_lat = pl.program_id(0) < n_lat

    @pl.when(is_lat)
    def _():
        scr[...] = lat_ref[...]

    @pl.when(jnp.logical_not(is_lat))
    def _():
        scr[...] = ctx_ref[...]

    return scr[...]


class _Rows:
    def __init__(self, B, L, Lc, tm):
        assert L % tm == 0 and (B * Lc) % tm == 0
        self.B, self.L, self.Lc, self.tm = B, L, Lc, tm
        self.n_lat = B * L // tm
        self.n_all = (B * L + B * Lc) // tm
        self.per_b = L // tm

    def group(self, i):
        return jnp.where(i < self.n_lat, 1 + i // self.per_b, 0)

    def pos_block(self, i):
        return jnp.where(i < self.n_lat, i % self.per_b, self.per_b)

    def pair_specs(self, cols):
        return [pl.BlockSpec((self.tm, cols), lambda i: (jnp.minimum(i, self.n_lat - 1), 0)),
                pl.BlockSpec((self.tm, cols), lambda i: (jnp.maximum(i - self.n_lat, 0), 0))]


def _adaln_kernel(c_ref, w_ref, b_ref, o_ref):
    cond = _silu(c_ref[...])
    o_ref[0] = jnp.dot(cond, w_ref[0], preferred_element_type=F32,
                       precision=lax.Precision.HIGHEST) + b_ref[0]


def _adaln(cond_in, ada_w, ada_b):
    depth, D, N = ada_w.shape
    G = cond_in.shape[0]
    tn = 1536
    return pl.pallas_call(
        _adaln_kernel,
        grid=(depth, N // tn),
        in_specs=[pl.BlockSpec((G, D), lambda l, j: (0, 0)),
                  pl.BlockSpec((1, D, tn), lambda l, j: (l, 0, j)),
                  pl.BlockSpec((1, 1, tn), lambda l, j: (l, 0, j))],
        out_specs=pl.BlockSpec((1, G, tn), lambda l, j: (l, 0, j)),
        out_shape=jax.ShapeDtypeStruct((depth, G, N), F32),
        compiler_params=_cparams(("arbitrary", "arbitrary")),
        name="adaln",
    )(cond_in, ada_w, ada_b.reshape(depth, 1, N))


def _proj_kernel(x_ref, mod_ref, w_ref, *rest, plan, rope, x_pair, n_lat):
    x = x_ref[...]
    if x_pair:
        x = _pick_rows(x_ref, rest[0], rest[-1], n_lat)
        rest = rest[1:-1]
    if rope:
        cos_ref, sin_ref = rest[:2]
        outs = rest[2:]
    else:
        outs = rest
    sh = mod_ref[0, 0:1, :]
    sc = mod_ref[0, 1:2, :]
    a = (x * (1.0 + sc) + sh).astype(BF16)
    for c0, width, kind, oi, oc0, scale, t_out in plan:
        acc = _dot(a, w_ref[:, c0:c0 + width])
        o_ref = outs[oi]
        if kind == "rope":
            cos = cos_ref[...]
            sin = sin_ref[...]
            for h in range(width // (2 * LANES)):
                x1 = acc[:, h * 256:h * 256 + LANES]
                x2 = acc[:, h * 256 + LANES:(h + 1) * 256]
                lo = oc0 + h * 256
                o_ref[:, lo:lo + LANES] = ((x1 * cos - x2 * sin) * scale).astype(o_ref.dtype)
                o_ref[:, lo + LANES:lo + 256] = ((x1 * sin + x2 * cos) * scale).astype(o_ref.dtype)
        else:
            o_ref[:, oc0:oc0 + width] = acc.astype(o_ref.dtype)
        if t_out is not None:
            outs[t_out][...] = acc.T.astype(outs[t_out].dtype)


def _proj(rows, x, mod, w, plan, outs, rope=None):
    x_pair = isinstance(x, tuple)
    D = w.shape[0]
    tm = rows.tm
    R = rows.n_all * tm
    mod_spec = pl.BlockSpec((1, 6, D), lambda i: (rows.group(i), 0, 0))
    if x_pair:
        lat_spec, ctx_spec = rows.pair_specs(D)
        in_specs = [lat_spec, mod_spec, _const_spec(w.shape), ctx_spec]
        args = [x[0], mod, w, x[1]]
    else:
        in_specs = [pl.BlockSpec((tm, D), lambda i: (i, 0)), mod_spec, _const_spec(w.shape)]
        args = [x, mod, w]
    if rope is not None:
        in_specs += [pl.BlockSpec((tm, LANES), lambda i: (rows.pos_block(i), 0))] * 2
        args += list(rope)
    out_specs, out_shape = [], []
    for cols, dtype, transposed in outs:
        if transposed:
            out_specs.append(pl.BlockSpec((cols, tm), lambda i: (0, i)))
            out_shape.append(jax.ShapeDtypeStruct((cols, R), dtype))
        else:
            out_specs.append(pl.BlockSpec((tm, cols), lambda i: (i, 0)))
            out_shape.append(jax.ShapeDtypeStruct((R, cols), dtype))
    return pl.pallas_call(
        functools.partial(_proj_kernel, plan=tuple(plan), rope=rope is not None, x_pair=x_pair,
                          n_lat=rows.n_lat),
        grid=(rows.n_all,),
        in_specs=in_specs,
        out_specs=out_specs,
        out_shape=out_shape,
        scratch_shapes=[pltpu.VMEM((tm, D), F32)] if x_pair else [],
        compiler_params=_cparams(("arbitrary",)),
        name="proj",
    )(*args)


def _post_kernel(x_ref, mod_ref, *rest, scan_heads, has_norm_g, alpha, fc, n_lat, x_pair):
    x = x_ref[...]
    if x_pair:
        x = _pick_rows(x_ref, rest[0], rest[-1], n_lat)
        rest = rest[1:-1]
    if scan_heads:
        of_ref, ob_ref, g_ref = rest[:3]
        rest = rest[3:]
        if has_norm_g:
            ng_ref, rest = rest[0], rest[1:]
        t = of_ref[...].astype(F32) + ob_ref[...].astype(F32)
        dh = t.shape[1] // scan_heads
        parts = []
        for h in range(scan_heads):
            th = t[:, h * dh:(h + 1) * dh]
            yh = th * lax.rsqrt(jnp.mean(th * th, axis=-1, keepdims=True) + RMS_EPS)
            parts.append(yh * ng_ref[...] if has_norm_g else yh)
        mix = (jnp.concatenate(parts, axis=1) * _silu(g_ref[...].astype(F32))).astype(BF16)
    else:
        mix = _pick_rows(rest[0], rest[1], rest[-1], n_lat)
        rest = rest[2:-1]
    wo_ref, ln1g_ref, ln1b_ref, w13_ref, w2_ref, ln2g_ref, ln2b_ref, o_ref = rest
    r = alpha * x + mod_ref[0, 2:3, :] * _dot(mix, wo_ref[...])
    h1 = _layer_norm(r, ln1g_ref[...], ln1b_ref[...])
    a = (h1 * (1.0 + mod_ref[0, 4:5, :]) + mod_ref[0, 3:4, :]).astype(BF16)
    F = w2_ref.shape[0]
    acc = jnp.zeros(h1.shape, F32)
    for c in range(F // fc):
        g = _dot(a, w13_ref[:, c * fc:(c + 1) * fc])
        u = _dot(a, w13_ref[:, F + c * fc:F + (c + 1) * fc])
        acc = acc + _dot((_silu(g) * u).astype(BF16), w2_ref[c * fc:(c + 1) * fc, :])
    r2 = alpha * h1 + mod_ref[0, 5:6, :] * acc
    o_ref[...] = _layer_norm(r2, ln2g_ref[...], ln2b_ref[...])


def _post(rows, x, mod, mix, w_out, ln_g, ln_b, w13, w2, alpha, n_tiles, scan_heads=0, norm_g=None):
    x_pair = isinstance(x, tuple)
    K, D = w_out.shape
    F = w2.shape[0]
    tm = rows.tm
    row = lambda cols, blk=0: pl.BlockSpec((tm, cols), lambda i: (i, blk))
    mod_spec = pl.BlockSpec((1, 6, D), lambda i: (rows.group(i), 0, 0))
    if x_pair:
        lat_spec, ctx_spec = rows.pair_specs(D)
        in_specs = [lat_spec, mod_spec, ctx_spec]
        args = [x[0], mod, x[1]]
    else:
        in_specs = [row(D), mod_spec]
        args = [x, mod]
    scratch = []
    if scan_heads:
        o_f, o_b, (g_arr, g_blk) = mix
        in_specs += [row(K), row(K), row(K, g_blk)]
        args += [o_f, o_b, g_arr]
        if norm_g is not None:
            in_specs.append(_const_spec((1, norm_g.shape[0])))
            args.append(norm_g.reshape(1, -1))
    else:
        in_specs += rows.pair_specs(K)
        args += list(mix)
        scratch = [pltpu.VMEM((tm, K), BF16)]
    vec = lambda v: v.reshape(1, D)
    in_specs += [_const_spec((K, D)), _const_spec((1, D)), _const_spec((1, D)),
                 _const_spec((D, 2 * F)), _const_spec((F, D)), _const_spec((1, D)), _const_spec((1, D))]
    args += [w_out, vec(ln_g[0]), vec(ln_b[0]), w13, w2, vec(ln_g[1]), vec(ln_b[1])]
    return pl.pallas_call(
        functools.partial(_post_kernel, scan_heads=scan_heads, has_norm_g=norm_g is not None,
                          alpha=alpha, fc=256, n_lat=rows.n_lat, x_pair=x_pair),
        grid=(n_tiles,),
        in_specs=in_specs,
        out_specs=row(D),
        out_shape=jax.ShapeDtypeStruct((n_tiles * tm, D), F32),
        scratch_shapes=scratch + ([pltpu.VMEM((tm, D), F32)] if x_pair else []),
        compiler_params=_cparams(("arbitrary",)),
        name="post",
    )(*args)


def _chunk_order(B, L, Lc, C, backward):
    ncc, ncl = Lc // C, L // C

    def blk(b, j):
        if backward:
            in_ctx = j < ncc
            return jnp.where(in_ctx, B * ncl + b * ncc + (ncc - 1 - j), b * ncl + (ncl - 1 - (j - ncc)))
        return jnp.where(j < ncc, B * ncl + b * ncc + j, b * ncl + (j - ncc))

    return blk, ncc + ncl


def _ret_scan_kernel(lg_ref, qf_ref, kf_ref, vf_ref, qb_ref, kb_ref, vb_ref, of_ref, ob_ref, st_scr,
                     *, C, heads):
    j = pl.program_id(1)

    @pl.when(j == 0)
    def _():
        st_scr[...] = jnp.zeros_like(st_scr)

    dk = qf_ref.shape[1] // heads
    dv = vf_ref.shape[1] // heads
    ti = lax.broadcasted_iota(jnp.int32, (C, C), 0)
    si = lax.broadcasted_iota(jnp.int32, (C, C), 1)
    tcol = lax.broadcasted_iota(jnp.int32, (C, 1), 0)
    for d, (q_ref, k_ref, v_ref, o_ref) in enumerate(((qf_ref, kf_ref, vf_ref, of_ref),
                                                      (qb_ref, kb_ref, vb_ref, ob_ref))):
        backward = d == 1
        dist = (si - ti) if backward else (ti - si)
        keep = dist >= 0
        fdist = jnp.maximum(dist, 0).astype(F32)
        eq = ((C - tcol) if backward else (tcol + 1)).astype(F32)
        ek = (tcol if backward else (C - 1 - tcol)).astype(F32)
        for h in range(heads):
            lg = lg_ref[d, h]
            q = q_ref[:, h * dk:(h + 1) * dk]
            k = k_ref[:, h * dk:(h + 1) * dk]
            v = v_ref[:, h * dv:(h + 1) * dv]
            decay = jnp.where(keep, jnp.exp(lg * fdist), 0.0)
            att = (_dot_nt(q, k) * decay).astype(BF16)
            st = st_scr[d, h]
            o = _dot(att, v) + _dot_nt(q, st.astype(BF16)) * jnp.exp(lg * eq)
            kd = (k.astype(F32) * jnp.exp(lg * ek)).astype(BF16)
            st_scr[d, h] = st * jnp.exp(lg * jnp.full((1, 1), float(C), F32)) + _dot_tn(v, kd)
            o_ref[:, h * dv:(h + 1) * dv] = o.astype(o_ref.dtype)


def _ret_scan(proj, lg, B, L, Lc):
    R = proj.shape[0]
    C = RET_CHUNK
    H, dk, dv = RET_HEADS, 256, 512
    blk_f, nch = _chunk_order(B, L, Lc, C, False)
    blk_b, _ = _chunk_order(B, L, Lc, C, True)
    specs = lambda blk: [pl.BlockSpec((C, H * dk), lambda b, j: (blk(b, j), 0)),
                         pl.BlockSpec((C, H * dk), lambda b, j: (blk(b, j), 1)),
                         pl.BlockSpec((C, H * dv), lambda b, j: (blk(b, j), 1))]
    return pl.pallas_call(
        functools.partial(_ret_scan_kernel, C=C, heads=H),
        grid=(B, nch),
        in_specs=[pl.BlockSpec(memory_space=pltpu.SMEM)] + specs(blk_f) + specs(blk_b),
        out_specs=[pl.BlockSpec((C, H * dv), lambda b, j: (blk_f(b, j), 0)),
                   pl.BlockSpec((C, H * dv), lambda b, j: (blk_b(b, j), 0))],
        out_shape=[jax.ShapeDtypeStruct((R, H * dv), BF16)] * 2,
        scratch_shapes=[pltpu.VMEM((2, H, dv, dk), F32)],
        compiler_params=_cparams(("arbitrary", "arbitrary")),
        name="ret_scan",
    )(lg, proj, proj, proj, proj, proj, proj)


def _cumsum_rows(x):
    n = x.shape[0]
    r = lax.broadcasted_iota(jnp.int32, (n, 1), 0)
    s = 1
    while s < n:
        x = x + jnp.where(r >= s, pltpu.roll(x, s, axis=0), 0.0)
        s *= 2
    return x


def _hg_scan_kernel(qf_ref, ff_ref, vf_ref, qb_ref, fb_ref, vb_ref, lb_ref, of_ref, ob_ref, st_scr,
                    *, C, sub, heads, scale):
    j = pl.program_id(1)

    @pl.when(j == 0)
    def _():
        st_scr[...] = jnp.zeros_like(st_scr)

    lb = lb_ref[...]
    ti = lax.broadcasted_iota(jnp.int32, (C, C), 0)
    si = lax.broadcasted_iota(jnp.int32, (C, C), 1)
    d = HG_EXPAND
    for dr, (q_ref, f_ref, v_ref, o_ref) in enumerate(((qf_ref, ff_ref, vf_ref, of_ref),
                                                       (qb_ref, fb_ref, vb_ref, ob_ref))):
        backward = dr == 1
        keep = (si >= ti) if backward else (ti >= si)
        for c in (range(sub - 1, -1, -1) if backward else range(sub)):
            rs = slice(c * C, (c + 1) * C)
            forget = lb + (1.0 - lb) * jax.nn.sigmoid(f_ref[rs, :])
            kk = 1.0 - forget
            gl = jnp.log(forget)
            pre = _cumsum_rows(gl)
            tot = pre[C - 1:C, :]
            bc = (tot - pre + gl) if backward else pre
            qd = (_silu(q_ref[rs, :].astype(F32)) * scale * jnp.exp(bc)).astype(BF16)
            kd = (kk * jnp.exp(-bc)).astype(BF16)
            ke = (kk * jnp.exp(tot - bc)).astype(BF16)
            v = v_ref[rs, :]
            etot = jnp.exp(tot)
            for h in range(heads):
                sl = slice(h * d, (h + 1) * d)
                att = jnp.where(keep, _dot_nt(qd[:, sl], kd[:, sl]), 0.0).astype(BF16)
                st = st_scr[dr, h]
                o = _dot(att, v[:, sl]) + _dot_nt(qd[:, sl], st.astype(BF16))
                st_scr[dr, h] = st * etot[:, sl] + _dot_tn(v[:, sl], ke[:, sl])
                o_ref[rs, sl] = o.astype(o_ref.dtype)


def _hg_scan(qig, ff, lb, B, L, Lc):
    R = qig.shape[0]
    sub = 4
    C = HG_CHUNK * sub
    Dm = qig.shape[1] // 3
    heads = Dm // HG_EXPAND
    blk_f, nch = _chunk_order(B, L, Lc, C, False)
    blk_b, _ = _chunk_order(B, L, Lc, C, True)
    specs = lambda blk, fcol: [pl.BlockSpec((C, Dm), lambda b, j: (blk(b, j), 0)),
                               pl.BlockSpec((C, Dm), lambda b, j: (blk(b, j), fcol)),
                               pl.BlockSpec((C, Dm), lambda b, j: (blk(b, j), 1))]
    return pl.pallas_call(
        functools.partial(_hg_scan_kernel, C=HG_CHUNK, sub=sub, heads=heads, scale=HG_EXPAND ** -0.5),
        grid=(B, nch),
        in_specs=specs(blk_f, 0) + specs(blk_b, 1) + [pl.BlockSpec((1, Dm), lambda b, j: (0, 0))],
        out_specs=[pl.BlockSpec((C, Dm), lambda b, j: (blk_f(b, j), 0)),
                   pl.BlockSpec((C, Dm), lambda b, j: (blk_b(b, j), 0))],
        out_shape=[jax.ShapeDtypeStruct((R, Dm), BF16)] * 2,
        scratch_shapes=[pltpu.VMEM((2, heads, HG_EXPAND, HG_EXPAND), F32)],
        compiler_params=_cparams(("arbitrary", "arbitrary")),
        name="hg_scan",
    )(qig, ff, qig, qig, ff, qig, lb.reshape(1, Dm))


def _softmax_pv(scores, values):
    m = functools.reduce(jnp.maximum, [jnp.max(s, axis=-1, keepdims=True) for s in scores])
    ps = [jnp.exp(s - m) for s in scores]
    l = functools.reduce(lambda a, b: a + b, [jnp.sum(p, axis=-1, keepdims=True) for p in ps])
    o = functools.reduce(lambda a, b: a + b, [_dot(p.astype(BF16), v) for p, v in zip(ps, values)])
    return o / l


def _lane_lo(shape):
    return lax.broadcasted_iota(jnp.int32, shape, 1) < (LANES // 2)


NA_FRAME_ROWS = NA_WIN_ROWS + 2
LOG2E = float(np.log2(np.e))
NA_Q_SCALE = 0.125 * LOG2E


def _na_kernel(q_ref, kw_ref, vtw_ref, kc_ref, vtc_ref, bias_ref, o_ref, sla_ref, slb_ref, sca_ref, scb_ref,
               *, heads, nrows):
    W, wr, fr = GRID_W, NA_WIN_ROWS, NA_FRAME_ROWS
    r0 = 2 * pl.program_id(1)
    u = jnp.minimum(jnp.clip(r0 - wr // 2, 0, nrows - wr), nrows - fr)
    tile = []
    for j in range(fr):
        per = []
        for rho in range(2):
            r = r0 + rho
            rs = jnp.clip(r - wr // 2, 0, nrows - wr)
            ok = (u + j >= rs) & (u + j < rs + wr)
            per.append(jnp.where(ok, u + j - r + wr - 1, 2 * wr - 1))
        tile.append(per)
    lo = _lane_lo((2 * W, LANES))
    ones_l = jnp.ones((8, fr * W), BF16)
    ones_c = jnp.ones((8, kc_ref.shape[0]), BF16)
    ri = lax.broadcasted_iota(jnp.int32, (LANES, 4 * W), 0)
    li = lax.broadcasted_iota(jnp.int32, (LANES, 4 * W), 1)
    own_head = (ri >= W) == ((li % LANES) >= W)
    npairs = heads // 2
    lanes = lambda hp: pl.ds(pl.multiple_of(hp * LANES, LANES), LANES)

    def put(bufs, hp):
        sl_buf, sc_buf = bufs
        q2 = q_ref[:, lanes(hp)] * NA_Q_SCALE
        qlo = jnp.where(lo, q2, jnp.zeros_like(q2))
        qhi = jnp.where(lo, jnp.zeros_like(q2), q2)
        qblk = jnp.concatenate([qlo[:W], qhi[:W], qlo[W:], qhi[W:]], axis=0)
        s_raw = _dot_nt(kw_ref[:, lanes(hp)], qblk)
        m = None
        for j in range(fr):
            sj = (s_raw[j * W:(j + 1) * W]
                  + jnp.concatenate([bias_ref[hp, tile[j][0]], bias_ref[hp, tile[j][1]]], axis=1))
            sl_buf[j * W:(j + 1) * W, :] = sj
            mj = jnp.max(sj, axis=0, keepdims=True)
            m = mj if m is None else jnp.maximum(m, mj)
        s_c = _dot_nt(kc_ref[:, lanes(hp)], qblk)
        sc_buf[...] = s_c
        return jnp.maximum(m, jnp.max(s_c, axis=0, keepdims=True))

    def pv(bufs, hp, m):
        sl_buf, sc_buf = bufs
        p_l = jnp.exp2(sl_buf[...] - m).astype(BF16)
        p_c = jnp.exp2(sc_buf[...] - m).astype(BF16)
        rows = pl.ds(pl.multiple_of(hp * LANES, LANES), LANES)
        return (_dot(jnp.concatenate([vtw_ref[rows, :], ones_l], axis=0), p_l)
                + _dot(jnp.concatenate([vtc_ref[rows, :], ones_c], axis=0), p_c))

    def store(acc, hp):
        ot = jnp.where(own_head, acc[:LANES] / acc[LANES:LANES + 1], 0.0)
        tr = ot.T
        for rho in range(2):
            blk = tr[rho * LANES:rho * LANES + W] + tr[rho * LANES + W:(rho + 1) * LANES]
            o_ref[rho * W:(rho + 1) * W, lanes(hp)] = blk.astype(o_ref.dtype)

    buf_a, buf_b = (sla_ref, sca_ref), (slb_ref, scb_ref)
    m0 = put(buf_a, 0)

    def body(i, carry):
        m_a, acc_prev = carry
        m_b = put(buf_b, 2 * i + 1)
        acc_a = pv(buf_a, 2 * i, m_a)
        store(acc_prev, jnp.maximum(2 * i - 1, 0))
        m_next = put(buf_a, 2 * i + 2)
        acc_b = pv(buf_b, 2 * i + 1, m_b)
        store(acc_a, 2 * i)
        return m_next, acc_b

    m_a, acc_prev = lax.fori_loop(0, npairs // 2 - 1, body, (m0, jnp.ones((LANES + 8, 4 * W), F32)))
    m_b = put(buf_b, npairs - 1)
    acc_a = pv(buf_a, npairs - 2, m_a)
    store(acc_prev, npairs - 3)
    acc_b = pv(buf_b, npairs - 1, m_b)
    store(acc_a, npairs - 2)
    store(acc_b, npairs - 1)


def _na_bias_table(rpb):
    H = rpb.shape[0]
    W, wr, wc = GRID_W, NA_WIN_ROWS, NA_WIN_COLS
    qcol = np.arange(W)[:, None]
    kcol = np.arange(W)[None, :]
    ws = np.clip(qcol - wc // 2, 0, W - wc)
    ok = (kcol >= ws) & (kcol < ws + wc)
    r_pad = jnp.pad(rpb.astype(F32), ((0, 0), (0, 0), (W - wc, W + wc - (2 * wc - 1))))
    skew = jnp.tile(r_pad, (1, 1, W))[:, :, :W * (2 * W - 1)].reshape(H, 2 * wr - 1, W, 2 * W - 1)
    t15 = jnp.where(ok[None, None], skew[:, :, :, W - 1:] * LOG2E, NEG_INF)
    t = t15.reshape(H // 2, 2, 2 * wr - 1, W, W).transpose(0, 2, 4, 1, 3).reshape(H // 2, 2 * wr - 1, W, 2 * W)
    return jnp.concatenate([t, jnp.full((H // 2, 1, W, 2 * W), NEG_INF, F32)], axis=1)


def _na_attention(qkv, vt, bias, B, L, Lc):
    D = qkv.shape[1] // 3
    W, wr, fr = GRID_W, NA_WIN_ROWS, NA_FRAME_ROWS
    nrows = L // W
    assert nrows % 2 == 0 and nrows >= fr
    frame0 = lambda g: jnp.minimum(jnp.clip(2 * g - wr // 2, 0, nrows - wr), nrows - fr)
    ctx_blk0 = B * L // Lc
    return pl.pallas_call(
        functools.partial(_na_kernel, heads=NA_HEADS, nrows=nrows),
        grid=(B, nrows // 2),
        in_specs=[pl.BlockSpec((2 * W, D), lambda b, g: (b * (nrows // 2) + g, 0)),
                  pl.BlockSpec((pl.Element(fr * W), pl.Element(D)),
                               lambda b, g: ((b * nrows + frame0(g)) * W, D)),
                  pl.BlockSpec((pl.Element(D), pl.Element(fr * W)),
                               lambda b, g: (0, pl.multiple_of((b * nrows + frame0(g)) * W, 2 * W))),
                  pl.BlockSpec((Lc, D), lambda b, g: (ctx_blk0 + b, 1)),
                  pl.BlockSpec((D, Lc), lambda b, g: (0, ctx_blk0 + b)),
                  _const_spec(bias.shape)],
        out_specs=pl.BlockSpec((2 * W, D), lambda b, g: (b * (nrows // 2) + g, 0)),
        out_shape=jax.ShapeDtypeStruct((B * L, D), BF16),
        scratch_shapes=[pltpu.VMEM((fr * W, 4 * W), F32)] * 2 + [pltpu.VMEM((Lc, 4 * W), F32)] * 2,
        compiler_params=_cparams(("arbitrary", "arbitrary")),
        name="na_attn",
    )(qkv, qkv, vt, qkv, vt, bias)


def _ctx_attn_kernel(q_ref, k_ref, v_ref, o_ref, *, heads, split64, q_scale):
    lo = _lane_lo((q_ref.shape[0], LANES))
    for hp in range(heads // 2):
        vsl = slice(hp * LANES, (hp + 1) * LANES)
        v2 = v_ref[:, vsl]
        outs = []
        for e in range(2):
            if split64:
                q2 = q_ref[:, vsl] * q_scale
                qh = jnp.where(lo if e == 0 else ~lo, q2, jnp.zeros_like(q2))
                kh = k_ref[:, vsl]
            else:
                hsl = slice((2 * hp + e) * LANES, (2 * hp + e + 1) * LANES)
                qh, kh = q_ref[:, hsl], k_ref[:, hsl]
            outs.append(_softmax_pv([_dot_nt(qh, kh)], [v2]))
        o_ref[:, vsl] = jnp.where(lo, outs[0], outs[1]).astype(o_ref.dtype)


def _ctx_attention(q_src, k_src, v_src, B, L, Lc, heads, split64, q_scale=1.0):
    blk0 = B * L // Lc
    spec = lambda src: pl.BlockSpec((Lc, src[1]), lambda b: (blk0 + b, src[2]))
    Dv = v_src[1]
    return pl.pallas_call(
        functools.partial(_ctx_attn_kernel, heads=heads, split64=split64, q_scale=q_scale),
        grid=(B,),
        in_specs=[spec(q_src), spec(k_src), spec(v_src)],
        out_specs=pl.BlockSpec((Lc, Dv), lambda b: (b, 0)),
        out_shape=jax.ShapeDtypeStruct((B * Lc, Dv), BF16),
        compiler_params=_cparams(("arbitrary",)),
        name="ctx_attn",
    )(q_src[0], k_src[0], v_src[0])


def _rope_slot(x, cos, sina, sinb):
    return x * cos + pltpu.roll(x, LANES - 16, axis=1) * sina + pltpu.roll(x, 16, axis=1) * sinb


def _mla_proj_kernel(x_ref, mod_ref, wd_ref, qn_ref, kvn_ref, wq_ref, wk_ref, wvt_ref, vone_ref,
                     cos_ref, sina_ref, sinb_ref, q_ref, k_ref, vt_ref, *, heads, scale):
    sh = mod_ref[0, 0:1, :]
    sc = mod_ref[0, 1:2, :]
    a = (x_ref[...] * (1.0 + sc) + sh).astype(BF16)
    d = _dot(a, wd_ref[...])
    cq = d[:, :MLA_Q_LORA]
    ckv = d[:, MLA_Q_LORA:MLA_Q_LORA + MLA_KV_LORA]
    kr = d[:, MLA_Q_LORA + MLA_KV_LORA:]
    rms = lambda t, g: (t * lax.rsqrt(jnp.mean(t * t, axis=-1, keepdims=True) + RMS_EPS) * g).astype(BF16)
    cqn = rms(cq, qn_ref[...])
    ckvn = rms(ckv, kvn_ref[...])
    cos, sina, sinb = cos_ref[...], sina_ref[...], sinb_ref[...]
    krr = _rope_slot(kr, cos, sina, sinb)
    qf = _dot(cqn, wq_ref[...])
    kf = _dot(ckvn, wk_ref[...])
    for h in range(heads):
        sl = slice(h * LANES, (h + 1) * LANES)
        q_ref[:, sl] = (_rope_slot(qf[:, sl], cos, sina, sinb) * scale).astype(q_ref.dtype)
        k_ref[:, sl] = (kf[:, sl] + krr).astype(k_ref.dtype)
    vt_ref[...] = (_dot_nt(wvt_ref[...], ckvn) + vone_ref[...]).astype(vt_ref.dtype)


def _mla_proj(rows, x, mod, wd, qn, kvn, wq, wk, wvt, vone, tables):
    R, D = x.shape
    tm = rows.tm
    H = MLA_HEADS
    tab_spec = pl.BlockSpec((tm, LANES), lambda i: (rows.pos_block(i), 0))
    return pl.pallas_call(
        functools.partial(_mla_proj_kernel, heads=H,
                          scale=(MLA_NOPE + MLA_ROPE) ** -0.5 * float(np.log2(np.e))),
        grid=(rows.n_all,),
        in_specs=[pl.BlockSpec((tm, D), lambda i: (i, 0)),
                  pl.BlockSpec((1, 6, D), lambda i: (rows.group(i), 0, 0)),
                  _const_spec(wd.shape), _const_spec((1, MLA_Q_LORA)), _const_spec((1, MLA_KV_LORA)),
                  _const_spec(wq.shape), _const_spec(wk.shape), _const_spec(wvt.shape),
                  _const_spec(vone.shape), tab_spec, tab_spec, tab_spec],
        out_specs=[pl.BlockSpec((tm, H * LANES), lambda i: (i, 0)),
                   pl.BlockSpec((tm, H * LANES), lambda i: (i, 0)),
                   pl.BlockSpec((H * LANES, tm), lambda i: (0, i))],
        out_shape=[jax.ShapeDtypeStruct((R, H * LANES), BF16),
                   jax.ShapeDtypeStruct((R, H * LANES), BF16),
                   jax.ShapeDtypeStruct((H * LANES, R), BF16)],
        compiler_params=_cparams(("arbitrary",)),
        name="mla_proj",
    )(x, mod, wd, qn.reshape(1, -1), kvn.reshape(1, -1), wq, wk, wvt, vone, *tables)


def _mla_flash_kernel(q_ref, kc_ref, vtc_ref, *rest, tk, cpi, with_latent):
    if with_latent:
        k_ref, vt_ref, o_ref = rest[:3]
        bufs = rest[3:]
    else:
        (o_ref,) = rest
    tq = q_ref.shape[0]
    hsl = [slice(e * LANES, (e + 1) * LANES) for e in range(2)]
    qs = [q_ref[:, sl] for sl in hsl]

    def scores(kblk):
        return [_dot_nt(kblk[:, hsl[e]], qs[e]) for e in range(2)]

    vrows = MLA_V + 8
    qc = min(tq, 512)

    def colmax(ss):
        return [jnp.max(s, axis=0, keepdims=True) for s in ss]

    def update(ss, cms, vtblk, state):
        out = []
        for e in range(2):
            m, acc = state[e]
            m_new = jnp.maximum(m, cms[e])
            alpha = jnp.exp2(m - m_new)
            vte = vtblk[e * LANES:e * LANES + vrows, :]
            parts = []
            for j in range(0, tq, qc):
                sl = slice(j, j + qc)
                p = jnp.exp2(ss[e][:, sl] - m_new[:, sl]).astype(BF16)
                parts.append(alpha[:, sl] * acc[:, sl] + _dot(vte, p))
            acc = parts[0] if len(parts) == 1 else jnp.concatenate(parts, axis=1)
            out.append((m_new, acc))
        return tuple(out)

    init = (jnp.full((1, tq), NEG_INF, F32), jnp.zeros((vrows, tq), F32))
    s_ctx = scores(kc_ref[...])
    if not with_latent:
        state = update(s_ctx, colmax(s_ctx), vtc_ref[...], (init, init))
    else:
        n = k_ref.shape[0] // tk
        nb = len(bufs)
        assert cpi % nb == 0 and n % cpi == 0

        def put(buf, c):
            ss = scores(k_ref[pl.ds(pl.multiple_of(c * tk, tk), tk), :])
            for e in range(2):
                buf[e] = ss[e]
            return tuple(colmax(ss))

        def group(c0, carry, last):
            st, cms = carry
            for t in range(cpi):
                nxt = None
                if not (last and t == cpi - 1):
                    nxt = put(bufs[(t + 1) % nb], c0 + t + 1)
                buf = bufs[t % nb]
                vtblk = vt_ref[:, pl.ds(pl.multiple_of((c0 + t) * tk, tk), tk)]
                st = update([buf.at[0], buf.at[1]], cms, vtblk, st)
                cms = nxt
            return st, cms

        cms = put(bufs[0], 0)
        state = update(s_ctx, colmax(s_ctx), vtc_ref[...], (init, init))
        carry = lax.fori_loop(0, n // cpi - 1, lambda i, cr: group(i * cpi, cr, False), (state, cms))
        state, _ = group(n - cpi, carry, True)
    ot = jnp.concatenate([acc[:MLA_V] / acc[MLA_V:MLA_V + 1] for _, acc in state], axis=0)
    o_ref[...] = ot.T.astype(o_ref.dtype)


def _mla_attention(q, k, vt, B, L, Lc, tq, tk, latent_queries):
    H = MLA_HEADS
    ctx_blk0 = B * L // Lc
    if latent_queries:
        nq, q0, n_out = L // tq, 0, B * L
    else:
        assert tq == Lc
        nq, q0, n_out = 1, ctx_blk0, B * Lc
    in_specs = [pl.BlockSpec((tq, 2 * LANES), lambda b, hp, i: (q0 + b * nq + i, hp)),
                pl.BlockSpec((Lc, 2 * LANES), lambda b, hp, i: (ctx_blk0 + b, hp)),
                pl.BlockSpec((2 * LANES, Lc), lambda b, hp, i: (hp, ctx_blk0 + b))]
    args = [q, k, vt]
    scratch = []
    if latent_queries:
        in_specs += [pl.BlockSpec((L, 2 * LANES), lambda b, hp, i: (b, hp)),
                     pl.BlockSpec((2 * LANES, L), lambda b, hp, i: (hp, b))]
        args += [k, vt]
        scratch = [pltpu.VMEM((2, tk, tq), F32)] * 2
    return pl.pallas_call(
        functools.partial(_mla_flash_kernel, tk=tk, cpi=4 if (L // tk) % 4 == 0 and L // tk > 8 else 2,
                          with_latent=latent_queries),
        grid=(B, H // 2, nq),
        in_specs=in_specs,
        out_specs=pl.BlockSpec((tq, LANES), lambda b, hp, i: (b * nq + i, hp)),
        out_shape=jax.ShapeDtypeStruct((n_out, H * MLA_V), BF16),
        scratch_shapes=scratch,
        compiler_params=_cparams(("arbitrary", "arbitrary", "arbitrary")),
        name="mla_flash" if latent_queries else "mla_flash_ctx",
    )(*args)


def _axial_angles(L, rot_dim):
    t = jnp.arange(L)
    rows = (t // GRID_W).astype(F32)
    cols = (t % GRID_W).astype(F32)
    n_freq = rot_dim // 4
    inv = ROPE_BASE ** (-jnp.arange(n_freq, dtype=F32) / n_freq)
    return jnp.concatenate([rows[:, None] * inv, cols[:, None] * inv], -1)


def _ret_rope_tables(L, tm):
    ang = _axial_angles(L, 256)
    cos = jnp.concatenate([jnp.cos(ang), jnp.ones((tm, LANES), F32)], 0)
    sin = jnp.concatenate([jnp.sin(ang), jnp.zeros((tm, LANES), F32)], 0)
    return cos, sin


def _mla_rope_tables(L, tm):
    ang = _axial_angles(L, MLA_ROPE)
    c, s = jnp.cos(ang), jnp.sin(ang)
    one = jnp.ones((L, MLA_NOPE), F32)
    z16 = jnp.zeros((L, 16), F32)
    z32 = jnp.zeros((L, 32), F32)
    z64 = jnp.zeros((L, MLA_NOPE), F32)
    cos = jnp.concatenate([one, c, c, jnp.ones((L, 32), F32)], -1)
    sina = jnp.concatenate([z64, -s, z16, z32], -1)
    sinb = jnp.concatenate([z64, z16, s, z32], -1)
    ident = lambda t, fill: jnp.concatenate([t, jnp.full((tm, LANES), fill, F32)], 0)
    return ident(cos, 1.0), ident(sina, 0.0), ident(sinb, 0.0)


def kernel(x, c, ctx, c_ctx, ada_w, ada_b, ln_g, ln_b, ffn_w13, ffn_w2, ret_w_in, ret_decay, ret_w_out, na_w_qkv, na_rpb, na_w_out, mla_w_down, mla_q_norm, mla_kv_norm, mla_w_uq, mla_w_ukv, mla_w_out, hg_w_in, hg_lower_bounds, hg_norm_g, hg_w_out):
    B, L, D = x.shape
    Lc = ctx.shape[1]
    depth = ada_w.shape[0]
    alpha = (2 * depth) ** 0.25
    tm = 512 if (B * Lc) % 512 == 0 else 256
    rows = _Rows(B, L, Lc, tm)
    n_lat_rows = B * L

    h = (x.reshape(B * L, D), ctx.reshape(B * Lc, D))

    G = 8 * (-(-(B + 1) // 8))
    cond_in = jnp.zeros((G, D), F32).at[0].set(c_ctx).at[1:B + 1].set(c)
    mods = _adaln(cond_in, ada_w, ada_b).reshape(depth, G, 6, D)

    bf = lambda w: w.astype(BF16)
    H = MLA_HEADS
    wd = jnp.zeros((D, MLA_Q_LORA + MLA_KV_LORA + LANES), F32)
    wd = wd.at[:, :MLA_Q_LORA + MLA_KV_LORA].set(mla_w_down[:, :MLA_Q_LORA + MLA_KV_LORA])
    wd = wd.at[:, MLA_Q_LORA + MLA_KV_LORA + MLA_NOPE:MLA_Q_LORA + MLA_KV_LORA + MLA_NOPE + MLA_ROPE].set(
        mla_w_down[:, MLA_Q_LORA + MLA_KV_LORA:])
    wq = jnp.pad(mla_w_uq.reshape(MLA_Q_LORA, H, MLA_NOPE + MLA_ROPE),
                 ((0, 0), (0, 0), (0, LANES - MLA_NOPE - MLA_ROPE))).reshape(MLA_Q_LORA, H * LANES)
    wukv = mla_w_ukv.reshape(MLA_KV_LORA, H, MLA_NOPE + MLA_V)
    wk = jnp.pad(wukv[:, :, :MLA_NOPE], ((0, 0), (0, 0), (0, LANES - MLA_NOPE))).reshape(MLA_KV_LORA, H * LANES)
    wvt = jnp.pad(wukv[:, :, MLA_NOPE:], ((0, 0), (0, 0), (0, LANES - MLA_V))).reshape(MLA_KV_LORA, H * LANES).T
    vone = jnp.tile((jnp.arange(LANES) == MLA_V).astype(F32), H).reshape(H * LANES, 1)

    plain = lambda n, width=1024: [(c0, width, "plain", 0, c0, 1.0, None) for c0 in range(0, n, width)]
    for i in range(depth):
        mod = mods[i]
        kind = i % 4
        n_tiles = rows.n_all if i < depth - 1 else rows.n_lat
        post = functools.partial(_post, rows, h, mod, ln_g=ln_g[i], ln_b=ln_b[i], w13=bf(ffn_w13[i]),
                                 w2=bf(ffn_w2[i]), alpha=alpha, n_tiles=n_tiles)
        if kind == 0:
            lg = -jnp.exp(ret_decay.astype(F32))
            plan = [(0, 1024, "rope", 0, 0, 1.0, None), (1024, 1024, "rope", 0, 1024, 256 ** -0.5, None)]
            plan += plain(6144)[2:]
            (proj,) = _proj(rows, h, mod, bf(ret_w_in), plan, [(6144, BF16, False)],
                            rope=_ret_rope_tables(L, tm))
            o_f, o_b = _ret_scan(proj, lg, B, L, Lc)
            h = post(mix=(o_f, o_b, (proj, 2)), w_out=bf(ret_w_out), scan_heads=RET_HEADS)
        elif kind == 1:
            plan = plain(2048) + [(2048, 1024, "plain", 0, 2048, 1.0, 1)]
            qkv, vt = _proj(rows, h, mod, bf(na_w_qkv), plan, [(3072, BF16, False), (1024, BF16, True)])
            o_lat = _na_attention(qkv, vt, _na_bias_table(na_rpb), B, L, Lc)
            o_ctx = _ctx_attention((qkv, D, 0), (qkv, D, 1), (qkv, D, 2), B, L, Lc, NA_HEADS,
                                   split64=True, q_scale=0.125)
            h = post(mix=(o_lat, o_ctx), w_out=bf(na_w_out))
        elif kind == 2:
            q, k, vt = _mla_proj(rows, h, mod, bf(wd), mla_q_norm, mla_kv_norm, bf(wq), bf(wk), bf(wvt), vone,
                                 _mla_rope_tables(L, tm))
            o_lat = _mla_attention(q, k, vt, B, L, Lc, min(2048, L), 512, latent_queries=True)
            o_ctx = _mla_attention(q, k, vt, B, L, Lc, Lc, 256, latent_queries=False)
            h = post(mix=(o_lat, o_ctx), w_out=bf(mla_w_out))
        else:
            lb_soft = jax.nn.softmax(hg_lower_bounds.astype(F32), axis=0)
            lb = (jnp.cumsum(lb_soft, axis=0) - lb_soft[0])[i]
            plan = [(0, 1024, "plain", 0, 0, 1.0, None), (1024, 1024, "plain", 1, 0, 1.0, None),
                    (2048, 1024, "plain", 1, 1024, 1.0, None), (3072, 1024, "plain", 0, 1024, 1.0, None),
                    (4096, 1024, "plain", 0, 2048, 1.0, None)]
            qig, ff = _proj(rows, h, mod, bf(hg_w_in), plan, [(3072, BF16, False), (2048, F32, False)])
            o_f, o_b = _hg_scan(qig, ff, lb, B, L, Lc)
            h = post(mix=(o_f, o_b, (qig, 2)), w_out=bf(hg_w_out), scan_heads=D // HG_EXPAND,
                     norm_g=hg_norm_g)
    return h[:n_lat_rows].reshape(B, L, D)
```

```python
import functools

import numpy as np
import jax
import jax.numpy as jnp
from jax import lax
from jax.experimental import pallas as pl
from jax.experimental.pallas import tpu as pltpu

F32 = jnp.float32
BF16 = jnp.bfloat16

GRID_W = 64
LN_EPS = 1e-5
RMS_EPS = 1e-6
ROPE_BASE = 10000.0
NEG_INF = -1e30

RET_HEADS = 4
NA_HEADS = 16
NA_WIN_ROWS = 8
NA_WIN_COLS = 16
MLA_HEADS = 16
MLA_NOPE = 64
MLA_ROPE = 32
MLA_V = 64
MLA_Q_LORA = 512
MLA_KV_LORA = 256
HG_EXPAND = 128
HG_CHUNK = 64
RET_CHUNK = 256

VMEM_LIMIT = 56 * 1024 * 1024
LANES = 128


def _cparams(sem):
    return pltpu.CompilerParams(dimension_semantics=sem, vmem_limit_bytes=VMEM_LIMIT)


def _const_spec(shape):
    nd = len(shape)
    return pl.BlockSpec(shape, lambda *_: (0,) * nd, pipeline_mode=pl.Buffered(1))


def _dot(a, b):
    return jnp.dot(a, b, preferred_element_type=F32)


def _dot_nt(a, b):
    return lax.dot_general(a, b, (((1,), (1,)), ((), ())), preferred_element_type=F32)


def _dot_tn(a, b):
    return lax.dot_general(a, b, (((0,), (0,)), ((), ())), preferred_element_type=F32)


def _silu(x):
    return x * jax.nn.sigmoid(x)


def _layer_norm(r, g, b):
    mu = jnp.mean(r, axis=-1, keepdims=True)
    rc = r - mu
    var = jnp.mean(rc * rc, axis=-1, keepdims=True)
    return rc * lax.rsqrt(var + LN_EPS) * g + b


def _pick_rows(lat_ref, ctx_ref, scr, n_lat):
    is_lat = pl.program_id(0) < n_lat

    @pl.when(is_lat)
    def _():
        scr[...] = lat_ref[...]

    @pl.when(jnp.logical_not(is_lat))
    def _():
        scr[...] = ctx_ref[...]

    return scr[...]


class _Rows:
    def __init__(self, B, L, Lc, tm):
        assert L % tm == 0 and (B * Lc) % tm == 0
        self.B, self.L, self.Lc, self.tm = B, L, Lc, tm
        self.n_lat = B * L // tm
        self.n_all = (B * L + B * Lc) // tm
        self.per_b = L // tm

    def group(self, i):
        return jnp.where(i < self.n_lat, 1 + i // self.per_b, 0)

    def pos_block(self, i):
        return jnp.where(i < self.n_lat, i % self.per_b, self.per_b)

    def pair_specs(self, cols):
        return [pl.BlockSpec((self.tm, cols), lambda i: (jnp.minimum(i, self.n_lat - 1), 0)),
                pl.BlockSpec((self.tm, cols), lambda i: (jnp.maximum(i - self.n_lat, 0), 0))]


def _adaln_kernel(c_ref, w_ref, b_ref, o_ref):
    cond = _silu(c_ref[...])
    o_ref[0] = jnp.dot(cond, w_ref[0], preferred_element_type=F32,
                       precision=lax.Precision.HIGHEST) + b_ref[0]


def _adaln(cond_in, ada_w, ada_b):
    depth, D, N = ada_w.shape
    G = cond_in.shape[0]
    tn = 1536
    return pl.pallas_call(
        _adaln_kernel,
        grid=(depth, N // tn),
        in_specs=[pl.BlockSpec((G, D), lambda l, j: (0, 0)),
                  pl.BlockSpec((1, D, tn), lambda l, j: (l, 0, j)),
                  pl.BlockSpec((1, 1, tn), lambda l, j: (l, 0, j))],
        out_specs=pl.BlockSpec((1, G, tn), lambda l, j: (l, 0, j)),
        out_shape=jax.ShapeDtypeStruct((depth, G, N), F32),
        compiler_params=_cparams(("arbitrary", "arbitrary")),
        name="adaln",
    )(cond_in, ada_w, ada_b.reshape(depth, 1, N))


def _proj_kernel(x_ref, mod_ref, w_ref, *rest, plan, rope, x_pair, n_lat):
    x = x_ref[...]
    if x_pair:
        x = _pick_rows(x_ref, rest[0], rest[-1], n_lat)
        rest = rest[1:-1]
    if rope:
        cos_ref, sin_ref = rest[:2]
        outs = rest[2:]
    else:
        outs = rest
    sh = mod_ref[0, 0:1, :]
    sc = mod_ref[0, 1:2, :]
    a = (x * (1.0 + sc) + sh).astype(BF16)
    for c0, width, kind, oi, oc0, scale, t_out in plan:
        acc = _dot(a, w_ref[:, c0:c0 + width])
        o_ref = outs[oi]
        if kind == "rope":
            cos = cos_ref[...]
            sin = sin_ref[...]
            for h in range(width // (2 * LANES)):
                x1 = acc[:, h * 256:h * 256 + LANES]
                x2 = acc[:, h * 256 + LANES:(h + 1) * 256]
                lo = oc0 + h * 256
                o_ref[:, lo:lo + LANES] = ((x1 * cos - x2 * sin) * scale).astype(o_ref.dtype)
                o_ref[:, lo + LANES:lo + 256] = ((x1 * sin + x2 * cos) * scale).astype(o_ref.dtype)
        else:
            o_ref[:, oc0:oc0 + width] = acc.astype(o_ref.dtype)
        if t_out is not None:
            outs[t_out][...] = acc.T.astype(outs[t_out].dtype)


def _proj(rows, x, mod, w, plan, outs, rope=None):
    x_pair = isinstance(x, tuple)
    D = w.shape[0]
    tm = rows.tm
    R = rows.n_all * tm
    mod_spec = pl.BlockSpec((1, 6, D), lambda i: (rows.group(i), 0, 0))
    if x_pair:
        lat_spec, ctx_spec = rows.pair_specs(D)
        in_specs = [lat_spec, mod_spec, _const_spec(w.shape), ctx_spec]
        args = [x[0], mod, w, x[1]]
    else:
        in_specs = [pl.BlockSpec((tm, D), lambda i: (i, 0)), mod_spec, _const_spec(w.shape)]
        args = [x, mod, w]
    if rope is not None:
        in_specs += [pl.BlockSpec((tm, LANES), lambda i: (rows.pos_block(i), 0))] * 2
        args += list(rope)
    out_specs, out_shape = [], []
    for cols, dtype, transposed in outs:
        if transposed:
            out_specs.append(pl.BlockSpec((cols, tm), lambda i: (0, i)))
            out_shape.append(jax.ShapeDtypeStruct((cols, R), dtype))
        else:
            out_specs.append(pl.BlockSpec((tm, cols), lambda i: (i, 0)))
            out_shape.append(jax.ShapeDtypeStruct((R, cols), dtype))
    return pl.pallas_call(
        functools.partial(_proj_kernel, plan=tuple(plan), rope=rope is not None, x_pair=x_pair,
                          n_lat=rows.n_lat),
        grid=(rows.n_all,),
        in_specs=in_specs,
        out_specs=out_specs,
        out_shape=out_shape,
        scratch_shapes=[pltpu.VMEM((tm, D), F32)] if x_pair else [],
        compiler_params=_cparams(("arbitrary",)),
        name="proj",
    )(*args)


def _post_kernel(x_ref, mod_ref, *rest, scan_heads, has_norm_g, alpha, fc, n_lat, x_pair):
    x = x_ref[...]
    if x_pair:
        x = _pick_rows(x_ref, rest[0], rest[-1], n_lat)
        rest = rest[1:-1]
    if scan_heads:
        of_ref, ob_ref, g_ref = rest[:3]
        rest = rest[3:]
        if has_norm_g:
            ng_ref, rest = rest[0], rest[1:]
        t = of_ref[...].astype(F32) + ob_ref[...].astype(F32)
        dh = t.shape[1] // scan_heads
        parts = []
        for h in range(scan_heads):
            th = t[:, h * dh:(h + 1) * dh]
            yh = th * lax.rsqrt(jnp.mean(th * th, axis=-1, keepdims=True) + RMS_EPS)
            parts.append(yh * ng_ref[...] if has_norm_g else yh)
        mix = (jnp.concatenate(parts, axis=1) * _silu(g_ref[...].astype(F32))).astype(BF16)
    else:
        mix = _pick_rows(rest[0], rest[1], rest[-1], n_lat)
        rest = rest[2:-1]
    wo_ref, ln1g_ref, ln1b_ref, w13_ref, w2_ref, ln2g_ref, ln2b_ref, o_ref = rest
    r = alpha * x + mod_ref[0, 2:3, :] * _dot(mix, wo_ref[...])
    h1 = _layer_norm(r, ln1g_ref[...], ln1b_ref[...])
    a = (h1 * (1.0 + mod_ref[0, 4:5, :]) + mod_ref[0, 3:4, :]).astype(BF16)
    F = w2_ref.shape[0]
    acc = jnp.zeros(h1.shape, F32)
    for c in range(F // fc):
        g = _dot(a, w13_ref[:, c * fc:(c + 1) * fc])
        u = _dot(a, w13_ref[:, F + c * fc:F + (c + 1) * fc])
        acc = acc + _dot((_silu(g) * u).astype(BF16), w2_ref[c * fc:(c + 1) * fc, :])
    r2 = alpha * h1 + mod_ref[0, 5:6, :] * acc
    o_ref[...] = _layer_norm(r2, ln2g_ref[...], ln2b_ref[...])


def _post(rows, x, mod, mix, w_out, ln_g, ln_b, w13, w2, alpha, n_tiles, scan_heads=0, norm_g=None):
    x_pair = isinstance(x, tuple)
    K, D = w_out.shape
    F = w2.shape[0]
    tm = rows.tm
    row = lambda cols, blk=0: pl.BlockSpec((tm, cols), lambda i: (i, blk))
    mod_spec = pl.BlockSpec((1, 6, D), lambda i: (rows.group(i), 0, 0))
    if x_pair:
        lat_spec, ctx_spec = rows.pair_specs(D)
        in_specs = [lat_spec, mod_spec, ctx_spec]
        args = [x[0], mod, x[1]]
    else:
        in_specs = [row(D), mod_spec]
        args = [x, mod]
    scratch = []
    if scan_heads:
        o_f, o_b, (g_arr, g_blk) = mix
        in_specs += [row(K), row(K), row(K, g_blk)]
        args += [o_f, o_b, g_arr]
        if norm_g is not None:
            in_specs.append(_const_spec((1, norm_g.shape[0])))
            args.append(norm_g.reshape(1, -1))
    else:
        in_specs += rows.pair_specs(K)
        args += list(mix)
        scratch = [pltpu.VMEM((tm, K), BF16)]
    vec = lambda v: v.reshape(1, D)
    in_specs += [_const_spec((K, D)), _const_spec((1, D)), _const_spec((1, D)),
                 _const_spec((D, 2 * F)), _const_spec((F, D)), _const_spec((1, D)), _const_spec((1, D))]
    args += [w_out, vec(ln_g[0]), vec(ln_b[0]), w13, w2, vec(ln_g[1]), vec(ln_b[1])]
    return pl.pallas_call(
        functools.partial(_post_kernel, scan_heads=scan_heads, has_norm_g=norm_g is not None,
                          alpha=alpha, fc=256, n_lat=rows.n_lat, x_pair=x_pair),
        grid=(n_tiles,),
        in_specs=in_specs,
        out_specs=row(D),
        out_shape=jax.ShapeDtypeStruct((n_tiles * tm, D), F32),
        scratch_shapes=scratch + ([pltpu.VMEM((tm, D), F32)] if x_pair else []),
        compiler_params=_cparams(("arbitrary",)),
        name="post",
    )(*args)


def _chunk_order(B, L, Lc, C, backward):
    ncc, ncl = Lc // C, L // C

    def blk(b, j):
        if backward:
            in_ctx = j < ncc
            return jnp.where(in_ctx, B * ncl + b * ncc + (ncc - 1 - j), b * ncl + (ncl - 1 - (j - ncc)))
        return jnp.where(j < ncc, B * ncl + b * ncc + j, b * ncl + (j - ncc))

    return blk, ncc + ncl


def _ret_scan_kernel(lg_ref, qf_ref, kf_ref, vf_ref, qb_ref, kb_ref, vb_ref, of_ref, ob_ref, st_scr,
                     *, C, heads):
    j = pl.program_id(1)

    @pl.when(j == 0)
    def _():
        st_scr[...] = jnp.zeros_like(st_scr)

    dk = qf_ref.shape[1] // heads
    dv = vf_ref.shape[1] // heads
    ti = lax.broadcasted_iota(jnp.int32, (C, C), 0)
    si = lax.broadcasted_iota(jnp.int32, (C, C), 1)
    tcol = lax.broadcasted_iota(jnp.int32, (C, 1), 0)
    for d, (q_ref, k_ref, v_ref, o_ref) in enumerate(((qf_ref, kf_ref, vf_ref, of_ref),
                                                      (qb_ref, kb_ref, vb_ref, ob_ref))):
        backward = d == 1
        dist = (si - ti) if backward else (ti - si)
        keep = dist >= 0
        fdist = jnp.maximum(dist, 0).astype(F32)
        eq = ((C - tcol) if backward else (tcol + 1)).astype(F32)
        ek = (tcol if backward else (C - 1 - tcol)).astype(F32)
        for h in range(heads):
            lg = lg_ref[d, h]
            q = q_ref[:, h * dk:(h + 1) * dk]
            k = k_ref[:, h * dk:(h + 1) * dk]
            v = v_ref[:, h * dv:(h + 1) * dv]
            decay = jnp.where(keep, jnp.exp(lg * fdist), 0.0)
            att = (_dot_nt(q, k) * decay).astype(BF16)
            st = st_scr[d, h]
            o = _dot(att, v) + _dot_nt(q, st.astype(BF16)) * jnp.exp(lg * eq)
            kd = (k.astype(F32) * jnp.exp(lg * ek)).astype(BF16)
            st_scr[d, h] = st * jnp.exp(lg * jnp.full((1, 1), float(C), F32)) + _dot_tn(v, kd)
            o_ref[:, h * dv:(h + 1) * dv] = o.astype(o_ref.dtype)


def _ret_scan(proj, lg, B, L, Lc):
    R = proj.shape[0]
    C = RET_CHUNK
    H, dk, dv = RET_HEADS, 256, 512
    blk_f, nch = _chunk_order(B, L, Lc, C, False)
    blk_b, _ = _chunk_order(B, L, Lc, C, True)
    specs = lambda blk: [pl.BlockSpec((C, H * dk), lambda b, j: (blk(b, j), 0)),
                         pl.BlockSpec((C, H * dk), lambda b, j: (blk(b, j), 1)),
                         pl.BlockSpec((C, H * dv), lambda b, j: (blk(b, j), 1))]
    return pl.pallas_call(
        functools.partial(_ret_scan_kernel, C=C, heads=H),
        grid=(B, nch),
        in_specs=[pl.BlockSpec(memory_space=pltpu.SMEM)] + specs(blk_f) + specs(blk_b),
        out_specs=[pl.BlockSpec((C, H * dv), lambda b, j: (blk_f(b, j), 0)),
                   pl.BlockSpec((C, H * dv), lambda b, j: (blk_b(b, j), 0))],
        out_shape=[jax.ShapeDtypeStruct((R, H * dv), BF16)] * 2,
        scratch_shapes=[pltpu.VMEM((2, H, dv, dk), F32)],
        compiler_params=_cparams(("arbitrary", "arbitrary")),
        name="ret_scan",
    )(lg, proj, proj, proj, proj, proj, proj)


def _cumsum_rows(x):
    n = x.shape[0]
    r = lax.broadcasted_iota(jnp.int32, (n, 1), 0)
    s = 1
    while s < n:
        x = x + jnp.where(r >= s, pltpu.roll(x, s, axis=0), 0.0)
        s *= 2
    return x


def _hg_scan_kernel(qf_ref, ff_ref, vf_ref, qb_ref, fb_ref, vb_ref, lb_ref, of_ref, ob_ref, st_scr,
                    *, C, sub, heads, scale):
    j = pl.program_id(1)

    @pl.when(j == 0)
    def _():
        st_scr[...] = jnp.zeros_like(st_scr)

    lb = lb_ref[...]
    ti = lax.broadcasted_iota(jnp.int32, (C, C), 0)
    si = lax.broadcasted_iota(jnp.int32, (C, C), 1)
    d = HG_EXPAND
    for dr, (q_ref, f_ref, v_ref, o_ref) in enumerate(((qf_ref, ff_ref, vf_ref, of_ref),
                                                       (qb_ref, fb_ref, vb_ref, ob_ref))):
        backward = dr == 1
        keep = (si >= ti) if backward else (ti >= si)
        for c in (range(sub - 1, -1, -1) if backward else range(sub)):
            rs = slice(c * C, (c + 1) * C)
            forget = lb + (1.0 - lb) * jax.nn.sigmoid(f_ref[rs, :])
            kk = 1.0 - forget
            gl = jnp.log(forget)
            pre = _cumsum_rows(gl)
            tot = pre[C - 1:C, :]
            bc = (tot - pre + gl) if backward else pre
            bc2 = bc * LOG2E
            tot2 = tot * LOG2E
            qd = (_silu(q_ref[rs, :].astype(F32)) * scale * jnp.exp2(bc2)).astype(BF16)
            kd = (kk * jnp.exp2(-bc2)).astype(BF16)
            ke = (kk * jnp.exp2(tot2 - bc2)).astype(BF16)
            v = v_ref[rs, :]
            etot = jnp.exp2(tot2)
            for h in range(heads):
                sl = slice(h * d, (h + 1) * d)
                att = jnp.where(keep, _dot_nt(qd[:, sl], kd[:, sl]), 0.0).astype(BF16)
                st = st_scr[dr, h]
                o = _dot(att, v[:, sl]) + _dot_nt(qd[:, sl], st.astype(BF16))
                st_scr[dr, h] = st * etot[:, sl] + _dot_tn(v[:, sl], ke[:, sl])
                o_ref[rs, sl] = o.astype(o_ref.dtype)


def _hg_scan(qig, ff, lb, B, L, Lc):
    R = qig.shape[0]
    sub = 4
    C = HG_CHUNK * sub
    Dm = qig.shape[1] // 3
    heads = Dm // HG_EXPAND
    blk_f, nch = _chunk_order(B, L, Lc, C, False)
    blk_b, _ = _chunk_order(B, L, Lc, C, True)
    specs = lambda blk, fcol: [pl.BlockSpec((C, Dm), lambda b, j: (blk(b, j), 0)),
                               pl.BlockSpec((C, Dm), lambda b, j: (blk(b, j), fcol)),
                               pl.BlockSpec((C, Dm), lambda b, j: (blk(b, j), 1))]
    return pl.pallas_call(
        functools.partial(_hg_scan_kernel, C=HG_CHUNK, sub=sub, heads=heads, scale=HG_EXPAND ** -0.5),
        grid=(B, nch),
        in_specs=specs(blk_f, 0) + specs(blk_b, 1) + [pl.BlockSpec((1, Dm), lambda b, j: (0, 0))],
        out_specs=[pl.BlockSpec((C, Dm), lambda b, j: (blk_f(b, j), 0)),
                   pl.BlockSpec((C, Dm), lambda b, j: (blk_b(b, j), 0))],
        out_shape=[jax.ShapeDtypeStruct((R, Dm), BF16)] * 2,
        scratch_shapes=[pltpu.VMEM((2, heads, HG_EXPAND, HG_EXPAND), F32)],
        compiler_params=_cparams(("arbitrary", "arbitrary")),
        name="hg_scan",
    )(qig, ff, qig, qig, ff, qig, lb.reshape(1, Dm))


def _softmax_pv(scores, values):
    m = functools.reduce(jnp.maximum, [jnp.max(s, axis=-1, keepdims=True) for s in scores])
    ps = [jnp.exp(s - m) for s in scores]
    l = functools.reduce(lambda a, b: a + b, [jnp.sum(p, axis=-1, keepdims=True) for p in ps])
    o = functools.reduce(lambda a, b: a + b, [_dot(p.astype(BF16), v) for p, v in zip(ps, values)])
    return o / l


def _lane_lo(shape):
    return lax.broadcasted_iota(jnp.int32, shape, 1) < (LANES // 2)


NA_FRAME_ROWS = NA_WIN_ROWS + 2
LOG2E = float(np.log2(np.e))
NA_Q_SCALE = 0.125 * LOG2E


def _na_kernel(q_ref, kw_ref, vtw_ref, kc_ref, vtc_ref, bias_ref, o_ref, sla_ref, slb_ref, sca_ref, scb_ref,
               *, heads, nrows):
    W, wr, fr = GRID_W, NA_WIN_ROWS, NA_FRAME_ROWS
    r0 = 2 * pl.program_id(1)
    u = jnp.minimum(jnp.clip(r0 - wr // 2, 0, nrows - wr), nrows - fr)
    tile = []
    for j in range(fr):
        per = []
        for rho in range(2):
            r = r0 + rho
            rs = jnp.clip(r - wr // 2, 0, nrows - wr)
            ok = (u + j >= rs) & (u + j < rs + wr)
            per.append(jnp.where(ok, u + j - r + wr - 1, 2 * wr - 1))
        tile.append(per)
    lo = _lane_lo((2 * W, LANES))
    ones_l = jnp.ones((8, fr * W), BF16)
    ones_c = jnp.ones((8, kc_ref.shape[0]), BF16)
    ri = lax.broadcasted_iota(jnp.int32, (LANES, 4 * W), 0)
    li = lax.broadcasted_iota(jnp.int32, (LANES, 4 * W), 1)
    own_head = (ri >= W) == ((li % LANES) >= W)
    npairs = heads // 2
    lanes = lambda hp: pl.ds(pl.multiple_of(hp * LANES, LANES), LANES)

    def put(bufs, hp):
        sl_buf, sc_buf = bufs
        q2 = q_ref[:, lanes(hp)] * NA_Q_SCALE
        qlo = jnp.where(lo, q2, jnp.zeros_like(q2))
        qhi = jnp.where(lo, jnp.zeros_like(q2), q2)
        qblk = jnp.concatenate([qlo[:W], qhi[:W], qlo[W:], qhi[W:]], axis=0)
        s_raw = _dot_nt(kw_ref[:, lanes(hp)], qblk)
        m = None
        for j in range(fr):
            sj = (s_raw[j * W:(j + 1) * W]
                  + jnp.concatenate([bias_ref[hp, tile[j][0]], bias_ref[hp, tile[j][1]]], axis=1))
            sl_buf[j * W:(j + 1) * W, :] = sj
            mj = jnp.max(sj, axis=0, keepdims=True)
            m = mj if m is None else jnp.maximum(m, mj)
        s_c = _dot_nt(kc_ref[:, lanes(hp)], qblk)
        sc_buf[...] = s_c
        return jnp.maximum(m, jnp.max(s_c, axis=0, keepdims=True))

    def pv(bufs, hp, m):
        sl_buf, sc_buf = bufs
        p_l = jnp.exp2(sl_buf[...] - m).astype(BF16)
        p_c = jnp.exp2(sc_buf[...] - m).astype(BF16)
        rows = pl.ds(pl.multiple_of(hp * LANES, LANES), LANES)
        return (_dot(jnp.concatenate([vtw_ref[rows, :], ones_l], axis=0), p_l)
                + _dot(jnp.concatenate([vtc_ref[rows, :], ones_c], axis=0), p_c))

    def store(acc, hp):
        ot = jnp.where(own_head, acc[:LANES] / acc[LANES:LANES + 1], 0.0)
        tr = ot.T
        for rho in range(2):
            blk = tr[rho * LANES:rho * LANES + W] + tr[rho * LANES + W:(rho + 1) * LANES]
            o_ref[rho * W:(rho + 1) * W, lanes(hp)] = blk.astype(o_ref.dtype)

    buf_a, buf_b = (sla_ref, sca_ref), (slb_ref, scb_ref)
    m0 = put(buf_a, 0)

    def body(i, carry):
        m_a, acc_prev = carry
        m_b = put(buf_b, 2 * i + 1)
        acc_a = pv(buf_a, 2 * i, m_a)
        store(acc_prev, jnp.maximum(2 * i - 1, 0))
        m_next = put(buf_a, 2 * i + 2)
        acc_b = pv(buf_b, 2 * i + 1, m_b)
        store(acc_a, 2 * i)
        return m_next, acc_b

    m_a, acc_prev = lax.fori_loop(0, npairs // 2 - 1, body, (m0, jnp.ones((LANES + 8, 4 * W), F32)))
    m_b = put(buf_b, npairs - 1)
    acc_a = pv(buf_a, npairs - 2, m_a)
    store(acc_prev, npairs - 3)
    acc_b = pv(buf_b, npairs - 1, m_b)
    store(acc_a, npairs - 2)
    store(acc_b, npairs - 1)


def _na_bias_table(rpb):
    H = rpb.shape[0]
    W, wr, wc = GRID_W, NA_WIN_ROWS, NA_WIN_COLS
    qcol = np.arange(W)[:, None]
    kcol = np.arange(W)[None, :]
    ws = np.clip(qcol - wc // 2, 0, W - wc)
    ok = (kcol >= ws) & (kcol < ws + wc)
    r_pad = jnp.pad(rpb.astype(F32), ((0, 0), (0, 0), (W - wc, W + wc - (2 * wc - 1))))
    skew = jnp.tile(r_pad, (1, 1, W))[:, :, :W * (2 * W - 1)].reshape(H, 2 * wr - 1, W, 2 * W - 1)
    t15 = jnp.where(ok[None, None], skew[:, :, :, W - 1:] * LOG2E, NEG_INF)
    t = t15.reshape(H // 2, 2, 2 * wr - 1, W, W).transpose(0, 2, 4, 1, 3).reshape(H // 2, 2 * wr - 1, W, 2 * W)
    return jnp.concatenate([t, jnp.full((H // 2, 1, W, 2 * W), NEG_INF, F32)], axis=1)


def _na_attention(qkv, vt, bias, B, L, Lc):
    D = qkv.shape[1] // 3
    W, wr, fr = GRID_W, NA_WIN_ROWS, NA_FRAME_ROWS
    nrows = L // W
    assert nrows % 2 == 0 and nrows >= fr
    frame0 = lambda g: jnp.minimum(jnp.clip(2 * g - wr // 2, 0, nrows - wr), nrows - fr)
    ctx_blk0 = B * L // Lc
    return pl.pallas_call(
        functools.partial(_na_kernel, heads=NA_HEADS, nrows=nrows),
        grid=(B, nrows // 2),
        in_specs=[pl.BlockSpec((2 * W, D), lambda b, g: (b * (nrows // 2) + g, 0)),
                  pl.BlockSpec((pl.Element(fr * W), pl.Element(D)),
                               lambda b, g: ((b * nrows + frame0(g)) * W, D)),
                  pl.BlockSpec((pl.Element(D), pl.Element(fr * W)),
                               lambda b, g: (0, pl.multiple_of((b * nrows + frame0(g)) * W, 2 * W))),
                  pl.BlockSpec((Lc, D), lambda b, g: (ctx_blk0 + b, 1)),
                  pl.BlockSpec((D, Lc), lambda b, g: (0, ctx_blk0 + b)),
                  _const_spec(bias.shape)],
        out_specs=pl.BlockSpec((2 * W, D), lambda b, g: (b * (nrows // 2) + g, 0)),
        out_shape=jax.ShapeDtypeStruct((B * L, D), BF16),
        scratch_shapes=[pltpu.VMEM((fr * W, 4 * W), F32)] * 2 + [pltpu.VMEM((Lc, 4 * W), F32)] * 2,
        compiler_params=_cparams(("arbitrary", "arbitrary")),
        name="na_attn",
    )(qkv, qkv, vt, qkv, vt, bias)


def _ctx_attn_kernel(q_ref, k_ref, v_ref, o_ref, *, heads, split64, q_scale):
    lo = _lane_lo((q_ref.shape[0], LANES))
    for hp in range(heads // 2):
        vsl = slice(hp * LANES, (hp + 1) * LANES)
        v2 = v_ref[:, vsl]
        outs = []
        for e in range(2):
            if split64:
                q2 = q_ref[:, vsl] * q_scale
                qh = jnp.where(lo if e == 0 else ~lo, q2, jnp.zeros_like(q2))
                kh = k_ref[:, vsl]
            else:
                hsl = slice((2 * hp + e) * LANES, (2 * hp + e + 1) * LANES)
                qh, kh = q_ref[:, hsl], k_ref[:, hsl]
            outs.append(_softmax_pv([_dot_nt(qh, kh)], [v2]))
        o_ref[:, vsl] = jnp.where(lo, outs[0], outs[1]).astype(o_ref.dtype)


def _ctx_attention(q_src, k_src, v_src, B, L, Lc, heads, split64, q_scale=1.0):
    blk0 = B * L // Lc
    spec = lambda src: pl.BlockSpec((Lc, src[1]), lambda b: (blk0 + b, src[2]))
    Dv = v_src[1]
    return pl.pallas_call(
        functools.partial(_ctx_attn_kernel, heads=heads, split64=split64, q_scale=q_scale),
        grid=(B,),
        in_specs=[spec(q_src), spec(k_src), spec(v_src)],
        out_specs=pl.BlockSpec((Lc, Dv), lambda b: (b, 0)),
        out_shape=jax.ShapeDtypeStruct((B * Lc, Dv), BF16),
        compiler_params=_cparams(("arbitrary",)),
        name="ctx_attn",
    )(q_src[0], k_src[0], v_src[0])


def _rope_slot(x, cos, sina, sinb):
    return x * cos + pltpu.roll(x, LANES - 16, axis=1) * sina + pltpu.roll(x, 16, axis=1) * sinb


def _mla_proj_kernel(x_ref, mod_ref, wd_ref, qn_ref, kvn_ref, wq_ref, wk_ref, wvt_ref, vone_ref,
                     cos_ref, sina_ref, sinb_ref, q_ref, k_ref, vt_ref, *, heads, scale):
    sh = mod_ref[0, 0:1, :]
    sc = mod_ref[0, 1:2, :]
    a = (x_ref[...] * (1.0 + sc) + sh).astype(BF16)
    d = _dot(a, wd_ref[...])
    cq = d[:, :MLA_Q_LORA]
    ckv = d[:, MLA_Q_LORA:MLA_Q_LORA + MLA_KV_LORA]
    kr = d[:, MLA_Q_LORA + MLA_KV_LORA:]
    rms = lambda t, g: (t * lax.rsqrt(jnp.mean(t * t, axis=-1, keepdims=True) + RMS_EPS) * g).astype(BF16)
    cqn = rms(cq, qn_ref[...])
    ckvn = rms(ckv, kvn_ref[...])
    cos, sina, sinb = cos_ref[...], sina_ref[...], sinb_ref[...]
    krr = _rope_slot(kr, cos, sina, sinb)
    qf = _dot(cqn, wq_ref[...])
    kf = _dot(ckvn, wk_ref[...])
    for h in range(heads):
        sl = slice(h * LANES, (h + 1) * LANES)
        q_ref[:, sl] = (_rope_slot(qf[:, sl], cos, sina, sinb) * scale).astype(q_ref.dtype)
        k_ref[:, sl] = (kf[:, sl] + krr).astype(k_ref.dtype)
    vt_ref[...] = (_dot_nt(wvt_ref[...], ckvn) + vone_ref[...]).astype(vt_ref.dtype)


def _mla_proj(rows, x, mod, wd, qn, kvn, wq, wk, wvt, vone, tables):
    R, D = x.shape
    tm = rows.tm
    H = MLA_HEADS
    tab_spec = pl.BlockSpec((tm, LANES), lambda i: (rows.pos_block(i), 0))
    return pl.pallas_call(
        functools.partial(_mla_proj_kernel, heads=H,
                          scale=(MLA_NOPE + MLA_ROPE) ** -0.5 * float(np.log2(np.e))),
        grid=(rows.n_all,),
        in_specs=[pl.BlockSpec((tm, D), lambda i: (i, 0)),
                  pl.BlockSpec((1, 6, D), lambda i: (rows.group(i), 0, 0)),
                  _const_spec(wd.shape), _const_spec((1, MLA_Q_LORA)), _const_spec((1, MLA_KV_LORA)),
                  _const_spec(wq.shape), _const_spec(wk.shape), _const_spec(wvt.shape),
                  _const_spec(vone.shape), tab_spec, tab_spec, tab_spec],
        out_specs=[pl.BlockSpec((tm, H * LANES), lambda i: (i, 0)),
                   pl.BlockSpec((tm, H * LANES), lambda i: (i, 0)),
                   pl.BlockSpec((H * LANES, tm), lambda i: (0, i))],
        out_shape=[jax.ShapeDtypeStruct((R, H * LANES), BF16),
                   jax.ShapeDtypeStruct((R, H * LANES), BF16),
                   jax.ShapeDtypeStruct((H * LANES, R), BF16)],
        compiler_params=_cparams(("arbitrary",)),
        name="mla_proj",
    )(x, mod, wd, qn.reshape(1, -1), kvn.reshape(1, -1), wq, wk, wvt, vone, *tables)


def _mla_flash_kernel(q_ref, kc_ref, vtc_ref, *rest, tk, cpi, with_latent):
    if with_latent:
        k_ref, vt_ref, o_ref = rest[:3]
        bufs = rest[3:]
    else:
        (o_ref,) = rest
    tq = q_ref.shape[0]
    hsl = [slice(e * LANES, (e + 1) * LANES) for e in range(2)]
    qs = [q_ref[:, sl] for sl in hsl]

    def scores(kblk):
        return [_dot_nt(kblk[:, hsl[e]], qs[e]) for e in range(2)]

    vrows = MLA_V + 8
    qc = min(tq, 512)

    def colmax(ss):
        return [jnp.max(s, axis=0, keepdims=True) for s in ss]

    def update(ss, cms, vtblk, state):
        out = []
        for e in range(2):
            m, acc = state[e]
            m_new = jnp.maximum(m, cms[e])
            alpha = jnp.exp2(m - m_new)
            vte = vtblk[e * LANES:e * LANES + vrows, :]
            parts = []
            for j in range(0, tq, qc):
                sl = slice(j, j + qc)
                p = jnp.exp2(ss[e][:, sl] - m_new[:, sl]).astype(BF16)
                parts.append(alpha[:, sl] * acc[:, sl] + _dot(vte, p))
            acc = parts[0] if len(parts) == 1 else jnp.concatenate(parts, axis=1)
            out.append((m_new, acc))
        return tuple(out)

    init = (jnp.full((1, tq), NEG_INF, F32), jnp.zeros((vrows, tq), F32))
    s_ctx = scores(kc_ref[...])
    if not with_latent:
        state = update(s_ctx, colmax(s_ctx), vtc_ref[...], (init, init))
    else:
        n = k_ref.shape[0] // tk
        nb = len(bufs)
        assert cpi % nb == 0 and n % cpi == 0

        def put(buf, c):
            ss = scores(k_ref[pl.ds(pl.multiple_of(c * tk, tk), tk), :])
            for e in range(2):
                buf[e] = ss[e]
            return tuple(colmax(ss))

        def group(c0, carry, last):
            st, cms = carry
            for t in range(cpi):
                nxt = None
                if not (last and t == cpi - 1):
                    nxt = put(bufs[(t + 1) % nb], c0 + t + 1)
                buf = bufs[t % nb]
                vtblk = vt_ref[:, pl.ds(pl.multiple_of((c0 + t) * tk, tk), tk)]
                st = update([buf.at[0], buf.at[1]], cms, vtblk, st)
                cms = nxt
            return st, cms

        cms = put(bufs[0], 0)
        state = update(s_ctx, colmax(s_ctx), vtc_ref[...], (init, init))
        carry = lax.fori_loop(0, n // cpi - 1, lambda i, cr: group(i * cpi, cr, False), (state, cms))
        state, _ = group(n - cpi, carry, True)
    ot = jnp.concatenate([acc[:MLA_V] / acc[MLA_V:MLA_V + 1] for _, acc in state], axis=0)
    o_ref[...] = ot.T.astype(o_ref.dtype)


def _mla_attention(q, k, vt, B, L, Lc, tq, tk, latent_queries):
    H = MLA_HEADS
    ctx_blk0 = B * L // Lc
    if latent_queries:
        nq, q0, n_out = L // tq, 0, B * L
    else:
        assert tq == Lc
        nq, q0, n_out = 1, ctx_blk0, B * Lc
    in_specs = [pl.BlockSpec((tq, 2 * LANES), lambda b, hp, i: (q0 + b * nq + i, hp)),
                pl.BlockSpec((Lc, 2 * LANES), lambda b, hp, i: (ctx_blk0 + b, hp)),
                pl.BlockSpec((2 * LANES, Lc), lambda b, hp, i: (hp, ctx_blk0 + b))]
    args = [q, k, vt]
    scratch = []
    if latent_queries:
        in_specs += [pl.BlockSpec((L, 2 * LANES), lambda b, hp, i: (b, hp)),
                     pl.BlockSpec((2 * LANES, L), lambda b, hp, i: (hp, b))]
        args += [k, vt]
        scratch = [pltpu.VMEM((2, tk, tq), F32)] * 2
    return pl.pallas_call(
        functools.partial(_mla_flash_kernel, tk=tk, cpi=4 if (L // tk) % 4 == 0 and L // tk > 8 else 2,
                          with_latent=latent_queries),
        grid=(B, H // 2, nq),
        in_specs=in_specs,
        out_specs=pl.BlockSpec((tq, LANES), lambda b, hp, i: (b * nq + i, hp)),
        out_shape=jax.ShapeDtypeStruct((n_out, H * MLA_V), BF16),
        scratch_shapes=scratch,
        compiler_params=_cparams(("arbitrary", "arbitrary", "arbitrary")),
        name="mla_flash" if latent_queries else "mla_flash_ctx",
    )(*args)


def _axial_angles(L, rot_dim):
    t = jnp.arange(L)
    rows = (t // GRID_W).astype(F32)
    cols = (t % GRID_W).astype(F32)
    n_freq = rot_dim // 4
    inv = ROPE_BASE ** (-jnp.arange(n_freq, dtype=F32) / n_freq)
    return jnp.concatenate([rows[:, None] * inv, cols[:, None] * inv], -1)


def _ret_rope_tables(L, tm):
    ang = _axial_angles(L, 256)
    cos = jnp.concatenate([jnp.cos(ang), jnp.ones((tm, LANES), F32)], 0)
    sin = jnp.concatenate([jnp.sin(ang), jnp.zeros((tm, LANES), F32)], 0)
    return cos, sin


def _mla_rope_tables(L, tm):
    ang = _axial_angles(L, MLA_ROPE)
    c, s = jnp.cos(ang), jnp.sin(ang)
    one = jnp.ones((L, MLA_NOPE), F32)
    z16 = jnp.zeros((L, 16), F32)
    z32 = jnp.zeros((L, 32), F32)
    z64 = jnp.zeros((L, MLA_NOPE), F32)
    cos = jnp.concatenate([one, c, c, jnp.ones((L, 32), F32)], -1)
    sina = jnp.concatenate([z64, -s, z16, z32], -1)
    sinb = jnp.concatenate([z64, z16, s, z32], -1)
    ident = lambda t, fill: jnp.concatenate([t, jnp.full((tm, LANES), fill, F32)], 0)
    return ident(cos, 1.0), ident(sina, 0.0), ident(sinb, 0.0)


def kernel(x, c, ctx, c_ctx, ada_w, ada_b, ln_g, ln_b, ffn_w13, ffn_w2, ret_w_in, ret_decay, ret_w_out, na_w_qkv, na_rpb, na_w_out, mla_w_down, mla_q_norm, mla_kv_norm, mla_w_uq, mla_w_ukv, mla_w_out, hg_w_in, hg_lower_bounds, hg_norm_g, hg_w_out):
    B, L, D = x.shape
    Lc = ctx.shape[1]
    depth = ada_w.shape[0]
    alpha = (2 * depth) ** 0.25
    tm = 512 if (B * Lc) % 512 == 0 else 256
    rows = _Rows(B, L, Lc, tm)
    n_lat_rows = B * L

    h = (x.reshape(B * L, D), ctx.reshape(B * Lc, D))

    G = 8 * (-(-(B + 1) // 8))
    cond_in = jnp.zeros((G, D), F32).at[0].set(c_ctx).at[1:B + 1].set(c)
    mods = _adaln(cond_in, ada_w, ada_b).reshape(depth, G, 6, D)

    bf = lambda w: w.astype(BF16)
    H = MLA_HEADS
    wd = jnp.zeros((D, MLA_Q_LORA + MLA_KV_LORA + LANES), F32)
    wd = wd.at[:, :MLA_Q_LORA + MLA_KV_LORA].set(mla_w_down[:, :MLA_Q_LORA + MLA_KV_LORA])
    wd = wd.at[:, MLA_Q_LORA + MLA_KV_LORA + MLA_NOPE:MLA_Q_LORA + MLA_KV_LORA + MLA_NOPE + MLA_ROPE].set(
        mla_w_down[:, MLA_Q_LORA + MLA_KV_LORA:])
    wq = jnp.pad(mla_w_uq.reshape(MLA_Q_LORA, H, MLA_NOPE + MLA_ROPE),
                 ((0, 0), (0, 0), (0, LANES - MLA_NOPE - MLA_ROPE))).reshape(MLA_Q_LORA, H * LANES)
    wukv = mla_w_ukv.reshape(MLA_KV_LORA, H, MLA_NOPE + MLA_V)
    wk = jnp.pad(wukv[:, :, :MLA_NOPE], ((0, 0), (0, 0), (0, LANES - MLA_NOPE))).reshape(MLA_KV_LORA, H * LANES)
    wvt = jnp.pad(wukv[:, :, MLA_NOPE:], ((0, 0), (0, 0), (0, LANES - MLA_V))).reshape(MLA_KV_LORA, H * LANES).T
    vone = jnp.tile((jnp.arange(LANES) == MLA_V).astype(F32), H).reshape(H * LANES, 1)

    plain = lambda n, width=1024: [(c0, width, "plain", 0, c0, 1.0, None) for c0 in range(0, n, width)]
    for i in range(depth):
        mod = mods[i]
        kind = i % 4
        n_tiles = rows.n_all if i < depth - 1 else rows.n_lat
        post = functools.partial(_post, rows, h, mod, ln_g=ln_g[i], ln_b=ln_b[i], w13=bf(ffn_w13[i]),
                                 w2=bf(ffn_w2[i]), alpha=alpha, n_tiles=n_tiles)
        if kind == 0:
            lg = -jnp.exp(ret_decay.astype(F32))
            plan = [(0, 1024, "rope", 0, 0, 1.0, None), (1024, 1024, "rope", 0, 1024, 256 ** -0.5, None)]
            plan += plain(6144)[2:]
            (proj,) = _proj(rows, h, mod, bf(ret_w_in), plan, [(6144, BF16, False)],
                            rope=_ret_rope_tables(L, tm))
            o_f, o_b = _ret_scan(proj, lg, B, L, Lc)
            h = post(mix=(o_f, o_b, (proj, 2)), w_out=bf(ret_w_out), scan_heads=RET_HEADS)
        elif kind == 1:
            plan = plain(2048) + [(2048, 1024, "plain", 0, 2048, 1.0, 1)]
            qkv, vt = _proj(rows, h, mod, bf(na_w_qkv), plan, [(3072, BF16, False), (1024, BF16, True)])
            o_lat = _na_attention(qkv, vt, _na_bias_table(na_rpb), B, L, Lc)
            o_ctx = _ctx_attention((qkv, D, 0), (qkv, D, 1), (qkv, D, 2), B, L, Lc, NA_HEADS,
                                   split64=True, q_scale=0.125)
            h = post(mix=(o_lat, o_ctx), w_out=bf(na_w_out))
        elif kind == 2:
            q, k, vt = _mla_proj(rows, h, mod, bf(wd), mla_q_norm, mla_kv_norm, bf(wq), bf(wk), bf(wvt), vone,
                                 _mla_rope_tables(L, tm))
            o_lat = _mla_attention(q, k, vt, B, L, Lc, min(2048, L), 512, latent_queries=True)
            o_ctx = _mla_attention(q, k, vt, B, L, Lc, Lc, 256, latent_queries=False)
            h = post(mix=(o_lat, o_ctx), w_out=bf(mla_w_out))
        else:
            lb_soft = jax.nn.softmax(hg_lower_bounds.astype(F32), axis=0)
            lb = (jnp.cumsum(lb_soft, axis=0) - lb_soft[0])[i]
            plan = [(0, 1024, "plain", 0, 0, 1.0, None), (1024, 1024, "plain", 1, 0, 1.0, None),
                    (2048, 1024, "plain", 1, 1024, 1.0, None), (3072, 1024, "plain", 0, 1024, 1.0, None),
                    (4096, 1024, "plain", 0, 2048, 1.0, None)]
            qig, ff = _proj(rows, h, mod, bf(hg_w_in), plan, [(3072, BF16, False), (2048, F32, False)])
            o_f, o_b = _hg_scan(qig, ff, lb, B, L, Lc)
            h = post(mix=(o_f, o_b, (qig, 2)), w_out=bf(hg_w_out), scan_heads=D // HG_EXPAND,
                     norm_g=hg_norm_g)
    return h[:n_lat_rows].reshape(B, L, D)
```
